```python
import math
import jax, jax.numpy as jnp
from jax import lax
import numpy as np

D_MODEL = 1024
BATCH = 8
SEQ = 2048
DEPTH = 4
DEC_BATCH = 128
DEC_SEQ = 4
PAST_LEN = 16384
PAGE_SIZE = 128

D_A = D_MODEL // 2
K_A = 3
D_B = D_MODEL // 2
K_B = 31
H_C = 4
DH_C = D_MODEL // H_C
D_C = H_C * DH_C
MLSTM_CHUNK = 128
N_MEM = 256
HX = 4
DX = D_MODEL // HX
D_FF = -(-8 * D_MODEL // (3 * 256)) * 256
EPS = 1e-6
IN_SPLITS = (D_A, D_A, D_A, D_B, D_B, D_C, D_C, D_C, D_C, H_C, H_C, D_MODEL, D_MODEL, D_MODEL)
IN_WIDTH = 3 * D_A + 2 * D_B + 4 * D_C + 2 * H_C + 3 * D_MODEL

kernel_name = 'hybrid_conv_mlstm_memxattn_step'


def _split(z, sizes):
    idx, acc = [], 0
    for s in sizes[:-1]:
        acc += s
        idx.append(acc)
    return jnp.split(z, idx, axis=-1)


def rmsnorm(x, g):
    xf = x.astype(jnp.float32)
    y = xf * lax.rsqrt(jnp.mean(xf * xf, axis=-1, keepdims=True) + EPS)
    return (y * g.astype(jnp.float32)).astype(x.dtype)


def layernorm(x, g, b):
    xf = x.astype(jnp.float32)
    mu = jnp.mean(xf, axis=-1, keepdims=True)
    xc = xf - mu
    y = xc * lax.rsqrt(jnp.mean(xc * xc, axis=-1, keepdims=True) + EPS)
    return (y * g.astype(jnp.float32) + b.astype(jnp.float32)).astype(x.dtype)


def causal_dwconv(x, prev, w):
    K, C = w.shape
    xp = jnp.concatenate([prev.astype(x.dtype), x], axis=1)
    y = lax.conv_general_dilated(xp, w[:, None, :].astype(x.dtype), window_strides=(1,), padding='VALID',
                                 dimension_numbers=('NWC', 'WIO', 'NWC'), feature_group_count=C)
    return y, xp[:, xp.shape[1] - (K - 1):]


def mlstm_chunkwise(q, k, v, ig, fg, c0, n0, m0):
    B, T, H, DH = q.shape
    L = math.gcd(T, MLSTM_CHUNK)
    NC = T // L
    logf = jax.nn.log_sigmoid(fg)

    def to_chunks(a):
        return jnp.moveaxis(a.reshape((B, NC, L) + a.shape[2:]), 1, 0)

    qc, kc, vc, ic, lfc = (to_chunks(a) for a in (q, k, v, ig, logf))
    causal = jnp.tril(jnp.ones((L, L), dtype=bool))

    def step(carry, inp):
        c, n, m = carry
        qq, kk, vv, ii, lf = inp
        bt = jnp.moveaxis(jnp.cumsum(lf, axis=1), 1, 2)
        it = jnp.moveaxis(ii, 1, 2)
        dlog = bt[:, :, :, None] - bt[:, :, None, :] + it[:, :, None, :]
        dlog = jnp.where(causal, dlog, -jnp.inf)
        inter = bt + m[:, :, None]
        m_t = jnp.maximum(inter, jnp.max(dlog, axis=-1))
        w_intra = jnp.exp(dlog - m_t[..., None])
        w_inter = jnp.moveaxis(jnp.exp(inter - m_t), 1, 2)
        s = jnp.einsum('bthd,bshd->bhts', qq, kk) * w_intra
        num = jnp.einsum('bhts,bshd->bthd', s, vv) + jnp.einsum('bhed,bthd->bthe', c, qq) * w_inter[..., None]
        den = jnp.moveaxis(jnp.sum(s, axis=-1), 1, 2) + w_inter * jnp.einsum('bhd,bthd->bth', n, qq)
        floor = jnp.exp(-jnp.moveaxis(m_t, 1, 2))
        h = num / jnp.maximum(jnp.abs(den), floor)[..., None]
        m_new = m_t[:, :, -1]
        w_last = jnp.exp(dlog[:, :, -1, :] - m_new[..., None])
        w_prev = jnp.exp(inter[:, :, -1] - m_new)
        c_new = w_prev[..., None, None] * c + jnp.einsum('bhs,bshe,bshd->bhed', w_last, vv, kk)
        n_new = w_prev[..., None] * n + jnp.einsum('bhs,bshd->bhd', w_last, kk)
        return (c_new, n_new, m_new), h

    (c, n, m), hs = lax.scan(step, (c0, n0, m0), (qc, kc, vc, ic, lfc))
    h = jnp.moveaxis(hs, 0, 1).reshape(B, T, H, DH)
    return h, (c, n, m)


def mixer(h, conv_a_prev, conv_b_prev, c0, n0, m0, p):
    B, T, _ = h.shape
    dt = h.dtype
    f32 = jnp.float32
    z = h @ p['w_in']
    a_b, a_c, a_x, b_val, b_gt, q, k, v, o, ig, fg, g_a, g_b, g_c = _split(z, IN_SPLITS)
    a_conv, sa = causal_dwconv(a_c * a_x, conv_a_prev, p['conv_a_w'])
    y_a = (a_b * a_conv) @ p['w_out_a']
    b_conv, sb = causal_dwconv(b_val * jax.nn.sigmoid(b_gt), conv_b_prev, p['conv_b_w'])
    b_conv = b_conv + p['conv_b_b']
    y_b = jax.nn.silu(layernorm(b_conv, p['ln_b_g'], p['ln_b_b'])) @ p['w_out_b']
    gates = jnp.concatenate([ig, fg], axis=-1).astype(f32) + p['b_if'].astype(f32)
    qh = q.reshape(B, T, H_C, DH_C).astype(f32)
    kh = k.reshape(B, T, H_C, DH_C).astype(f32) * (DH_C ** -0.5)
    vh = v.reshape(B, T, H_C, DH_C).astype(f32)
    hc, (c, n, m) = mlstm_chunkwise(qh, kh, vh, gates[..., :H_C], gates[..., H_C:],
                                    c0.astype(f32), n0.astype(f32), m0.astype(f32))
    hc = hc * lax.rsqrt(jnp.mean(hc * hc, axis=-1, keepdims=True) + EPS)
    hc = hc.reshape(B, T, D_C) * p['mlstm_norm_g'].astype(f32)
    y_c = (jax.nn.sigmoid(o) * hc.astype(dt)) @ p['w_out_c']
    u = jax.nn.sigmoid(g_a) * y_a + jax.nn.sigmoid(g_b) * y_b + jax.nn.sigmoid(g_c) * y_c
    return u @ p['w_o'], (sa, sb, c.astype(dt), n.astype(dt), m.astype(dt))


def memory_kv(mem, g, w_kv):
    B, M, _ = mem.shape
    kv = rmsnorm(mem, g) @ w_kv
    k, v = jnp.split(kv, 2, axis=-1)
    return k.reshape(B, M, HX, DX), v.reshape(B, M, HX, DX)


def cross_attn(h, mk, mv, w_q, w_o):
    B, T, _ = h.shape
    q = (h @ w_q).reshape(B, T, HX, DX)
    s = jnp.einsum('bthd,bmhd->bhtm', q, mk.astype(h.dtype)).astype(jnp.float32) * (DX ** -0.5)
    pr = jax.nn.softmax(s, axis=-1).astype(h.dtype)
    out = jnp.einsum('bhtm,bmhd->bthd', pr, mv.astype(h.dtype)).reshape(B, T, HX * DX)
    return out @ w_o


def swiglu(h, w_in, w_out):
    gate, up = jnp.split(h @ w_in, 2, axis=-1)
    return (jax.nn.silu(gate) * up) @ w_out


def layer(x, mk, mv, conv_a_prev, conv_b_prev, c0, n0, m0, p):
    mix, st = mixer(rmsnorm(x, p['norm_mix_g']), conv_a_prev, conv_b_prev, c0, n0, m0, p)
    x = x + mix
    x = x + cross_attn(rmsnorm(x, p['norm_x_g']), mk, mv, p['w_xq'], p['w_xo'])
    x = x + swiglu(rmsnorm(x, p['norm_ffn_g']), p['w_ffn_in'], p['w_ffn_out'])
    return x, st


def setup_inputs(seed: int = 0) -> dict:
    key = jax.random.key(seed)
    ks = iter(jax.random.split(key, 48))

    def nrm(shape, scale):
        return jax.random.normal(next(ks), shape, jnp.float32) * scale

    def gain(shape):
        return 1.0 + nrm(shape, 0.01)

    b_if = jnp.concatenate([nrm((DEPTH, H_C), 0.1),
                            3.0 + 3.0 * jax.random.uniform(next(ks), (DEPTH, H_C), jnp.float32)], axis=-1)
    return {
        'x_prompt': nrm((BATCH, SEQ, D_MODEL), 1.0),
        'x_sample': nrm((DEC_BATCH, DEC_SEQ, D_MODEL), 1.0),
        'state_conv_a': nrm((DEPTH, DEC_BATCH, K_A - 1, D_A), 1.0),
        'state_conv_b': nrm((DEPTH, DEC_BATCH, K_B - 1, D_B), 0.5),
        'state_mlstm_c': nrm((DEPTH, DEC_BATCH, H_C, DH_C, DH_C), 0.3),
        'state_mlstm_n': nrm((DEPTH, DEC_BATCH, H_C, DH_C), 1.0),
        'state_mlstm_m': nrm((DEPTH, DEC_BATCH, H_C), 1.0),
        'cache_mem_k': nrm((DEPTH, DEC_BATCH, N_MEM, HX, DX), 1.0),
        'cache_mem_v': nrm((DEPTH, DEC_BATCH, N_MEM, HX, DX), 1.0),
        'mem_prompt': nrm((BATCH, N_MEM, D_MODEL), 1.0),
        'norm_mix_g': gain((DEPTH, D_MODEL)),
        'w_in': nrm((DEPTH, D_MODEL, IN_WIDTH), D_MODEL ** -0.5),
        'b_if': b_if,
        'conv_a_w': nrm((DEPTH, K_A, D_A), K_A ** -0.5),
        'w_out_a': nrm((DEPTH, D_A, D_MODEL), D_A ** -0.5),
        'conv_b_w': nrm((DEPTH, K_B, D_B), K_B ** -0.5),
        'conv_b_b': nrm((DEPTH, D_B), 0.01),
        'ln_b_g': gain((DEPTH, D_B)),
        'ln_b_b': nrm((DEPTH, D_B), 0.01),
        'w_out_b': nrm((DEPTH, D_B, D_MODEL), D_B ** -0.5),
        'mlstm_norm_g': gain((DEPTH, D_C)),
        'w_out_c': nrm((DEPTH, D_C, D_MODEL), D_C ** -0.5),
        'w_o': nrm((DEPTH, D_MODEL, D_MODEL), D_MODEL ** -0.5),
        'norm_x_g': gain((DEPTH, D_MODEL)),
        'norm_mem_g': gain((DEPTH, D_MODEL)),
        'w_xq': nrm((DEPTH, D_MODEL, HX * DX), D_MODEL ** -0.5),
        'w_xkv': nrm((DEPTH, D_MODEL, 2 * HX * DX), D_MODEL ** -0.5),
        'w_xo': nrm((DEPTH, HX * DX, D_MODEL), (HX * DX) ** -0.5),
        'norm_ffn_g': gain((DEPTH, D_MODEL)),
        'w_ffn_in': nrm((DEPTH, D_MODEL, 2 * D_FF), D_MODEL ** -0.5),
        'w_ffn_out': nrm((DEPTH, D_FF, D_MODEL), D_FF ** -0.5),
        'final_norm_g': gain((D_MODEL,)),
    }


def reference(x_prompt, x_sample, state_conv_a, state_conv_b, state_mlstm_c, state_mlstm_n, state_mlstm_m,
              cache_mem_k, cache_mem_v, mem_prompt, norm_mix_g, w_in, b_if, conv_a_w, w_out_a, conv_b_w,
              conv_b_b, ln_b_g, ln_b_b, w_out_b, mlstm_norm_g, w_out_c, w_o, norm_x_g, norm_mem_g, w_xq,
              w_xkv, w_xo, norm_ffn_g, w_ffn_in, w_ffn_out, final_norm_g):
    bp = x_prompt.shape[0]
    dt = x_prompt.dtype
    za = jnp.zeros((bp, K_A - 1, D_A), dt)
    zb = jnp.zeros((bp, K_B - 1, D_B), dt)
    zc = jnp.zeros((bp, H_C, DH_C, DH_C), jnp.float32)
    zn = jnp.zeros((bp, H_C, DH_C), jnp.float32)
    zm = jnp.zeros((bp, H_C), jnp.float32)
    xp, xs = x_prompt, x_sample
    pa, pb, pc, pn, pm, pk, pv = [], [], [], [], [], [], []
    sa, sb, sc, sn, sm = [], [], [], [], []
    for l in range(DEPTH):
        p = {'norm_mix_g': norm_mix_g[l], 'w_in': w_in[l], 'b_if': b_if[l], 'conv_a_w': conv_a_w[l],
             'w_out_a': w_out_a[l], 'conv_b_w': conv_b_w[l], 'conv_b_b': conv_b_b[l], 'ln_b_g': ln_b_g[l],
             'ln_b_b': ln_b_b[l], 'w_out_b': w_out_b[l], 'mlstm_norm_g': mlstm_norm_g[l], 'w_out_c': w_out_c[l],
             'w_o': w_o[l], 'norm_x_g': norm_x_g[l], 'w_xq': w_xq[l], 'w_xo': w_xo[l],
             'norm_ffn_g': norm_ffn_g[l], 'w_ffn_in': w_ffn_in[l], 'w_ffn_out': w_ffn_out[l]}
        mk, mv = memory_kv(mem_prompt, norm_mem_g[l], w_xkv[l])
        xp, (a1, b1, c1, n1, m1) = layer(xp, mk, mv, za, zb, zc, zn, zm, p)
        pa.append(a1); pb.append(b1); pc.append(c1); pn.append(n1); pm.append(m1)
        pk.append(mk); pv.append(mv)
        xs, (a2, b2, c2, n2, m2) = layer(xs, cache_mem_k[l], cache_mem_v[l], state_conv_a[l], state_conv_b[l],
                                        state_mlstm_c[l], state_mlstm_n[l], state_mlstm_m[l], p)
        sa.append(a2); sb.append(b2); sc.append(c2); sn.append(n2); sm.append(m2)
    y_prompt = rmsnorm(xp, final_norm_g)
    y_sample = rmsnorm(xs, final_norm_g)
    return (y_prompt, y_sample,
            jnp.stack(pa), jnp.stack(pb), jnp.stack(pc), jnp.stack(pn), jnp.stack(pm),
            jnp.stack(pk), jnp.stack(pv),
            jnp.stack(sa), jnp.stack(sb), jnp.stack(sc), jnp.stack(sn), jnp.stack(sm))
```

```python
import functools

import jax
import jax.numpy as jnp
from jax import lax
from jax.experimental import pallas as pl
from jax.experimental.pallas import tpu as pltpu

D_MODEL = 1024
DEPTH = 4
D_A = 512
K_A = 3
D_B = 512
K_B = 31
H_C = 4
DH_C = 256
N_MEM = 256
HX = 4
DX = 256
D_FF = 2816
EPS = 1e-6

OFF_AB, OFF_AC, OFF_AX, OFF_BV, OFF_BG = 0, 512, 1024, 1536, 2048
OFF_Q, OFF_K, OFF_V, OFF_O = 2560, 3584, 4608, 5632
W_MAIN = 6656
W_MERGE = 3 * D_MODEL
W_GATE = 128
W_P = 2 * D_A + D_MODEL

CHUNK = 128
ROW_TILE = 512
VMEM_LIMIT_BYTES = 56 * 1024 * 1024

F32 = jnp.float32
BF16 = jnp.bfloat16


def _dot(a, b):
    return jnp.dot(a, b, preferred_element_type=F32)


def _dot_nt(a, b):
    return lax.dot_general(a, b, (((1,), (1,)), ((), ())), preferred_element_type=F32)


def _dot_f32(a, b):
    return jnp.dot(a, b, preferred_element_type=F32, precision=lax.Precision.HIGHEST)


def _rmsnorm(x, g):
    ms = jnp.mean(x * x, axis=-1, keepdims=True)
    return x * lax.rsqrt(ms + EPS) * g


def _resident(shape):
    nd = len(shape)
    return pl.BlockSpec(shape, lambda *_: (0,) * nd, pipeline_mode=pl.Buffered(1))


def _params(sem):
    return pltpu.CompilerParams(dimension_semantics=sem, vmem_limit_bytes=VMEM_LIMIT_BYTES)


def _inproj_kernel(x_ref, g_ref, wm_ref, wg_ref, wgate_ref, zm_ref, zg_ref, gates_ref):
    h = _rmsnorm(x_ref[...], g_ref[...]).astype(BF16)
    for j in range(W_MAIN // 512):
        sl = slice(j * 512, (j + 1) * 512)
        zm_ref[:, sl] = _dot(h, wm_ref[:, sl]).astype(BF16)
    for j in range(W_MERGE // 512):
        sl = slice(j * 512, (j + 1) * 512)
        zg_ref[:, sl] = _dot(h, wg_ref[:, sl]).astype(BF16)
    gates_ref[...] = _dot(h, wgate_ref[...])


def _inproj(x, g, wm, wg, wgate):
    n = x.shape[0]
    tm = 256
    return pl.pallas_call(
        _inproj_kernel,
        grid=(n // tm,),
        in_specs=[
            pl.BlockSpec((tm, D_MODEL), lambda i: (i, 0)),
            _resident((1, D_MODEL)),
            _resident((D_MODEL, W_MAIN)),
            _resident((D_MODEL, W_MERGE)),
            _resident((D_MODEL, W_GATE)),
        ],
        out_specs=[
            pl.BlockSpec((tm, W_MAIN), lambda i: (i, 0)),
            pl.BlockSpec((tm, W_MERGE), lambda i: (i, 0)),
            pl.BlockSpec((tm, W_GATE), lambda i: (i, 0)),
        ],
        out_shape=[
            jax.ShapeDtypeStruct((n, W_MAIN), BF16),
            jax.ShapeDtypeStruct((n, W_MERGE), BF16),
            jax.ShapeDtypeStruct((n, W_GATE), F32),
        ],
        compiler_params=_params(("parallel",)),
        name="inproj",
    )(x, g, wm, wg, wgate)


def _mixer_kernel(zm_ref, gates_ref, ca0_ref, cb0_ref, c0_ref, n0_ref, m0_ref,
                  caw_ref, cbw_ref, cbb_ref, lng_ref, lnb_ref, bif_ref, mng_ref,
                  p_ref, sa_ref, sb_ref, c_out_ref, n_out_ref, m_out_ref,
                  xa_s, xb_s, aconv_s, bconv_s, c_s, n_s, m_s,
                  num_s, qc_s, vwt_s, winter_s, rinv_s, wlast_s, wprev_s, mnew_s,
                  *, seq_rows, carry):
    L = CHUNK
    nseq = L // seq_rows
    t = pl.program_id(1)
    j = pl.program_id(2)

    @pl.when(j == 0)
    def _pre():
        a_b = zm_ref[:, OFF_AB:OFF_AB + D_A].astype(F32)
        ca = zm_ref[:, OFF_AC:OFF_AC + D_A].astype(F32) * zm_ref[:, OFF_AX:OFF_AX + D_A].astype(F32)
        cb = zm_ref[:, OFF_BV:OFF_BV + D_B].astype(F32) * jax.nn.sigmoid(
            zm_ref[:, OFF_BG:OFF_BG + D_B].astype(F32))

        if carry:
            @pl.when(t == 0)
            def _():
                xa_s[0:8, :] = jnp.zeros((8, D_A), F32)
                xb_s[0:32, :] = jnp.zeros((32, D_B), F32)
                xa_s[6:8, :] = ca0_ref[0]
                xb_s[2:32, :] = cb0_ref[0]
                c_s[...] = c0_ref[0]
                n_s[0:1, :] = n0_ref[0]
                m_s[0:1, :] = m0_ref[0]

            xa_s[8:8 + L, :] = ca
            xb_s[32:32 + L, :] = cb
            for cblk in range(D_A // 128):
                cs = slice(cblk * 128, (cblk + 1) * 128)
                acc = caw_ref[0:1, cs] * xa_s[6:6 + L, cs]
                for k in range(1, K_A):
                    acc = acc + caw_ref[k:k + 1, cs] * xa_s[6 + k:6 + k + L, cs]
                aconv_s[:, cs] = acc
                acc = cbw_ref[0:1, cs] * xb_s[2:2 + L, cs]
                for k in range(1, K_B):
                    acc = acc + cbw_ref[k:k + 1, cs] * xb_s[2 + k:2 + k + L, cs]
                bconv_s[:, cs] = acc
            sa_new = xa_s[6 + L:8 + L, :]
            sb_new = xb_s[2 + L:32 + L, :]
            xa_s[6:8, :] = sa_new
            xb_s[2:32, :] = sb_new
            sa_ref[0] = sa_new
            sb_ref[0] = sb_new
        else:
            xa_s[...] = jnp.zeros(xa_s.shape, F32)
            xb_s[...] = jnp.zeros(xb_s.shape, F32)
            aconv_s[...] = ca
            bconv_s[...] = cb
            for b in range(nseq):
                r0 = b * seq_rows
                xa_s[6:8, :] = ca0_ref[b]
                xa_s[8:8 + seq_rows, :] = aconv_s[r0:r0 + seq_rows, :]
                xb_s[2:32, :] = cb0_ref[b]
                xb_s[32:32 + seq_rows, :] = bconv_s[r0:r0 + seq_rows, :]
                acc = caw_ref[0:1, :] * xa_s[6:14, :]
                for k in range(1, K_A):
                    acc = acc + caw_ref[k:k + 1, :] * xa_s[6 + k:14 + k, :]
                sa_ref[b] = xa_s[6 + seq_rows:8 + seq_rows, :]
                aconv_s[r0:r0 + seq_rows, :] = acc[0:seq_rows]
                acc = cbw_ref[0:1, :] * xb_s[2:10, :]
                for k in range(1, K_B):
                    acc = acc + cbw_ref[k:k + 1, :] * xb_s[2 + k:10 + k, :]
                sb_ref[b] = xb_s[2 + seq_rows:32 + seq_rows, :]
                bconv_s[r0:r0 + seq_rows, :] = acc[0:seq_rows]

        p_ref[:, 0:D_A] = (a_b * aconv_s[...]).astype(BF16)
        bc = bconv_s[...] + cbb_ref[...]
        mu = jnp.mean(bc, axis=-1, keepdims=True)
        xc = bc - mu
        ln = xc * lax.rsqrt(jnp.mean(xc * xc, axis=-1, keepdims=True) + EPS) * lng_ref[...] + lnb_ref[...]
        p_ref[:, D_A:D_A + D_B] = (ln * jax.nn.sigmoid(ln)).astype(BF16)

        row = lax.broadcasted_iota(jnp.int32, (L, L), 0)
        col = lax.broadcasted_iota(jnp.int32, (L, L), 1)
        same = (row // seq_rows) == (col // seq_rows)
        causal = same & (col <= row)
        g = gates_ref[...] + bif_ref[...]
        logf = jnp.minimum(g, 0.0) - jnp.log1p(jnp.exp(-jnp.abs(g)))
        bt = _dot_f32(causal.astype(F32), logf)
        btl = _dot_f32(same.astype(F32), logf)
        bt_h = pltpu.roll(bt, 128 - H_C, axis=1)
        btl_h = pltpu.roll(btl, 128 - H_C, axis=1)
        g_t = g.T
        bt_t = bt.T
        m_rows = m_s[0:1, :] if carry else m0_ref[0]
        inter = bt_h + m_rows
        lane = lax.broadcasted_iota(jnp.int32, (L, W_GATE), 1)

        dlogs = []
        rmax = jnp.zeros((L, W_GATE), F32)
        for h in range(H_C):
            dlog = jnp.where(causal, bt[:, H_C + h:H_C + h + 1] - bt_t[H_C + h:H_C + h + 1, :]
                             + g_t[h:h + 1, :], -jnp.inf)
            dlogs.append(dlog)
            rmax = jnp.where(lane == h, jnp.max(dlog, axis=-1, keepdims=True), rmax)
        m_t = jnp.maximum(inter, rmax)
        w_inter = jnp.exp(inter - m_t)
        floor = jnp.exp(-m_t)
        last = (col == (row // seq_rows) * seq_rows + (seq_rows - 1)).astype(F32)
        m_new = _dot_f32(last, m_t)
        w_last = jnp.exp(btl_h - bt_h + g - m_new)
        w_prev = jnp.exp(inter - m_new)
        n_rows = n_s[0:1, :] if carry else n0_ref[0]

        den = jnp.zeros((L, W_GATE), F32)
        for h in range(H_C):
            hs = slice(h * DH_C, (h + 1) * DH_C)
            q = zm_ref[:, OFF_Q + h * DH_C:OFF_Q + (h + 1) * DH_C]
            k = zm_ref[:, OFF_K + h * DH_C:OFF_K + (h + 1) * DH_C]
            v = zm_ref[:, OFF_V + h * DH_C:OFF_V + (h + 1) * DH_C]
            s = _dot_nt(q, k) * (DH_C ** -0.5) * jnp.exp(dlogs[h] - m_t[:, h:h + 1])
            num_s[:, hs] = _dot(s.astype(BF16), v)
            qn = jnp.sum(q.astype(F32) * n_rows[:, hs], axis=-1, keepdims=True)
            den_h = jnp.sum(s, axis=-1, keepdims=True) + w_inter[:, h:h + 1] * qn
            den = jnp.where(lane == h, den_h, den)
            vw = v.astype(F32) * w_last[:, h:h + 1]
            vwt_s[h] = vw.T.astype(BF16)
        if nseq > 1:
            qc_s[...] = jnp.zeros(qc_s.shape, F32)
        winter_s[...] = w_inter
        rinv_s[...] = 1.0 / jnp.maximum(jnp.abs(den), floor)
        wlast_s[...] = w_last
        wprev_s[...] = w_prev
        mnew_s[...] = m_new

    tile = 16
    if nseq == 1:
        wprev_row = wprev_s[L - 1:L, :]
        m_out_row = mnew_s[L - 1:L, :]
    else:
        last_row = j * seq_rows + (seq_rows - 1)
        lane_l = lax.broadcasted_iota(jnp.int32, (1, L), 1)
        row_l = lax.broadcasted_iota(jnp.int32, (L, 1), 0)
        seq_lanes = (lane_l // seq_rows) == j
        seq_rows_mask = (row_l // seq_rows) == j
        wprev_row = wprev_s[pl.ds(last_row, 1), :]
        m_out_row = mnew_s[pl.ds(last_row, 1), :]
    for h in range(H_C):
        hs = slice(h * DH_C, (h + 1) * DH_C)
        c_old = c_s[h] if carry else c0_ref[0, h]
        c_bf = c_old.astype(BF16)
        if nseq == 1:
            qc_s[:, hs] = _dot_nt(zm_ref[:, OFF_Q + h * DH_C:OFF_Q + (h + 1) * DH_C], c_bf)
        else:
            r0 = pl.multiple_of((j * seq_rows // tile) * tile, tile)
            q16 = zm_ref[pl.ds(r0, tile), OFF_Q + h * DH_C:OFF_Q + (h + 1) * DH_C]
            r = _dot_nt(q16, c_bf)
            rid = lax.broadcasted_iota(jnp.int32, (tile, 1), 0) + r0
            qc_s[pl.ds(r0, tile), hs] = jnp.where((rid // seq_rows) == j, r, qc_s[pl.ds(r0, tile), hs])
        k = zm_ref[:, OFF_K + h * DH_C:OFF_K + (h + 1) * DH_C]
        if nseq == 1:
            vwt = vwt_s[h]
            wl = wlast_s[:, h:h + 1]
        else:
            vwt = jnp.where(seq_lanes, vwt_s[h], jnp.zeros((DH_C, L), BF16))
            wl = jnp.where(seq_rows_mask, wlast_s[:, h:h + 1], 0.0)
        kv = _dot(vwt, k)
        w_prev = wprev_row[:, h:h + 1]
        c_new = w_prev * c_old + kv * (DH_C ** -0.5)
        ksum = jnp.sum(k.astype(F32) * wl, axis=0, keepdims=True)
        if carry:
            n_old = n_s[0:1, hs]
        else:
            n_old = n0_ref[0, pl.ds(j * seq_rows, 1), hs]
        n_new = w_prev * n_old + ksum * (DH_C ** -0.5)
        if carry:
            c_s[h] = c_new
            n_s[0:1, hs] = n_new
        c_out_ref[0, h] = c_new
        n_out_ref[0, h:h + 1, :] = n_new
    if carry:
        m_s[0:1, :] = m_out_row
    m_out_ref[0] = m_out_row

    @pl.when(j == nseq - 1)
    def _post():
        for h in range(H_C):
            hs = slice(h * DH_C, (h + 1) * DH_C)
            hh = (num_s[:, hs] + qc_s[:, hs] * winter_s[:, h:h + 1]) * rinv_s[:, h:h + 1]
            hn = hh * lax.rsqrt(jnp.mean(hh * hh, axis=-1, keepdims=True) + EPS) * mng_ref[:, hs]
            o = zm_ref[:, OFF_O + h * DH_C:OFF_O + (h + 1) * DH_C].astype(F32)
            p_ref[:, 2 * D_A + h * DH_C:2 * D_A + (h + 1) * DH_C] = (jax.nn.sigmoid(o) * hn).astype(BF16)


def _mixer(zm, gates, ca0, cb0, c0, n0, m0, caw, cbw, cbb, lng, lnb, bif, mng, *,
           row_block0, n_groups, n_chunks, seq_rows, carry):
    L = CHUNK
    nseq = L // seq_rows
    n_state = n_groups if carry else n_chunks * nseq
    rows = n_groups * n_chunks * L

    def tok(b, t, j):
        return (row_block0 + b * n_chunks + t, 0)

    def out_tok(b, t, j):
        return (b * n_chunks + t, 0)

    if carry:
        def st3(b, t, j):
            return (b, 0, 0)

        def st4(b, t, j):
            return (b, 0, 0, 0)
        conv_blk = 1
        st3c = st3
        nrow_spec = pl.BlockSpec((1, 1, D_MODEL), st3)
        mrow_spec = pl.BlockSpec((1, 1, W_GATE), st3)
    else:
        def st3(b, t, j):
            return (t * nseq + j, 0, 0)

        def st4(b, t, j):
            return (t * nseq + j, 0, 0, 0)

        def st3c(b, t, j):
            return (t, 0, 0)
        conv_blk = nseq
        nrow_spec = pl.BlockSpec((1, L, D_MODEL), st3c)
        mrow_spec = pl.BlockSpec((1, L, W_GATE), st3c)

    kern = functools.partial(_mixer_kernel, seq_rows=seq_rows, carry=carry)
    return pl.pallas_call(
        kern,
        grid=(n_groups, n_chunks, nseq),
        in_specs=[
            pl.BlockSpec((L, W_MAIN), tok),
            pl.BlockSpec((L, W_GATE), tok),
            pl.BlockSpec((conv_blk, K_A - 1, D_A), st3c),
            pl.BlockSpec((conv_blk, K_B - 1, D_B), st3c),
            pl.BlockSpec((1, H_C, DH_C, DH_C), st4),
            nrow_spec,
            mrow_spec,
            _resident((K_A, D_A)),
            _resident((K_B, D_B)),
            _resident((1, D_B)),
            _resident((1, D_B)),
            _resident((1, D_B)),
            _resident((1, W_GATE)),
            _resident((1, D_MODEL)),
        ],
        out_specs=[
            pl.BlockSpec((L, W_P), out_tok),
            pl.BlockSpec((conv_blk, K_A - 1, D_A), st3c),
            pl.BlockSpec((conv_blk, K_B - 1, D_B), st3c),
            pl.BlockSpec((1, H_C, DH_C, DH_C), st4),
            pl.BlockSpec((1, H_C, DH_C), st3),
            pl.BlockSpec((1, 1, W_GATE), st3),
        ],
        out_shape=[
            jax.ShapeDtypeStruct((rows, W_P), BF16),
            jax.ShapeDtypeStruct((n_state, K_A - 1, D_A), F32),
            jax.ShapeDtypeStruct((n_state, K_B - 1, D_B), F32),
            jax.ShapeDtypeStruct((n_state, H_C, DH_C, DH_C), F32),
            jax.ShapeDtypeStruct((n_state, H_C, DH_C), F32),
            jax.ShapeDtypeStruct((n_state, 1, W_GATE), F32),
        ],
        scratch_shapes=[
            pltpu.VMEM((8 + L, D_A) if carry else (16, D_A), F32),
            pltpu.VMEM((32 + L, D_B) if carry else (40, D_B), F32),
            pltpu.VMEM((L, D_A), F32),
            pltpu.VMEM((L, D_B), F32),
            pltpu.VMEM((H_C, DH_C, DH_C), F32),
            pltpu.VMEM((8, D_MODEL), F32),
            pltpu.VMEM((8, W_GATE), F32),
            pltpu.VMEM((L, D_MODEL), F32),
            pltpu.VMEM((L, D_MODEL), F32),
            pltpu.VMEM((H_C, DH_C, L), BF16),
            pltpu.VMEM((L, W_GATE), F32),
            pltpu.VMEM((L, W_GATE), F32),
            pltpu.VMEM((L, W_GATE), F32),
            pltpu.VMEM((L, W_GATE), F32),
            pltpu.VMEM((L, W_GATE), F32),
        ],
        compiler_params=_params(("arbitrary", "arbitrary", "arbitrary")),
        name="mixer_prompt" if carry else "mixer_sample",
    )(zm, gates, ca0, cb0, c0, n0, m0, caw, cbw, cbb, lng, lnb, bif, mng)


def _outproj_kernel(x_ref, pp_ref, ps_ref, zg_ref, wa_ref, wb_ref, wc_ref, wo_ref, gx_ref, wq_ref,
                    x1_ref, q_ref, *, n_prompt_tiles):
    i = pl.program_id(0)
    p = jnp.where(i >= n_prompt_tiles, ps_ref[...], pp_ref[...])
    y_a = _dot(p[:, 0:D_A], wa_ref[...])
    y_b = _dot(p[:, D_A:D_A + D_B], wb_ref[...])
    y_c = _dot(p[:, D_A + D_B:], wc_ref[...])
    u = (jax.nn.sigmoid(zg_ref[:, 0:D_MODEL].astype(F32)) * y_a
         + jax.nn.sigmoid(zg_ref[:, D_MODEL:2 * D_MODEL].astype(F32)) * y_b
         + jax.nn.sigmoid(zg_ref[:, 2 * D_MODEL:].astype(F32)) * y_c)
    x1 = x_ref[...] + _dot(u.astype(BF16), wo_ref[...])
    x1_ref[...] = x1
    q_ref[...] = _dot(_rmsnorm(x1, gx_ref[...]).astype(BF16), wq_ref[...]).astype(BF16)


def _outproj(x, p_p, p_s, zg, wa, wb, wc, wo, gx, wq):
    n = x.shape[0]
    tm = ROW_TILE
    npt = p_p.shape[0] // tm
    return pl.pallas_call(
        functools.partial(_outproj_kernel, n_prompt_tiles=npt),
        grid=(n // tm,),
        in_specs=[
            pl.BlockSpec((tm, D_MODEL), lambda i: (i, 0)),
            pl.BlockSpec((tm, W_P), lambda i: (jnp.minimum(i, npt - 1), 0)),
            pl.BlockSpec((tm, W_P), lambda i: (0, 0)),
            pl.BlockSpec((tm, W_MERGE), lambda i: (i, 0)),
            _resident((D_A, D_MODEL)),
            _resident((D_B, D_MODEL)),
            _resident((D_MODEL, D_MODEL)),
            _resident((D_MODEL, D_MODEL)),
            _resident((1, D_MODEL)),
            _resident((D_MODEL, D_MODEL)),
        ],
        out_specs=[
            pl.BlockSpec((tm, D_MODEL), lambda i: (i, 0)),
            pl.BlockSpec((tm, D_MODEL), lambda i: (i, 0)),
        ],
        out_shape=[
            jax.ShapeDtypeStruct((n, D_MODEL), F32),
            jax.ShapeDtypeStruct((n, D_MODEL), BF16),
        ],
        compiler_params=_params(("parallel",)),
        name="outproj",
    )(x, p_p, p_s, zg, wa, wb, wc, wo, gx, wq)


def _memkv_kernel(mem_ref, g_ref, w_ref, k_ref, v_ref, kb_ref, vb_ref):
    h = _rmsnorm(mem_ref[...], g_ref[0]).astype(BF16)
    hd = HX * DX
    k = _dot(h, w_ref[0, :, 0:hd])
    v = _dot(h, w_ref[0, :, hd:])
    k_ref[0] = k
    v_ref[0] = v
    kb_ref[0] = k.astype(BF16)
    vb_ref[0] = v.astype(BF16)


def _memkv(mem, g, w):
    n = mem.shape[0]
    tm = ROW_TILE
    hd = HX * DX
    o_spec = pl.BlockSpec((1, tm, hd), lambda l, i: (l, i, 0))
    return pl.pallas_call(
        _memkv_kernel,
        grid=(DEPTH, n // tm),
        in_specs=[
            pl.BlockSpec((tm, D_MODEL), lambda l, i: (i, 0)),
            pl.BlockSpec((1, 1, D_MODEL), lambda l, i: (l, 0, 0)),
            pl.BlockSpec((1, D_MODEL, 2 * hd), lambda l, i: (l, 0, 0)),
        ],
        out_specs=[o_spec, o_spec, o_spec, o_spec],
        out_shape=[
            jax.ShapeDtypeStruct((DEPTH, n, hd), F32),
            jax.ShapeDtypeStruct((DEPTH, n, hd), F32),
            jax.ShapeDtypeStruct((DEPTH, n, hd), BF16),
            jax.ShapeDtypeStruct((DEPTH, n, hd), BF16),
        ],
        compiler_params=_params(("arbitrary", "arbitrary")),
        name="memkv",
    )(mem, g, w)


def _attend(q, k, v):
    s = _dot_nt(q, k) * (DX ** -0.5)
    e = jnp.exp(s - jnp.max(s, axis=-1, keepdims=True))
    return _dot(e.astype(BF16), v) * (1.0 / jnp.sum(e, axis=-1, keepdims=True))


def _xattn_prompt_kernel(q_ref, k_ref, v_ref, o_ref):
    for h in range(HX):
        hs = slice(h * DX, (h + 1) * DX)
        o_ref[:, hs] = _attend(q_ref[:, hs], k_ref[0, :, hs], v_ref[0, :, hs]).astype(BF16)


def _xattn_prompt(q, kb, vb, layer, *, n_groups, rows_per_group):
    tq = ROW_TILE
    nt = rows_per_group // tq
    hd = HX * DX
    kv_spec = pl.BlockSpec((1, N_MEM, hd), lambda b, t: (layer, b, 0))
    return pl.pallas_call(
        _xattn_prompt_kernel,
        grid=(n_groups, nt),
        in_specs=[pl.BlockSpec((tq, hd), lambda b, t: (b * nt + t, 0)), kv_spec, kv_spec],
        out_specs=pl.BlockSpec((tq, hd), lambda b, t: (b * nt + t, 0)),
        out_shape=jax.ShapeDtypeStruct((n_groups * rows_per_group, hd), BF16),
        compiler_params=_params(("parallel", "parallel")),
        name="xattn_prompt",
    )(q, kb, vb)


def _xattn_sample_kernel(q_ref, k_ref, v_ref, o_ref, *, seq_rows):
    rows = q_ref.shape[0]
    rid = lax.broadcasted_iota(jnp.int32, (rows, 1), 0)
    for h in range(HX):
        hs = slice(h * DX, (h + 1) * DX)
        q = q_ref[:, hs]
        acc = jnp.zeros((rows, DX), F32)
        for e in range(rows // seq_rows):
            o = _attend(q, k_ref[e, :, hs].astype(BF16), v_ref[e, :, hs].astype(BF16))
            acc = jnp.where((rid // seq_rows) == e, o, acc)
        o_ref[:, hs] = acc.astype(BF16)


def _xattn_sample(q, k, v, *, row_block0, n_seq, seq_rows):
    rows = 16
    per = rows // seq_rows
    hd = HX * DX
    kv_spec = pl.BlockSpec((per, N_MEM, hd), lambda i: (i, 0, 0))
    return pl.pallas_call(
        functools.partial(_xattn_sample_kernel, seq_rows=seq_rows),
        grid=(n_seq // per,),
        in_specs=[pl.BlockSpec((rows, hd), lambda i: (row_block0 + i, 0)), kv_spec, kv_spec],
        out_specs=pl.BlockSpec((rows, hd), lambda i: (i, 0)),
        out_shape=jax.ShapeDtypeStruct((n_seq * seq_rows, hd), BF16),
        compiler_params=_params(("parallel",)),
        name="xattn_sample",
    )(q, k, v)


FF_BLOCK = 256


def _ffn_kernel(x_ref, cp_ref, cs_ref, wxo_ref, gf_ref, wg_ref, wu_ref, wout_ref, gfin_ref,
                o_ref, act_s, *, n_prompt_tiles, final):
    i = pl.program_id(0)
    ctx = jnp.where(i >= n_prompt_tiles, cs_ref[...], cp_ref[...])
    x2 = x_ref[...] + _dot(ctx, wxo_ref[...])
    h = _rmsnorm(x2, gf_ref[...]).astype(BF16)
    for jb in range(D_FF // FF_BLOCK):
        sl = slice(jb * FF_BLOCK, (jb + 1) * FF_BLOCK)
        gate = _dot(h, wg_ref[:, sl])
        up = _dot(h, wu_ref[:, sl])
        act_s[:, sl] = (gate * jax.nn.sigmoid(gate) * up).astype(BF16)
    x3 = x2 + _dot(act_s[...], wout_ref[...])
    if final:
        x3 = _rmsnorm(x3, gfin_ref[...])
    o_ref[...] = x3


def _ffn(x, c_p, c_s, wxo, gf, wg, wu, wout, gfin, *, final):
    n = x.shape[0]
    tm = ROW_TILE
    npt = c_p.shape[0] // tm
    return pl.pallas_call(
        functools.partial(_ffn_kernel, n_prompt_tiles=npt, final=final),
        grid=(n // tm,),
        in_specs=[
            pl.BlockSpec((tm, D_MODEL), lambda i: (i, 0)),
            pl.BlockSpec((tm, D_MODEL), lambda i: (jnp.minimum(i, npt - 1), 0)),
            pl.BlockSpec((tm, D_MODEL), lambda i: (0, 0)),
            _resident((D_MODEL, D_MODEL)),
            _resident((1, D_MODEL)),
            _resident((D_MODEL, D_FF)),
            _resident((D_MODEL, D_FF)),
            _resident((D_FF, D_MODEL)),
            _resident((1, D_MODEL)),
        ],
        out_specs=pl.BlockSpec((tm, D_MODEL), lambda i: (i, 0)),
        out_shape=jax.ShapeDtypeStruct((n, D_MODEL), F32),
        scratch_shapes=[pltpu.VMEM((tm, D_FF), BF16)],
        compiler_params=_params(("parallel",)),
        name="ffn",
    )(x, c_p, c_s, wxo, gf, wg, wu, wout, gfin)


def kernel(x_prompt, x_sample, state_conv_a, state_conv_b, state_mlstm_c, state_mlstm_n, state_mlstm_m,
           cache_mem_k, cache_mem_v, mem_prompt, norm_mix_g, w_in, b_if, conv_a_w, w_out_a, conv_b_w,
           conv_b_b, ln_b_g, ln_b_b, w_out_b, mlstm_norm_g, w_out_c, w_o, norm_x_g, norm_mem_g, w_xq,
           w_xkv, w_xo, norm_ffn_g, w_ffn_in, w_ffn_out, final_norm_g):
    bp, tp, d = x_prompt.shape
    bs, ts, _ = x_sample.shape
    n_p = bp * tp
    n_s = bs * ts
    hd = HX * DX

    gate_lo = W_MAIN
    gate_hi = W_MAIN + 2 * H_C
    w_main = w_in[:, :, :gate_lo].astype(BF16)
    w_gate = jnp.pad(w_in[:, :, gate_lo:gate_hi], ((0, 0), (0, 0), (0, W_GATE - 2 * H_C))).astype(BF16)
    w_merge = w_in[:, :, gate_hi:].astype(BF16)
    bif = jnp.pad(b_if, ((0, 0), (0, W_GATE - 2 * H_C)))[:, None, :]
    w_a = w_out_a.astype(BF16)
    w_b = w_out_b.astype(BF16)
    w_c = w_out_c.astype(BF16)
    w_ob = w_o.astype(BF16)
    w_q = w_xq.astype(BF16)
    w_kv = w_xkv.astype(BF16)
    w_xob = w_xo.astype(BF16)
    w_fg = w_ffn_in[:, :, :D_FF].astype(BF16)
    w_fu = w_ffn_in[:, :, D_FF:].astype(BF16)
    w_fo = w_ffn_out.astype(BF16)

    x = jnp.concatenate([x_prompt.reshape(n_p, d), x_sample.reshape(n_s, d)], axis=0)

    mem_k, mem_v, mem_kb, mem_vb = _memkv(mem_prompt.reshape(bp * N_MEM, d), norm_mem_g[:, None, :], w_kv)

    zeros_ca = jnp.zeros((bp, K_A - 1, D_A), F32)
    zeros_cb = jnp.zeros((bp, K_B - 1, D_B), F32)
    zeros_c = jnp.zeros((bp, H_C, DH_C, DH_C), F32)
    zeros_n = jnp.zeros((bp, 1, H_C * DH_C), F32)
    zeros_m = jnp.zeros((bp, 1, W_GATE), F32)

    n_chunks_s = n_s // CHUNK
    outs = {k: [] for k in ("pa", "pb", "pc", "pn", "pm", "sa", "sb", "sc", "sn", "sm")}
    for l in range(DEPTH):
        zm, zg, gates = _inproj(x, norm_mix_g[l][None, :], w_main[l], w_merge[l], w_gate[l])
        small = (conv_a_w[l], conv_b_w[l], conv_b_b[l][None, :], ln_b_g[l][None, :], ln_b_b[l][None, :],
                 bif[l], mlstm_norm_g[l][None, :])
        p_p, a1, b1, c1, n1, m1 = _mixer(
            zm, gates, zeros_ca, zeros_cb, zeros_c, zeros_n, zeros_m, *small,
            row_block0=0, n_groups=bp, n_chunks=tp // CHUNK, seq_rows=CHUNK, carry=True)
        n0_rows = jnp.repeat(state_mlstm_n[l].reshape(bs, H_C * DH_C), ts, axis=0)
        m0_rows = jnp.repeat(jnp.pad(state_mlstm_m[l], ((0, 0), (0, W_GATE - H_C))), ts, axis=0)
        p_s, a2, b2, c2, n2, m2 = _mixer(
            zm, gates, state_conv_a[l], state_conv_b[l], state_mlstm_c[l],
            n0_rows.reshape(n_chunks_s, CHUNK, H_C * DH_C), m0_rows.reshape(n_chunks_s, CHUNK, W_GATE), *small,
            row_block0=n_p // CHUNK, n_groups=1, n_chunks=n_chunks_s, seq_rows=ts, carry=False)
        x1, qx = _outproj(x, p_p, p_s, zg, w_a[l], w_b[l], w_c[l], w_ob[l], norm_x_g[l][None, :], w_q[l])
        c_p = _xattn_prompt(qx, mem_kb, mem_vb, l, n_groups=bp, rows_per_group=tp)
        c_s = _xattn_sample(qx, cache_mem_k[l].reshape(bs, N_MEM, hd), cache_mem_v[l].reshape(bs, N_MEM, hd),
                            row_block0=n_p // 16, n_seq=bs, seq_rows=ts)
        x = _ffn(x1, c_p, c_s, w_xob[l], norm_ffn_g[l][None, :], w_fg[l], w_fu[l], w_fo[l],
                 final_norm_g[None, :], final=(l == DEPTH - 1))
        outs["pa"].append(a1); outs["pb"].append(b1); outs["pc"].append(c1)
        outs["pn"].append(n1); outs["pm"].append(m1[:, 0, :H_C])
        outs["sa"].append(a2); outs["sb"].append(b2); outs["sc"].append(c2)
        outs["sn"].append(n2); outs["sm"].append(m2[:, 0, :H_C])

    y_prompt = x[:n_p].reshape(bp, tp, d)
    y_sample = x[n_p:].reshape(bs, ts, d)
    st = {k: jnp.stack(v) for k, v in outs.items()}
    return (y_prompt, y_sample,
            st["pa"], st["pb"], st["pc"], st["pn"], st["pm"],
            mem_k.reshape(DEPTH, bp, N_MEM, HX, DX), mem_v.reshape(DEPTH, bp, N_MEM, HX, DX),
            st["sa"], st["sb"], st["sc"], st["sn"], st["sm"])
```

```python
import functools

import jax
import jax.numpy as jnp
from jax import lax
from jax.experimental import pallas as pl
from jax.experimental.pallas import tpu as pltpu

D_MODEL = 1024
DEPTH = 4
D_A = 512
K_A = 3
D_B = 512
K_B = 31
H_C = 4
DH_C = 256
N_MEM = 256
HX = 4
DX = 256
D_FF = 2816
EPS = 1e-6

OFF_AB, OFF_AC, OFF_AX, OFF_BV, OFF_BG = 0, 512, 1024, 1536, 2048
OFF_Q, OFF_K, OFF_V, OFF_O = 2560, 3584, 4608, 5632
W_MAIN = 6656
W_MERGE = 3 * D_MODEL
W_GATE = 128
W_P = 2 * D_A + D_MODEL

CHUNK = 128
ROW_TILE = 512
VMEM_LIMIT_BYTES = 56 * 1024 * 1024

F32 = jnp.float32
BF16 = jnp.bfloat16


def _dot(a, b):
    return jnp.dot(a, b, preferred_element_type=F32)


def _dot_nt(a, b):
    return lax.dot_general(a, b, (((1,), (1,)), ((), ())), preferred_element_type=F32)


def _dot_f32(a, b):
    return jnp.dot(a, b, preferred_element_type=F32, precision=lax.Precision.HIGHEST)


def _rmsnorm(x, g):
    ms = jnp.mean(x * x, axis=-1, keepdims=True)
    return x * lax.rsqrt(ms + EPS) * g


def _resident(shape):
    nd = len(shape)
    return pl.BlockSpec(shape, lambda *_: (0,) * nd, pipeline_mode=pl.Buffered(1))


def _params(sem):
    return pltpu.CompilerParams(dimension_semantics=sem, vmem_limit_bytes=VMEM_LIMIT_BYTES)


def _inproj_kernel(x_ref, g_ref, wm_ref, wg_ref, wgate_ref, zm_ref, zg_ref, gates_ref):
    h = _rmsnorm(x_ref[...], g_ref[...]).astype(BF16)
    for j in range(W_MAIN // 512):
        sl = slice(j * 512, (j + 1) * 512)
        zm_ref[:, sl] = _dot(h, wm_ref[:, sl]).astype(BF16)
    for j in range(W_MERGE // 512):
        sl = slice(j * 512, (j + 1) * 512)
        zg_ref[:, sl] = _dot(h, wg_ref[:, sl]).astype(BF16)
    gates_ref[...] = _dot(h, wgate_ref[...])


def _inproj(x, g, wm, wg, wgate):
    n = x.shape[0]
    tm = 256
    return pl.pallas_call(
        _inproj_kernel,
        grid=(n // tm,),
        in_specs=[
            pl.BlockSpec((tm, D_MODEL), lambda i: (i, 0)),
            _resident((1, D_MODEL)),
            _resident((D_MODEL, W_MAIN)),
            _resident((D_MODEL, W_MERGE)),
            _resident((D_MODEL, W_GATE)),
        ],
        out_specs=[
            pl.BlockSpec((tm, W_MAIN), lambda i: (i, 0)),
            pl.BlockSpec((tm, W_MERGE), lambda i: (i, 0)),
            pl.BlockSpec((tm, W_GATE), lambda i: (i, 0)),
        ],
        out_shape=[
            jax.ShapeDtypeStruct((n, W_MAIN), BF16),
            jax.ShapeDtypeStruct((n, W_MERGE), BF16),
            jax.ShapeDtypeStruct((n, W_GATE), F32),
        ],
        compiler_params=_params(("parallel",)),
        name="inproj",
    )(x, g, wm, wg, wgate)


def _mixer_kernel(*refs, seq_rows, carry, has_acc):
    (zm_ref, gates_ref, ca0_ref, cb0_ref, c0_ref, n0_ref, m0_ref,
     caw_ref, cbw_ref, cbb_ref, lng_ref, lnb_ref, bif_ref, mng_ref) = refs[:14]
    refs = refs[15:] if has_acc else refs[14:]
    (p_ref, sa_ref, sb_ref, c_out_ref, n_out_ref, m_out_ref,
     xa_s, xb_s, aconv_s, bconv_s, c_s, n_s, m_s,
     num_s, qc_s, vwt_s, winter_s, rinv_s, wlast_s, wprev_s, mnew_s) = refs
    L = CHUNK
    nseq = L // seq_rows
    t = pl.program_id(1)
    j = pl.program_id(2)

    @pl.when(j == 0)
    def _pre():
        a_b = zm_ref[:, OFF_AB:OFF_AB + D_A].astype(F32)
        ca = zm_ref[:, OFF_AC:OFF_AC + D_A].astype(F32) * zm_ref[:, OFF_AX:OFF_AX + D_A].astype(F32)
        cb = zm_ref[:, OFF_BV:OFF_BV + D_B].astype(F32) * jax.nn.sigmoid(
            zm_ref[:, OFF_BG:OFF_BG + D_B].astype(F32))

        if carry:
            @pl.when(t == 0)
            def _():
                xa_s[0:8, :] = jnp.zeros((8, D_A), F32)
                xb_s[0:32, :] = jnp.zeros((32, D_B), F32)
                xa_s[6:8, :] = ca0_ref[0, 0]
                xb_s[2:32, :] = cb0_ref[0, 0]
                c_s[...] = c0_ref[0, 0]
                n_s[0:1, :] = n0_ref[0]
                m_s[0:1, :] = m0_ref[0]

            xa_s[8:8 + L, :] = ca
            xb_s[32:32 + L, :] = cb
            for cblk in range(D_A // 128):
                cs = slice(cblk * 128, (cblk + 1) * 128)
                acc = caw_ref[0:1, cs] * xa_s[6:6 + L, cs]
                for k in range(1, K_A):
                    acc = acc + caw_ref[k:k + 1, cs] * xa_s[6 + k:6 + k + L, cs]
                aconv_s[:, cs] = acc
                acc = cbw_ref[0:1, cs] * xb_s[2:2 + L, cs]
                for k in range(1, K_B):
                    acc = acc + cbw_ref[k:k + 1, cs] * xb_s[2 + k:2 + k + L, cs]
                bconv_s[:, cs] = acc
            sa_new = xa_s[6 + L:8 + L, :]
            sb_new = xb_s[2 + L:32 + L, :]
            xa_s[6:8, :] = sa_new
            xb_s[2:32, :] = sb_new
            sa_ref[0] = sa_new
            sb_ref[0] = sb_new
        else:
            xa_s[...] = jnp.zeros(xa_s.shape, F32)
            xb_s[...] = jnp.zeros(xb_s.shape, F32)
            aconv_s[...] = ca
            bconv_s[...] = cb
            for b in range(nseq):
                r0 = b * seq_rows
                xa_s[6:8, :] = ca0_ref[0, b]
                xa_s[8:8 + seq_rows, :] = aconv_s[r0:r0 + seq_rows, :]
                xb_s[2:32, :] = cb0_ref[0, b]
                xb_s[32:32 + seq_rows, :] = bconv_s[r0:r0 + seq_rows, :]
                acc = caw_ref[0:1, :] * xa_s[6:14, :]
                for k in range(1, K_A):
                    acc = acc + caw_ref[k:k + 1, :] * xa_s[6 + k:14 + k, :]
                sa_ref[b] = xa_s[6 + seq_rows:8 + seq_rows, :]
                aconv_s[r0:r0 + seq_rows, :] = acc[0:seq_rows]
                acc = cbw_ref[0:1, :] * xb_s[2:10, :]
                for k in range(1, K_B):
                    acc = acc + cbw_ref[k:k + 1, :] * xb_s[2 + k:10 + k, :]
                sb_ref[b] = xb_s[2 + seq_rows:32 + seq_rows, :]
                bconv_s[r0:r0 + seq_rows, :] = acc[0:seq_rows]

        p_ref[:, 0:D_A] = (a_b * aconv_s[...]).astype(BF16)
        bc = bconv_s[...] + cbb_ref[...]
        mu = jnp.mean(bc, axis=-1, keepdims=True)
        xc = bc - mu
        ln = xc * lax.rsqrt(jnp.mean(xc * xc, axis=-1, keepdims=True) + EPS) * lng_ref[...] + lnb_ref[...]
        p_ref[:, D_A:D_A + D_B] = (ln * jax.nn.sigmoid(ln)).astype(BF16)

        row = lax.broadcasted_iota(jnp.int32, (L, L), 0)
        col = lax.broadcasted_iota(jnp.int32, (L, L), 1)
        same = (row // seq_rows) == (col // seq_rows)
        causal = same & (col <= row)
        g = gates_ref[...] + bif_ref[...]
        logf = jnp.minimum(g, 0.0) - jnp.log1p(jnp.exp(-jnp.abs(g)))
        bt = _dot_f32(causal.astype(F32), logf)
        btl = _dot_f32(same.astype(F32), logf)
        bt_h = pltpu.roll(bt, 128 - H_C, axis=1)
        btl_h = pltpu.roll(btl, 128 - H_C, axis=1)
        g_t = g.T
        bt_t = bt.T
        m_rows = m_s[0:1, :] if carry else m0_ref[0]
        inter = bt_h + m_rows
        lane = lax.broadcasted_iota(jnp.int32, (L, W_GATE), 1)

        dlogs = []
        rmax = jnp.zeros((L, W_GATE), F32)
        for h in range(H_C):
            dlog = jnp.where(causal, bt[:, H_C + h:H_C + h + 1] - bt_t[H_C + h:H_C + h + 1, :]
                             + g_t[h:h + 1, :], -jnp.inf)
            dlogs.append(dlog)
            rmax = jnp.where(lane == h, jnp.max(dlog, axis=-1, keepdims=True), rmax)
        m_t = jnp.maximum(inter, rmax)
        w_inter = jnp.exp(inter - m_t)
        floor = jnp.exp(-m_t)
        last = (col == (row // seq_rows) * seq_rows + (seq_rows - 1)).astype(F32)
        m_new = _dot_f32(last, m_t)
        w_last = jnp.exp(btl_h - bt_h + g - m_new)
        w_prev = jnp.exp(inter - m_new)
        n_rows = n_s[0:1, :] if carry else n0_ref[0]

        den = jnp.zeros((L, W_GATE), F32)
        for h in range(H_C):
            hs = slice(h * DH_C, (h + 1) * DH_C)
            q = zm_ref[:, OFF_Q + h * DH_C:OFF_Q + (h + 1) * DH_C]
            k = zm_ref[:, OFF_K + h * DH_C:OFF_K + (h + 1) * DH_C]
            v = zm_ref[:, OFF_V + h * DH_C:OFF_V + (h + 1) * DH_C]
            s = _dot_nt(q, k) * (DH_C ** -0.5) * jnp.exp(dlogs[h] - m_t[:, h:h + 1])
            num_s[:, hs] = _dot(s.astype(BF16), v)
            qn = jnp.sum(q.astype(F32) * n_rows[:, hs], axis=-1, keepdims=True)
            den_h = jnp.sum(s, axis=-1, keepdims=True) + w_inter[:, h:h + 1] * qn
            den = jnp.where(lane == h, den_h, den)
            vw = v.astype(F32) * w_last[:, h:h + 1]
            vwt_s[h] = vw.T.astype(BF16)
        if nseq > 1:
            qc_s[...] = jnp.zeros(qc_s.shape, F32)
        winter_s[...] = w_inter
        rinv_s[...] = 1.0 / jnp.maximum(jnp.abs(den), floor)
        wlast_s[...] = w_last
        wprev_s[...] = w_prev
        mnew_s[...] = m_new

    tile = 16
    if nseq == 1:
        wprev_row = wprev_s[L - 1:L, :]
        m_out_row = mnew_s[L - 1:L, :]
    else:
        last_row = j * seq_rows + (seq_rows - 1)
        lane_l = lax.broadcasted_iota(jnp.int32, (1, L), 1)
        row_l = lax.broadcasted_iota(jnp.int32, (L, 1), 0)
        seq_lanes = (lane_l // seq_rows) == j
        seq_rows_mask = (row_l // seq_rows) == j
        wprev_row = wprev_s[pl.ds(last_row, 1), :]
        m_out_row = mnew_s[pl.ds(last_row, 1), :]
    for h in range(H_C):
        hs = slice(h * DH_C, (h + 1) * DH_C)
        c_old = c_s[h] if carry else c0_ref[0, 0, h]
        c_bf = c_old.astype(BF16)
        if nseq == 1:
            qc_s[:, hs] = _dot_nt(zm_ref[:, OFF_Q + h * DH_C:OFF_Q + (h + 1) * DH_C], c_bf)
        else:
            r0 = pl.multiple_of((j * seq_rows // tile) * tile, tile)
            q16 = zm_ref[pl.ds(r0, tile), OFF_Q + h * DH_C:OFF_Q + (h + 1) * DH_C]
            r = _dot_nt(q16, c_bf)
            rid = lax.broadcasted_iota(jnp.int32, (tile, 1), 0) + r0
            qc_s[pl.ds(r0, tile), hs] = jnp.where((rid // seq_rows) == j, r, qc_s[pl.ds(r0, tile), hs])
        k = zm_ref[:, OFF_K + h * DH_C:OFF_K + (h + 1) * DH_C]
        if nseq == 1:
            vwt = vwt_s[h]
            wl = wlast_s[:, h:h + 1]
        else:
            vwt = jnp.where(seq_lanes, vwt_s[h], jnp.zeros((DH_C, L), BF16))
            wl = jnp.where(seq_rows_mask, wlast_s[:, h:h + 1], 0.0)
        kv = _dot(vwt, k)
        w_prev = wprev_row[:, h:h + 1]
        c_new = w_prev * c_old + kv * (DH_C ** -0.5)
        ksum = jnp.sum(k.astype(F32) * wl, axis=0, keepdims=True)
        if carry:
            n_old = n_s[0:1, hs]
        else:
            n_old = n0_ref[0, pl.ds(j * seq_rows, 1), hs]
        n_new = w_prev * n_old + ksum * (DH_C ** -0.5)
        if carry:
            c_s[h] = c_new
            n_s[0:1, hs] = n_new
        c_out_ref[0, 0, h] = c_new
        n_out_ref[0, h:h + 1, :] = n_new
    if carry:
        m_s[0:1, :] = m_out_row
    m_out_ref[0] = m_out_row

    @pl.when(j == nseq - 1)
    def _post():
        for h in range(H_C):
            hs = slice(h * DH_C, (h + 1) * DH_C)
            hh = (num_s[:, hs] + qc_s[:, hs] * winter_s[:, h:h + 1]) * rinv_s[:, h:h + 1]
            hn = hh * lax.rsqrt(jnp.mean(hh * hh, axis=-1, keepdims=True) + EPS) * mng_ref[:, hs]
            o = zm_ref[:, OFF_O + h * DH_C:OFF_O + (h + 1) * DH_C].astype(F32)
            p_ref[:, 2 * D_A + h * DH_C:2 * D_A + (h + 1) * DH_C] = (jax.nn.sigmoid(o) * hn).astype(BF16)


def _mixer(zm, gates, ca0, cb0, c0, n0, m0, caw, cbw, cbb, lng, lnb, bif, mng, c_acc, *,
           row_block0, n_groups, n_chunks, seq_rows, carry, layer_in, layer_out):
    L = CHUNK
    nseq = L // seq_rows
    n_state = n_groups if carry else n_chunks * nseq
    rows = n_groups * n_chunks * L

    def tok(b, t, j):
        return (row_block0 + b * n_chunks + t, 0)

    def out_tok(b, t, j):
        return (b * n_chunks + t, 0)

    if carry:
        def seq_of(b, t, j):
            return b

        def conv_of(b, t, j):
            return b
        conv_blk = 1
        rows_blk = 1
    else:
        def seq_of(b, t, j):
            return t * nseq + j

        def conv_of(b, t, j):
            return t
        conv_blk = nseq
        rows_blk = L

    has_acc = c_acc is not None
    kern = functools.partial(_mixer_kernel, seq_rows=seq_rows, carry=carry, has_acc=has_acc)
    in_specs = [
        pl.BlockSpec((L, W_MAIN), tok),
        pl.BlockSpec((L, W_GATE), tok),
        pl.BlockSpec((1, conv_blk, K_A - 1, D_A), lambda *g: (layer_in, conv_of(*g), 0, 0)),
        pl.BlockSpec((1, conv_blk, K_B - 1, D_B), lambda *g: (layer_in, conv_of(*g), 0, 0)),
        pl.BlockSpec((1, 1, H_C, DH_C, DH_C), lambda *g: (layer_in, seq_of(*g), 0, 0, 0)),
        pl.BlockSpec((1, rows_blk, D_MODEL), lambda *g: (conv_of(*g), 0, 0)),
        pl.BlockSpec((1, rows_blk, W_GATE), lambda *g: (conv_of(*g), 0, 0)),
        _resident((K_A, D_A)),
        _resident((K_B, D_B)),
        _resident((1, D_B)),
        _resident((1, D_B)),
        _resident((1, D_B)),
        _resident((1, W_GATE)),
        _resident((1, D_MODEL)),
    ]
    args = [zm, gates, ca0, cb0, c0, n0, m0, caw, cbw, cbb, lng, lnb, bif, mng]
    aliases = {}
    if has_acc:
        in_specs.append(pl.BlockSpec(memory_space=pl.ANY))
        args.append(c_acc)
        aliases = {len(args) - 1: 3}
    return pl.pallas_call(
        kern,
        grid=(n_groups, n_chunks, nseq),
        in_specs=in_specs,
        out_specs=[
            pl.BlockSpec((L, W_P), out_tok),
            pl.BlockSpec((conv_blk, K_A - 1, D_A), lambda *g: (conv_of(*g), 0, 0)),
            pl.BlockSpec((conv_blk, K_B - 1, D_B), lambda *g: (conv_of(*g), 0, 0)),
            pl.BlockSpec((1, 1, H_C, DH_C, DH_C), lambda *g: (layer_out, seq_of(*g), 0, 0, 0)),
            pl.BlockSpec((1, H_C, DH_C), lambda *g: (seq_of(*g), 0, 0)),
            pl.BlockSpec((1, 1, W_GATE), lambda *g: (seq_of(*g), 0, 0)),
        ],
        out_shape=[
            jax.ShapeDtypeStruct((rows, W_P), BF16),
            jax.ShapeDtypeStruct((n_state, K_A - 1, D_A), F32),
            jax.ShapeDtypeStruct((n_state, K_B - 1, D_B), F32),
            jax.ShapeDtypeStruct((DEPTH, n_state, H_C, DH_C, DH_C), F32),
            jax.ShapeDtypeStruct((n_state, H_C, DH_C), F32),
            jax.ShapeDtypeStruct((n_state, 1, W_GATE), F32),
        ],
        input_output_aliases=aliases,
        scratch_shapes=[
            pltpu.VMEM((8 + L, D_A) if carry else (16, D_A), F32),
            pltpu.VMEM((32 + L, D_B) if carry else (40, D_B), F32),
            pltpu.VMEM((L, D_A), F32),
            pltpu.VMEM((L, D_B), F32),
            pltpu.VMEM((H_C, DH_C, DH_C), F32),
            pltpu.VMEM((8, D_MODEL), F32),
            pltpu.VMEM((8, W_GATE), F32),
            pltpu.VMEM((L, D_MODEL), F32),
            pltpu.VMEM((L, D_MODEL), F32),
            pltpu.VMEM((H_C, DH_C, L), BF16),
            pltpu.VMEM((L, W_GATE), F32),
            pltpu.VMEM((L, W_GATE), F32),
            pltpu.VMEM((L, W_GATE), F32),
            pltpu.VMEM((L, W_GATE), F32),
            pltpu.VMEM((L, W_GATE), F32),
        ],
        compiler_params=_params(("arbitrary", "arbitrary", "arbitrary")),
        name="mixer_prompt" if carry else "mixer_sample",
    )(*args)


def _outproj_kernel(x_ref, pp_ref, ps_ref, zg_ref, wa_ref, wb_ref, wc_ref, wo_ref, gx_ref, wq_ref,
                    x1_ref, q_ref, *, n_prompt_tiles):
    i = pl.program_id(0)
    p = jnp.where(i >= n_prompt_tiles, ps_ref[...], pp_ref[...])
    y_a = _dot(p[:, 0:D_A], wa_ref[...])
    y_b = _dot(p[:, D_A:D_A + D_B], wb_ref[...])
    y_c = _dot(p[:, D_A + D_B:], wc_ref[...])
    u = (jax.nn.sigmoid(zg_ref[:, 0:D_MODEL].astype(F32)) * y_a
         + jax.nn.sigmoid(zg_ref[:, D_MODEL:2 * D_MODEL].astype(F32)) * y_b
         + jax.nn.sigmoid(zg_ref[:, 2 * D_MODEL:].astype(F32)) * y_c)
    x1 = x_ref[...] + _dot(u.astype(BF16), wo_ref[...])
    x1_ref[...] = x1
    q_ref[...] = _dot(_rmsnorm(x1, gx_ref[...]).astype(BF16), wq_ref[...]).astype(BF16)


def _outproj(x, p_p, p_s, zg, wa, wb, wc, wo, gx, wq):
    n = x.shape[0]
    tm = ROW_TILE
    npt = p_p.shape[0] // tm
    return pl.pallas_call(
        functools.partial(_outproj_kernel, n_prompt_tiles=npt),
        grid=(n // tm,),
        in_specs=[
            pl.BlockSpec((tm, D_MODEL), lambda i: (i, 0)),
            pl.BlockSpec((tm, W_P), lambda i: (jnp.minimum(i, npt - 1), 0)),
            pl.BlockSpec((tm, W_P), lambda i: (0, 0)),
            pl.BlockSpec((tm, W_MERGE), lambda i: (i, 0)),
            _resident((D_A, D_MODEL)),
            _resident((D_B, D_MODEL)),
            _resident((D_MODEL, D_MODEL)),
            _resident((D_MODEL, D_MODEL)),
            _resident((1, D_MODEL)),
            _resident((D_MODEL, D_MODEL)),
        ],
        out_specs=[
            pl.BlockSpec((tm, D_MODEL), lambda i: (i, 0)),
            pl.BlockSpec((tm, D_MODEL), lambda i: (i, 0)),
        ],
        out_shape=[
            jax.ShapeDtypeStruct((n, D_MODEL), F32),
            jax.ShapeDtypeStruct((n, D_MODEL), BF16),
        ],
        compiler_params=_params(("parallel",)),
        name="outproj",
    )(x, p_p, p_s, zg, wa, wb, wc, wo, gx, wq)


def _memkv_kernel(mem_ref, g_ref, w_ref, k_ref, v_ref, kb_ref, vb_ref):
    h = _rmsnorm(mem_ref[...], g_ref[0]).astype(BF16)
    hd = HX * DX
    k = _dot(h, w_ref[0, :, 0:hd])
    v = _dot(h, w_ref[0, :, hd:])
    for bb in range(k_ref.shape[1]):
        for hh in range(HX):
            k_ref[0, bb, :, hh, :] = k[bb * N_MEM:(bb + 1) * N_MEM, hh * DX:(hh + 1) * DX]
            v_ref[0, bb, :, hh, :] = v[bb * N_MEM:(bb + 1) * N_MEM, hh * DX:(hh + 1) * DX]
    kb_ref[0] = k.astype(BF16)
    vb_ref[0] = v.astype(BF16)


def _memkv(mem, g, w):
    n = mem.shape[0]
    tm = ROW_TILE
    hd = HX * DX
    per = tm // N_MEM
    o_spec = pl.BlockSpec((1, tm, hd), lambda l, i: (l, i, 0))
    o5_spec = pl.BlockSpec((1, per, N_MEM, HX, DX), lambda l, i: (l, i, 0, 0, 0))
    return pl.pallas_call(
        _memkv_kernel,
        grid=(DEPTH, n // tm),
        in_specs=[
            pl.BlockSpec((tm, D_MODEL), lambda l, i: (i, 0)),
            pl.BlockSpec((1, 1, D_MODEL), lambda l, i: (l, 0, 0)),
            pl.BlockSpec((1, D_MODEL, 2 * hd), lambda l, i: (l, 0, 0)),
        ],
        out_specs=[o5_spec, o5_spec, o_spec, o_spec],
        out_shape=[
            jax.ShapeDtypeStruct((DEPTH, n // N_MEM, N_MEM, HX, DX), F32),
            jax.ShapeDtypeStruct((DEPTH, n // N_MEM, N_MEM, HX, DX), F32),
            jax.ShapeDtypeStruct((DEPTH, n, hd), BF16),
            jax.ShapeDtypeStruct((DEPTH, n, hd), BF16),
        ],
        compiler_params=_params(("arbitrary", "arbitrary")),
        name="memkv",
    )(mem, g, w)


def _attend(q, k, v):
    s = _dot_nt(q, k) * (DX ** -0.5)
    e = jnp.exp(s - jnp.max(s, axis=-1, keepdims=True))
    return _dot(e.astype(BF16), v) * (1.0 / jnp.sum(e, axis=-1, keepdims=True))


def _xattn_prompt_kernel(q_ref, k_ref, v_ref, o_ref):
    for h in range(HX):
        hs = slice(h * DX, (h + 1) * DX)
        o_ref[:, hs] = _attend(q_ref[:, hs], k_ref[0, :, hs], v_ref[0, :, hs]).astype(BF16)


def _xattn_prompt(q, kb, vb, layer, *, n_groups, rows_per_group):
    tq = ROW_TILE
    nt = rows_per_group // tq
    hd = HX * DX
    kv_spec = pl.BlockSpec((1, N_MEM, hd), lambda b, t: (layer, b, 0))
    return pl.pallas_call(
        _xattn_prompt_kernel,
        grid=(n_groups, nt),
        in_specs=[pl.BlockSpec((tq, hd), lambda b, t: (b * nt + t, 0)), kv_spec, kv_spec],
        out_specs=pl.BlockSpec((tq, hd), lambda b, t: (b * nt + t, 0)),
        out_shape=jax.ShapeDtypeStruct((n_groups * rows_per_group, hd), BF16),
        compiler_params=_params(("parallel", "parallel")),
        name="xattn_prompt",
    )(q, kb, vb)


def _xattn_sample_kernel(q_ref, k_ref, v_ref, o_ref, *, seq_rows):
    rows = q_ref.shape[0]
    rid = lax.broadcasted_iota(jnp.int32, (rows, 1), 0)
    for h in range(HX):
        hs = slice(h * DX, (h + 1) * DX)
        q = q_ref[:, hs]
        acc = jnp.zeros((rows, DX), F32)
        for e in range(rows // seq_rows):
            o = _attend(q, k_ref[0, e, :, h, :].astype(BF16), v_ref[0, e, :, h, :].astype(BF16))
            acc = jnp.where((rid // seq_rows) == e, o, acc)
        o_ref[:, hs] = acc.astype(BF16)


def _xattn_sample(q, k, v, layer, *, row_block0, n_seq, seq_rows):
    rows = 16
    per = rows // seq_rows
    hd = HX * DX
    kv_spec = pl.BlockSpec((1, per, N_MEM, HX, DX), lambda i: (layer, i, 0, 0, 0))
    return pl.pallas_call(
        functools.partial(_xattn_sample_kernel, seq_rows=seq_rows),
        grid=(n_seq // per,),
        in_specs=[pl.BlockSpec((rows, hd), lambda i: (row_block0 + i, 0)), kv_spec, kv_spec],
        out_specs=pl.BlockSpec((rows, hd), lambda i: (i, 0)),
        out_shape=jax.ShapeDtypeStruct((n_seq * seq_rows, hd), BF16),
        compiler_params=_params(("parallel",)),
        name="xattn_sample",
    )(q, k, v)


FF_BLOCK = 256


def _ffn_kernel(x_ref, cp_ref, cs_ref, wxo_ref, gf_ref, wg_ref, wu_ref, wout_ref, gfin_ref,
                o_ref, act_s, *, n_prompt_tiles, final):
    i = pl.program_id(0)
    ctx = jnp.where(i >= n_prompt_tiles, cs_ref[...], cp_ref[...])
    x2 = x_ref[...] + _dot(ctx, wxo_ref[...])
    h = _rmsnorm(x2, gf_ref[...]).astype(BF16)
    for jb in range(D_FF // FF_BLOCK):
        sl = slice(jb * FF_BLOCK, (jb + 1) * FF_BLOCK)
        gate = _dot(h, wg_ref[:, sl])
        up = _dot(h, wu_ref[:, sl])
        act_s[:, sl] = (gate * jax.nn.sigmoid(gate) * up).astype(BF16)
    x3 = x2 + _dot(act_s[...], wout_ref[...])
    if final:
        x3 = _rmsnorm(x3, gfin_ref[...])
    o_ref[...] = x3


def _ffn(x, c_p, c_s, wxo, gf, wg, wu, wout, gfin, *, final):
    n = x.shape[0]
    tm = ROW_TILE
    npt = c_p.shape[0] // tm
    return pl.pallas_call(
        functools.partial(_ffn_kernel, n_prompt_tiles=npt, final=final),
        grid=(n // tm,),
        in_specs=[
            pl.BlockSpec((tm, D_MODEL), lambda i: (i, 0)),
            pl.BlockSpec((tm, D_MODEL), lambda i: (jnp.minimum(i, npt - 1), 0)),
            pl.BlockSpec((tm, D_MODEL), lambda i: (0, 0)),
            _resident((D_MODEL, D_MODEL)),
            _resident((1, D_MODEL)),
            _resident((D_MODEL, D_FF)),
            _resident((D_MODEL, D_FF)),
            _resident((D_FF, D_MODEL)),
            _resident((1, D_MODEL)),
        ],
        out_specs=pl.BlockSpec((tm, D_MODEL), lambda i: (i, 0)),
        out_shape=jax.ShapeDtypeStruct((n, D_MODEL), F32),
        scratch_shapes=[pltpu.VMEM((tm, D_FF), BF16)],
        compiler_params=_params(("parallel",)),
        name="ffn",
    )(x, c_p, c_s, wxo, gf, wg, wu, wout, gfin)


def kernel(x_prompt, x_sample, state_conv_a, state_conv_b, state_mlstm_c, state_mlstm_n, state_mlstm_m,
           cache_mem_k, cache_mem_v, mem_prompt, norm_mix_g, w_in, b_if, conv_a_w, w_out_a, conv_b_w,
           conv_b_b, ln_b_g, ln_b_b, w_out_b, mlstm_norm_g, w_out_c, w_o, norm_x_g, norm_mem_g, w_xq,
           w_xkv, w_xo, norm_ffn_g, w_ffn_in, w_ffn_out, final_norm_g):
    bp, tp, d = x_prompt.shape
    bs, ts, _ = x_sample.shape
    n_p = bp * tp
    n_s = bs * ts
    hd = HX * DX

    gate_lo = W_MAIN
    gate_hi = W_MAIN + 2 * H_C
    w_main = w_in[:, :, :gate_lo].astype(BF16)
    w_gate = jnp.pad(w_in[:, :, gate_lo:gate_hi], ((0, 0), (0, 0), (0, W_GATE - 2 * H_C))).astype(BF16)
    w_merge = w_in[:, :, gate_hi:].astype(BF16)
    bif = jnp.pad(b_if, ((0, 0), (0, W_GATE - 2 * H_C)))[:, None, :]
    w_a = w_out_a.astype(BF16)
    w_b = w_out_b.astype(BF16)
    w_c = w_out_c.astype(BF16)
    w_ob = w_o.astype(BF16)
    w_q = w_xq.astype(BF16)
    w_kv = w_xkv.astype(BF16)
    w_xob = w_xo.astype(BF16)
    w_fg = w_ffn_in[:, :, :D_FF].astype(BF16)
    w_fu = w_ffn_in[:, :, D_FF:].astype(BF16)
    w_fo = w_ffn_out.astype(BF16)

    x = jnp.concatenate([x_prompt.reshape(n_p, d), x_sample.reshape(n_s, d)], axis=0)

    mem_k, mem_v, mem_kb, mem_vb = _memkv(mem_prompt.reshape(bp * N_MEM, d), norm_mem_g[:, None, :], w_kv)

    zeros_ca = jnp.zeros((1, bp, K_A - 1, D_A), F32)
    zeros_cb = jnp.zeros((1, bp, K_B - 1, D_B), F32)
    zeros_c = jnp.zeros((1, bp, H_C, DH_C, DH_C), F32)
    zeros_n = jnp.zeros((bp, 1, H_C * DH_C), F32)
    zeros_m = jnp.zeros((bp, 1, W_GATE), F32)

    n_chunks_s = n_s // CHUNK
    outs = {k: [] for k in ("pa", "pb", "pn", "pm", "sa", "sb", "sn", "sm")}
    c1 = c2 = None
    for l in range(DEPTH):
        zm, zg, gates = _inproj(x, norm_mix_g[l][None, :], w_main[l], w_merge[l], w_gate[l])
        small = (conv_a_w[l], conv_b_w[l], conv_b_b[l][None, :], ln_b_g[l][None, :], ln_b_b[l][None, :],
                 bif[l], mlstm_norm_g[l][None, :])
        p_p, a1, b1, c1, n1, m1 = _mixer(
            zm, gates, zeros_ca, zeros_cb, zeros_c, zeros_n, zeros_m, *small, c1,
            row_block0=0, n_groups=bp, n_chunks=tp // CHUNK, seq_rows=CHUNK, carry=True,
            layer_in=0, layer_out=l)
        n0_rows = jnp.repeat(state_mlstm_n[l].reshape(bs, H_C * DH_C), ts, axis=0)
        m0_rows = jnp.repeat(jnp.pad(state_mlstm_m[l], ((0, 0), (0, W_GATE - H_C))), ts, axis=0)
        p_s, a2, b2, c2, n2, m2 = _mixer(
            zm, gates, state_conv_a, state_conv_b, state_mlstm_c,
            n0_rows.reshape(n_chunks_s, CHUNK, H_C * DH_C), m0_rows.reshape(n_chunks_s, CHUNK, W_GATE), *small, c2,
            row_block0=n_p // CHUNK, n_groups=1, n_chunks=n_chunks_s, seq_rows=ts, carry=False,
            layer_in=l, layer_out=l)
        x1, qx = _outproj(x, p_p, p_s, zg, w_a[l], w_b[l], w_c[l], w_ob[l], norm_x_g[l][None, :], w_q[l])
        c_p = _xattn_prompt(qx, mem_kb, mem_vb, l, n_groups=bp, rows_per_group=tp)
        c_s = _xattn_sample(qx, cache_mem_k, cache_mem_v, l, row_block0=n_p // 16, n_seq=bs, seq_rows=ts)
        x = _ffn(x1, c_p, c_s, w_xob[l], norm_ffn_g[l][None, :], w_fg[l], w_fu[l], w_fo[l],
                 final_norm_g[None, :], final=(l == DEPTH - 1))
        outs["pa"].append(a1); outs["pb"].append(b1)
        outs["pn"].append(n1); outs["pm"].append(m1[:, 0, :H_C])
        outs["sa"].append(a2); outs["sb"].append(b2)
        outs["sn"].append(n2); outs["sm"].append(m2[:, 0, :H_C])

    y_prompt = x[:n_p].reshape(bp, tp, d)
    y_sample = x[n_p:].reshape(bs, ts, d)
    st = {k: jnp.stack(v) for k, v in outs.items()}
    return (y_prompt, y_sample,
            st["pa"], st["pb"], c1, st["pn"], st["pm"],
            mem_k, mem_v,
            st["sa"], st["sb"], c2, st["sn"], st["sm"])
```

```python
import functools

import jax
import jax.numpy as jnp
from jax import lax
from jax.experimental import pallas as pl
from jax.experimental.pallas import tpu as pltpu

D_MODEL = 1024
DEPTH = 4
D_A = 512
K_A = 3
D_B = 512
K_B = 31
H_C = 4
DH_C = 256
N_MEM = 256
HX = 4
DX = 256
D_FF = 2816
EPS = 1e-6

OFF_AB, OFF_AC, OFF_AX, OFF_BV, OFF_BG = 0, 512, 1024, 1536, 2048
OFF_Q, OFF_K, OFF_V, OFF_O = 2560, 3584, 4608, 5632
W_MAIN = 6656
W_MERGE = 3 * D_MODEL
W_GATE = 128
W_P = 2 * D_A + D_MODEL

CHUNK = 128
ROW_TILE = 512
VMEM_LIMIT_BYTES = 56 * 1024 * 1024

F32 = jnp.float32
BF16 = jnp.bfloat16


def _dot(a, b):
    return jnp.dot(a, b, preferred_element_type=F32)


def _dot_nt(a, b):
    return lax.dot_general(a, b, (((1,), (1,)), ((), ())), preferred_element_type=F32)


def _dot_f32(a, b):
    return jnp.dot(a, b, preferred_element_type=F32, precision=lax.Precision.HIGHEST)


def _rmsnorm(x, g):
    ms = jnp.mean(x * x, axis=-1, keepdims=True)
    return x * lax.rsqrt(ms + EPS) * g


def _resident(shape):
    nd = len(shape)
    return pl.BlockSpec(shape, lambda *_: (0,) * nd, pipeline_mode=pl.Buffered(1))


def _params(sem):
    return pltpu.CompilerParams(dimension_semantics=sem, vmem_limit_bytes=VMEM_LIMIT_BYTES)


def _inproj_kernel(x_ref, g_ref, wm_ref, wg_ref, wgate_ref, zm_ref, zg_ref, gates_ref):
    h = _rmsnorm(x_ref[...], g_ref[...]).astype(BF16)
    for j in range(W_MAIN // 512):
        sl = slice(j * 512, (j + 1) * 512)
        zm_ref[:, sl] = _dot(h, wm_ref[:, sl]).astype(BF16)
    for j in range(W_MERGE // 512):
        sl = slice(j * 512, (j + 1) * 512)
        zg_ref[:, sl] = _dot(h, wg_ref[:, sl]).astype(BF16)
    gates_ref[...] = _dot(h, wgate_ref[...])


def _inproj(x, g, wm, wg, wgate):
    n = x.shape[0]
    tm = ROW_TILE
    return pl.pallas_call(
        _inproj_kernel,
        grid=(n // tm,),
        in_specs=[
            pl.BlockSpec((tm, D_MODEL), lambda i: (i, 0)),
            _resident((1, D_MODEL)),
            _resident((D_MODEL, W_MAIN)),
            _resident((D_MODEL, W_MERGE)),
            _resident((D_MODEL, W_GATE)),
        ],
        out_specs=[
            pl.BlockSpec((tm, W_MAIN), lambda i: (i, 0)),
            pl.BlockSpec((tm, W_MERGE), lambda i: (i, 0)),
            pl.BlockSpec((tm, W_GATE), lambda i: (i, 0)),
        ],
        out_shape=[
            jax.ShapeDtypeStruct((n, W_MAIN), BF16),
            jax.ShapeDtypeStruct((n, W_MERGE), BF16),
            jax.ShapeDtypeStruct((n, W_GATE), F32),
        ],
        compiler_params=_params(("parallel",)),
        name="inproj",
    )(x, g, wm, wg, wgate)


def _mixer_kernel(*refs, seq_rows, carry, has_acc):
    (zm_ref, gates_ref, ca0_ref, cb0_ref, c0_ref, n0_ref, m0_ref,
     caw_ref, cbw_ref, cbb_ref, lng_ref, lnb_ref, bif_ref, mng_ref) = refs[:14]
    refs = refs[15:] if has_acc else refs[14:]
    (p_ref, sa_ref, sb_ref, c_out_ref, n_out_ref, m_out_ref,
     xa_s, xb_s, xbr_s, aconv_s, bconv_s, c_s, n_s, m_s,
     num_s, qc_s, vwt_s, winter_s, rinv_s, wlast_s, wprev_s, mnew_s) = refs
    L = CHUNK
    nseq = L // seq_rows
    t = pl.program_id(1)
    j = pl.program_id(2)

    @pl.when(j == 0)
    def _pre():
        a_b = zm_ref[:, OFF_AB:OFF_AB + D_A].astype(F32)
        ca = zm_ref[:, OFF_AC:OFF_AC + D_A].astype(F32) * zm_ref[:, OFF_AX:OFF_AX + D_A].astype(F32)
        cb = zm_ref[:, OFF_BV:OFF_BV + D_B].astype(F32) * jax.nn.sigmoid(
            zm_ref[:, OFF_BG:OFF_BG + D_B].astype(F32))

        if carry:
            @pl.when(t == 0)
            def _():
                xa_s[0:8, :] = jnp.zeros((8, D_A), F32)
                xb_s[0:32, :] = jnp.zeros((32, D_B), F32)
                xa_s[6:8, :] = ca0_ref[0, 0]
                xb_s[2:32, :] = cb0_ref[0, 0]
                c_s[...] = c0_ref[0, 0]
                n_s[0:1, :] = n0_ref[0]
                m_s[0:1, :] = m0_ref[0]

            xa_s[8:8 + L, :] = ca
            xb_s[32:32 + L, :] = cb
            for r in range(1, 8):
                xbr_s[r] = xb_s[r:r + 24 + L, :]
            for cblk in range(D_A // 128):
                cs = slice(cblk * 128, (cblk + 1) * 128)
                acc = caw_ref[0:1, cs] * xa_s[6:6 + L, cs]
                for k in range(1, K_A):
                    acc = acc + caw_ref[k:k + 1, cs] * xa_s[6 + k:6 + k + L, cs]
                aconv_s[:, cs] = acc
                acc = None
                for k in range(K_B):
                    a8, r = (2 + k) // 8 * 8, (2 + k) % 8
                    src = xb_s[a8:a8 + L, cs] if r == 0 else xbr_s[r, a8:a8 + L, cs]
                    term = cbw_ref[k:k + 1, cs] * src
                    acc = term if acc is None else acc + term
                bconv_s[:, cs] = acc
            sa_new = xa_s[6 + L:8 + L, :]
            sb_new = xb_s[2 + L:32 + L, :]
            xa_s[6:8, :] = sa_new
            xb_s[2:32, :] = sb_new
            sa_ref[0] = sa_new
            sb_ref[0] = sb_new
        else:
            xa_s[...] = jnp.zeros(xa_s.shape, F32)
            xb_s[...] = jnp.zeros(xb_s.shape, F32)
            aconv_s[...] = ca
            bconv_s[...] = cb
            for b in range(nseq):
                r0 = b * seq_rows
                xa_s[6:8, :] = ca0_ref[0, b]
                xa_s[8:8 + seq_rows, :] = aconv_s[r0:r0 + seq_rows, :]
                xb_s[2:32, :] = cb0_ref[0, b]
                xb_s[32:32 + seq_rows, :] = bconv_s[r0:r0 + seq_rows, :]
                acc = caw_ref[0:1, :] * xa_s[6:14, :]
                for k in range(1, K_A):
                    acc = acc + caw_ref[k:k + 1, :] * xa_s[6 + k:14 + k, :]
                sa_ref[b] = xa_s[6 + seq_rows:8 + seq_rows, :]
                aconv_s[r0:r0 + seq_rows, :] = acc[0:seq_rows]
                acc = cbw_ref[0:1, :] * xb_s[2:10, :]
                for k in range(1, K_B):
                    acc = acc + cbw_ref[k:k + 1, :] * xb_s[2 + k:10 + k, :]
                sb_ref[b] = xb_s[2 + seq_rows:32 + seq_rows, :]
                bconv_s[r0:r0 + seq_rows, :] = acc[0:seq_rows]

        p_ref[:, 0:D_A] = (a_b * aconv_s[...]).astype(BF16)
        bc = bconv_s[...] + cbb_ref[...]
        mu = jnp.mean(bc, axis=-1, keepdims=True)
        xc = bc - mu
        ln = xc * lax.rsqrt(jnp.mean(xc * xc, axis=-1, keepdims=True) + EPS) * lng_ref[...] + lnb_ref[...]
        p_ref[:, D_A:D_A + D_B] = (ln * jax.nn.sigmoid(ln)).astype(BF16)

        row = lax.broadcasted_iota(jnp.int32, (L, L), 0)
        col = lax.broadcasted_iota(jnp.int32, (L, L), 1)
        same = (row // seq_rows) == (col // seq_rows)
        causal = same & (col <= row)
        g = gates_ref[...] + bif_ref[...]
        logf = jnp.minimum(g, 0.0) - jnp.log1p(jnp.exp(-jnp.abs(g)))
        bt = _dot_f32(causal.astype(F32), logf)
        btl = _dot_f32(same.astype(F32), logf)
        bt_h = pltpu.roll(bt, 128 - H_C, axis=1)
        btl_h = pltpu.roll(btl, 128 - H_C, axis=1)
        g_t = g.T
        bt_t = bt.T
        m_rows = m_s[0:1, :] if carry else m0_ref[0]
        inter = bt_h + m_rows
        lane = lax.broadcasted_iota(jnp.int32, (L, W_GATE), 1)

        dlogs = []
        rmax = jnp.zeros((L, W_GATE), F32)
        for h in range(H_C):
            dlog = jnp.where(causal, bt[:, H_C + h:H_C + h + 1] - bt_t[H_C + h:H_C + h + 1, :]
                             + g_t[h:h + 1, :], -jnp.inf)
            dlogs.append(dlog)
            rmax = jnp.where(lane == h, jnp.max(dlog, axis=-1, keepdims=True), rmax)
        m_t = jnp.maximum(inter, rmax)
        w_inter = jnp.exp(inter - m_t)
        floor = jnp.exp(-m_t)
        last = (col == (row // seq_rows) * seq_rows + (seq_rows - 1)).astype(F32)
        m_new = _dot_f32(last, m_t)
        w_last = jnp.exp(btl_h - bt_h + g - m_new)
        w_prev = jnp.exp(inter - m_new)
        n_rows = n_s[0:1, :] if carry else n0_ref[0]

        den = jnp.zeros((L, W_GATE), F32)
        for h in range(H_C):
            hs = slice(h * DH_C, (h + 1) * DH_C)
            q = zm_ref[:, OFF_Q + h * DH_C:OFF_Q + (h + 1) * DH_C]
            k = zm_ref[:, OFF_K + h * DH_C:OFF_K + (h + 1) * DH_C]
            v = zm_ref[:, OFF_V + h * DH_C:OFF_V + (h + 1) * DH_C]
            s = _dot_nt(q, k) * (DH_C ** -0.5) * jnp.exp(dlogs[h] - m_t[:, h:h + 1])
            num_s[:, hs] = _dot(s.astype(BF16), v)
            qn = jnp.sum(q.astype(F32) * n_rows[:, hs], axis=-1, keepdims=True)
            den_h = jnp.sum(s, axis=-1, keepdims=True) + w_inter[:, h:h + 1] * qn
            den = jnp.where(lane == h, den_h, den)
            vw = v.astype(F32) * w_last[:, h:h + 1]
            vwt_s[h] = vw.T.astype(BF16)
        if nseq > 1:
            qc_s[...] = jnp.zeros(qc_s.shape, F32)
        winter_s[...] = w_inter
        rinv_s[...] = 1.0 / jnp.maximum(jnp.abs(den), floor)
        wlast_s[...] = w_last
        wprev_s[...] = w_prev
        mnew_s[...] = m_new

    spp = c0_ref.shape[1]
    tile = spp * seq_rows if nseq > 1 else L
    for u in range(spp):
        if nseq == 1:
            wprev_row = wprev_s[L - 1:L, :]
            m_out_row = mnew_s[L - 1:L, :]
        else:
            sq = j * spp + u
            last_row = sq * seq_rows + (seq_rows - 1)
            lane_l = lax.broadcasted_iota(jnp.int32, (1, L), 1)
            row_l = lax.broadcasted_iota(jnp.int32, (L, 1), 0)
            seq_lanes = (lane_l // seq_rows) == sq
            seq_rows_mask = (row_l // seq_rows) == sq
            wprev_row = wprev_s[pl.ds(last_row, 1), :]
            m_out_row = mnew_s[pl.ds(last_row, 1), :]
        for h in range(H_C):
            hs = slice(h * DH_C, (h + 1) * DH_C)
            c_old = c_s[h] if carry else c0_ref[0, u, h]
            c_bf = c_old.astype(BF16)
            if nseq == 1:
                qc_s[:, hs] = _dot_nt(zm_ref[:, OFF_Q + h * DH_C:OFF_Q + (h + 1) * DH_C], c_bf)
            else:
                r0 = pl.multiple_of(j * tile, tile)
                q16 = zm_ref[pl.ds(r0, tile), OFF_Q + h * DH_C:OFF_Q + (h + 1) * DH_C]
                r = _dot_nt(q16, c_bf)
                rid = lax.broadcasted_iota(jnp.int32, (tile, 1), 0)
                qc_s[pl.ds(r0, tile), hs] = jnp.where((rid // seq_rows) == u, r, qc_s[pl.ds(r0, tile), hs])
            k = zm_ref[:, OFF_K + h * DH_C:OFF_K + (h + 1) * DH_C]
            if nseq == 1:
                vwt = vwt_s[h]
                wl = wlast_s[:, h:h + 1]
            else:
                vwt = jnp.where(seq_lanes, vwt_s[h], jnp.zeros((DH_C, L), BF16))
                wl = jnp.where(seq_rows_mask, wlast_s[:, h:h + 1], 0.0)
            kv = _dot(vwt, k)
            w_prev = wprev_row[:, h:h + 1]
            c_new = w_prev * c_old + kv * (DH_C ** -0.5)
            ksum = jnp.sum(k.astype(F32) * wl, axis=0, keepdims=True)
            if carry:
                n_old = n_s[0:1, hs]
            else:
                n_old = n0_ref[0, pl.ds(sq * seq_rows, 1), hs]
            n_new = w_prev * n_old + ksum * (DH_C ** -0.5)
            if carry:
                c_s[h] = c_new
                n_s[0:1, hs] = n_new
            c_out_ref[0, u, h] = c_new
            n_out_ref[u, h:h + 1, :] = n_new
        if carry:
            m_s[0:1, :] = m_out_row
        m_out_ref[u] = m_out_row

    @pl.when(j == nseq // spp - 1)
    def _post():
        for h in range(H_C):
            hs = slice(h * DH_C, (h + 1) * DH_C)
            hh = (num_s[:, hs] + qc_s[:, hs] * winter_s[:, h:h + 1]) * rinv_s[:, h:h + 1]
            hn = hh * lax.rsqrt(jnp.mean(hh * hh, axis=-1, keepdims=True) + EPS) * mng_ref[:, hs]
            o = zm_ref[:, OFF_O + h * DH_C:OFF_O + (h + 1) * DH_C].astype(F32)
            p_ref[:, 2 * D_A + h * DH_C:2 * D_A + (h + 1) * DH_C] = (jax.nn.sigmoid(o) * hn).astype(BF16)


def _mixer(zm, gates, ca0, cb0, c0, n0, m0, caw, cbw, cbb, lng, lnb, bif, mng, c_acc, *,
           row_block0, n_groups, n_chunks, seq_rows, carry, layer_in, layer_out):
    L = CHUNK
    nseq = L // seq_rows
    n_state = n_groups if carry else n_chunks * nseq
    rows = n_groups * n_chunks * L

    def tok(b, t, j):
        return (row_block0 + b * n_chunks + t, 0)

    def out_tok(b, t, j):
        return (b * n_chunks + t, 0)

    if carry:
        def seq_of(b, t, j):
            return b

        def conv_of(b, t, j):
            return b
        conv_blk = 1
        rows_blk = 1
        spp = 1
    else:
        spp = 16 // seq_rows

        def seq_of(b, t, j):
            return t * (nseq // spp) + j

        def conv_of(b, t, j):
            return t
        conv_blk = nseq
        rows_blk = L

    has_acc = c_acc is not None
    kern = functools.partial(_mixer_kernel, seq_rows=seq_rows, carry=carry, has_acc=has_acc)
    in_specs = [
        pl.BlockSpec((L, W_MAIN), tok),
        pl.BlockSpec((L, W_GATE), tok),
        pl.BlockSpec((1, conv_blk, K_A - 1, D_A), lambda *g: (layer_in, conv_of(*g), 0, 0)),
        pl.BlockSpec((1, conv_blk, K_B - 1, D_B), lambda *g: (layer_in, conv_of(*g), 0, 0)),
        pl.BlockSpec((1, spp, H_C, DH_C, DH_C), lambda *g: (layer_in, seq_of(*g), 0, 0, 0)),
        pl.BlockSpec((1, rows_blk, D_MODEL), lambda *g: (conv_of(*g), 0, 0)),
        pl.BlockSpec((1, rows_blk, W_GATE), lambda *g: (conv_of(*g), 0, 0)),
        _resident((K_A, D_A)),
        _resident((K_B, D_B)),
        _resident((1, D_B)),
        _resident((1, D_B)),
        _resident((1, D_B)),
        _resident((1, W_GATE)),
        _resident((1, D_MODEL)),
    ]
    args = [zm, gates, ca0, cb0, c0, n0, m0, caw, cbw, cbb, lng, lnb, bif, mng]
    aliases = {}
    if has_acc:
        in_specs.append(pl.BlockSpec(memory_space=pl.ANY))
        args.append(c_acc)
        aliases = {len(args) - 1: 3}
    return pl.pallas_call(
        kern,
        grid=(n_groups, n_chunks, nseq // spp),
        in_specs=in_specs,
        out_specs=[
            pl.BlockSpec((L, W_P), out_tok),
            pl.BlockSpec((conv_blk, K_A - 1, D_A), lambda *g: (conv_of(*g), 0, 0)),
            pl.BlockSpec((conv_blk, K_B - 1, D_B), lambda *g: (conv_of(*g), 0, 0)),
            pl.BlockSpec((1, spp, H_C, DH_C, DH_C), lambda *g: (layer_out, seq_of(*g), 0, 0, 0)),
            pl.BlockSpec((spp, H_C, DH_C), lambda *g: (seq_of(*g), 0, 0)),
            pl.BlockSpec((spp, 1, W_GATE), lambda *g: (seq_of(*g), 0, 0)),
        ],
        out_shape=[
            jax.ShapeDtypeStruct((rows, W_P), BF16),
            jax.ShapeDtypeStruct((n_state, K_A - 1, D_A), F32),
            jax.ShapeDtypeStruct((n_state, K_B - 1, D_B), F32),
            jax.ShapeDtypeStruct((DEPTH, n_state, H_C, DH_C, DH_C), F32),
            jax.ShapeDtypeStruct((n_state, H_C, DH_C), F32),
            jax.ShapeDtypeStruct((n_state, 1, W_GATE), F32),
        ],
        input_output_aliases=aliases,
        scratch_shapes=[
            pltpu.VMEM((8 + L, D_A) if carry else (16, D_A), F32),
            pltpu.VMEM((32 + L, D_B) if carry else (40, D_B), F32),
            pltpu.VMEM((8, 24 + L, D_B) if carry else (8, 8, 128), F32),
            pltpu.VMEM((L, D_A), F32),
            pltpu.VMEM((L, D_B), F32),
            pltpu.VMEM((H_C, DH_C, DH_C), F32),
            pltpu.VMEM((8, D_MODEL), F32),
            pltpu.VMEM((8, W_GATE), F32),
            pltpu.VMEM((L, D_MODEL), F32),
            pltpu.VMEM((L, D_MODEL), F32),
            pltpu.VMEM((H_C, DH_C, L), BF16),
            pltpu.VMEM((L, W_GATE), F32),
            pltpu.VMEM((L, W_GATE), F32),
            pltpu.VMEM((L, W_GATE), F32),
            pltpu.VMEM((L, W_GATE), F32),
            pltpu.VMEM((L, W_GATE), F32),
        ],
        compiler_params=_params(("arbitrary", "arbitrary", "arbitrary")),
        name="mixer_prompt" if carry else "mixer_sample",
    )(*args)


def _outproj_kernel(x_ref, pp_ref, ps_ref, zg_ref, wa_ref, wb_ref, wc_ref, wo_ref, gx_ref, wq_ref,
                    x1_ref, q_ref, *, n_prompt_tiles):
    i = pl.program_id(0)
    p = jnp.where(i >= n_prompt_tiles, ps_ref[...], pp_ref[...])
    y_a = _dot(p[:, 0:D_A], wa_ref[...])
    y_b = _dot(p[:, D_A:D_A + D_B], wb_ref[...])
    y_c = _dot(p[:, D_A + D_B:], wc_ref[...])
    u = (jax.nn.sigmoid(zg_ref[:, 0:D_MODEL].astype(F32)) * y_a
         + jax.nn.sigmoid(zg_ref[:, D_MODEL:2 * D_MODEL].astype(F32)) * y_b
         + jax.nn.sigmoid(zg_ref[:, 2 * D_MODEL:].astype(F32)) * y_c)
    x1 = x_ref[...] + _dot(u.astype(BF16), wo_ref[...])
    x1_ref[...] = x1
    q_ref[...] = _dot(_rmsnorm(x1, gx_ref[...]).astype(BF16), wq_ref[...]).astype(BF16)


def _outproj(x, p_p, p_s, zg, wa, wb, wc, wo, gx, wq):
    n = x.shape[0]
    tm = ROW_TILE
    npt = p_p.shape[0] // tm
    return pl.pallas_call(
        functools.partial(_outproj_kernel, n_prompt_tiles=npt),
        grid=(n // tm,),
        in_specs=[
            pl.BlockSpec((tm, D_MODEL), lambda i: (i, 0)),
            pl.BlockSpec((tm, W_P), lambda i: (jnp.minimum(i, npt - 1), 0)),
            pl.BlockSpec((tm, W_P), lambda i: (0, 0)),
            pl.BlockSpec((tm, W_MERGE), lambda i: (i, 0)),
            _resident((D_A, D_MODEL)),
            _resident((D_B, D_MODEL)),
            _resident((D_MODEL, D_MODEL)),
            _resident((D_MODEL, D_MODEL)),
            _resident((1, D_MODEL)),
            _resident((D_MODEL, D_MODEL)),
        ],
        out_specs=[
            pl.BlockSpec((tm, D_MODEL), lambda i: (i, 0)),
            pl.BlockSpec((tm, D_MODEL), lambda i: (i, 0)),
        ],
        out_shape=[
            jax.ShapeDtypeStruct((n, D_MODEL), F32),
            jax.ShapeDtypeStruct((n, D_MODEL), BF16),
        ],
        compiler_params=_params(("parallel",)),
        name="outproj",
    )(x, p_p, p_s, zg, wa, wb, wc, wo, gx, wq)


def _memkv_kernel(mem_ref, g_ref, w_ref, k_ref, v_ref, kb_ref, vb_ref):
    h = _rmsnorm(mem_ref[...], g_ref[0]).astype(BF16)
    hd = HX * DX
    k = _dot(h, w_ref[0, :, 0:hd])
    v = _dot(h, w_ref[0, :, hd:])
    for bb in range(k_ref.shape[1]):
        for hh in range(HX):
            k_ref[0, bb, :, hh, :] = k[bb * N_MEM:(bb + 1) * N_MEM, hh * DX:(hh + 1) * DX]
            v_ref[0, bb, :, hh, :] = v[bb * N_MEM:(bb + 1) * N_MEM, hh * DX:(hh + 1) * DX]
    kb_ref[0] = k.astype(BF16)
    vb_ref[0] = v.astype(BF16)


def _memkv(mem, g, w):
    n = mem.shape[0]
    tm = ROW_TILE
    hd = HX * DX
    per = tm // N_MEM
    o_spec = pl.BlockSpec((1, tm, hd), lambda l, i: (l, i, 0))
    o5_spec = pl.BlockSpec((1, per, N_MEM, HX, DX), lambda l, i: (l, i, 0, 0, 0))
    return pl.pallas_call(
        _memkv_kernel,
        grid=(DEPTH, n // tm),
        in_specs=[
            pl.BlockSpec((tm, D_MODEL), lambda l, i: (i, 0)),
            pl.BlockSpec((1, 1, D_MODEL), lambda l, i: (l, 0, 0)),
            pl.BlockSpec((1, D_MODEL, 2 * hd), lambda l, i: (l, 0, 0)),
        ],
        out_specs=[o5_spec, o5_spec, o_spec, o_spec],
        out_shape=[
            jax.ShapeDtypeStruct((DEPTH, n // N_MEM, N_MEM, HX, DX), F32),
            jax.ShapeDtypeStruct((DEPTH, n // N_MEM, N_MEM, HX, DX), F32),
            jax.ShapeDtypeStruct((DEPTH, n, hd), BF16),
            jax.ShapeDtypeStruct((DEPTH, n, hd), BF16),
        ],
        compiler_params=_params(("arbitrary", "arbitrary")),
        name="memkv",
    )(mem, g, w)


def _attend(q, k, v):
    s = _dot_nt(q, k) * (DX ** -0.5)
    e = jnp.exp(s - jnp.max(s, axis=-1, keepdims=True))
    return _dot(e.astype(BF16), v) * (1.0 / jnp.sum(e, axis=-1, keepdims=True))


def _xattn_prompt_kernel(q_ref, k_ref, v_ref, o_ref):
    for h in range(HX):
        hs = slice(h * DX, (h + 1) * DX)
        o_ref[:, hs] = _attend(q_ref[:, hs], k_ref[0, :, hs], v_ref[0, :, hs]).astype(BF16)


def _xattn_prompt(q, kb, vb, layer, *, n_groups, rows_per_group):
    tq = ROW_TILE
    nt = rows_per_group // tq
    hd = HX * DX
    kv_spec = pl.BlockSpec((1, N_MEM, hd), lambda b, t: (layer, b, 0))
    return pl.pallas_call(
        _xattn_prompt_kernel,
        grid=(n_groups, nt),
        in_specs=[pl.BlockSpec((tq, hd), lambda b, t: (b * nt + t, 0)), kv_spec, kv_spec],
        out_specs=pl.BlockSpec((tq, hd), lambda b, t: (b * nt + t, 0)),
        out_shape=jax.ShapeDtypeStruct((n_groups * rows_per_group, hd), BF16),
        compiler_params=_params(("parallel", "parallel")),
        name="xattn_prompt",
    )(q, kb, vb)


def _xattn_sample_kernel(q_ref, k_ref, v_ref, o_ref, *, seq_rows):
    rows = q_ref.shape[0]
    qs = jnp.concatenate([q_ref[:, h * DX:(h + 1) * DX] for h in range(HX)], axis=0)
    rid = lax.broadcasted_iota(jnp.int32, (HX * rows, 1), 0)
    cid = lax.broadcasted_iota(jnp.int32, (HX * rows, N_MEM * HX), 1)
    own_head = (cid % HX) == (rid // rows)
    acc = jnp.zeros((HX * rows, DX), F32)
    for e in range(rows // seq_rows):
        k2 = k_ref[0, e].reshape(N_MEM * HX, DX).astype(BF16)
        v2 = v_ref[0, e].reshape(N_MEM * HX, DX).astype(BF16)
        s = jnp.where(own_head, _dot_nt(qs, k2) * (DX ** -0.5), -jnp.inf)
        p = jnp.exp(s - jnp.max(s, axis=-1, keepdims=True))
        o = _dot(p.astype(BF16), v2) * (1.0 / jnp.sum(p, axis=-1, keepdims=True))
        acc = jnp.where(((rid % rows) // seq_rows) == e, o, acc)
    for h in range(HX):
        o_ref[:, h * DX:(h + 1) * DX] = acc[h * rows:(h + 1) * rows].astype(BF16)


XS_ROWS = 16


def _xattn_sample(q, k, v, layer, *, row_block0, n_seq, seq_rows):
    rows = XS_ROWS
    per = rows // seq_rows
    hd = HX * DX
    kv_spec = pl.BlockSpec((1, per, N_MEM, HX, DX), lambda i: (layer, i, 0, 0, 0))
    return pl.pallas_call(
        functools.partial(_xattn_sample_kernel, seq_rows=seq_rows),
        grid=(n_seq // per,),
        in_specs=[pl.BlockSpec((rows, hd), lambda i: (row_block0 + i, 0)), kv_spec, kv_spec],
        out_specs=pl.BlockSpec((rows, hd), lambda i: (i, 0)),
        out_shape=jax.ShapeDtypeStruct((n_seq * seq_rows, hd), BF16),
        compiler_params=_params(("parallel",)),
        name="xattn_sample",
    )(q, k, v)


FF_BLOCK = 256


def _ffn_kernel(x_ref, cp_ref, cs_ref, wxo_ref, gf_ref, wg_ref, wu_ref, wout_ref, gfin_ref,
                *rest, n_prompt_tiles, final):
    act_s = rest[-1]
    i = pl.program_id(0)
    ctx = jnp.where(i >= n_prompt_tiles, cs_ref[...], cp_ref[...])
    x2 = x_ref[...] + _dot(ctx, wxo_ref[...])
    h = _rmsnorm(x2, gf_ref[...]).astype(BF16)
    for jb in range(D_FF // FF_BLOCK):
        sl = slice(jb * FF_BLOCK, (jb + 1) * FF_BLOCK)
        gate = _dot(h, wg_ref[:, sl])
        up = _dot(h, wu_ref[:, sl])
        act_s[:, sl] = (gate * jax.nn.sigmoid(gate) * up).astype(BF16)
    x3 = x2 + _dot(act_s[...], wout_ref[...])
    if final:
        y = _rmsnorm(x3, gfin_ref[...])
        yp_ref, ys_ref = rest[0], rest[1]

        @pl.when(i < n_prompt_tiles)
        def _():
            yp_ref[...] = y

        @pl.when(i >= n_prompt_tiles)
        def _():
            ys_ref[...] = y
    else:
        rest[0][...] = x3


def _ffn(x, c_p, c_s, wxo, gf, wg, wu, wout, gfin, *, final):
    n = x.shape[0]
    tm = ROW_TILE
    npt = c_p.shape[0] // tm
    if final:
        assert n - npt * tm == tm
        out_specs = [pl.BlockSpec((tm, D_MODEL), lambda i: (jnp.minimum(i, npt - 1), 0)),
                     pl.BlockSpec((tm, D_MODEL), lambda i: (0, 0))]
        out_shape = [jax.ShapeDtypeStruct((npt * tm, D_MODEL), F32),
                     jax.ShapeDtypeStruct((tm, D_MODEL), F32)]
        sem = ("arbitrary",)
    else:
        out_specs = pl.BlockSpec((tm, D_MODEL), lambda i: (i, 0))
        out_shape = jax.ShapeDtypeStruct((n, D_MODEL), F32)
        sem = ("parallel",)
    return pl.pallas_call(
        functools.partial(_ffn_kernel, n_prompt_tiles=npt, final=final),
        grid=(n // tm,),
        in_specs=[
            pl.BlockSpec((tm, D_MODEL), lambda i: (i, 0)),
            pl.BlockSpec((tm, D_MODEL), lambda i: (jnp.minimum(i, npt - 1), 0)),
            pl.BlockSpec((tm, D_MODEL), lambda i: (0, 0)),
            _resident((D_MODEL, D_MODEL)),
            _resident((1, D_MODEL)),
            _resident((D_MODEL, D_FF)),
            _resident((D_MODEL, D_FF)),
            _resident((D_FF, D_MODEL)),
            _resident((1, D_MODEL)),
        ],
        out_specs=out_specs,
        out_shape=out_shape,
        scratch_shapes=[pltpu.VMEM((tm, D_FF), BF16)],
        compiler_params=_params(sem),
        name="ffn",
    )(x, c_p, c_s, wxo, gf, wg, wu, wout, gfin)


def kernel(x_prompt, x_sample, state_conv_a, state_conv_b, state_mlstm_c, state_mlstm_n, state_mlstm_m,
           cache_mem_k, cache_mem_v, mem_prompt, norm_mix_g, w_in, b_if, conv_a_w, w_out_a, conv_b_w,
           conv_b_b, ln_b_g, ln_b_b, w_out_b, mlstm_norm_g, w_out_c, w_o, norm_x_g, norm_mem_g, w_xq,
           w_xkv, w_xo, norm_ffn_g, w_ffn_in, w_ffn_out, final_norm_g):
    bp, tp, d = x_prompt.shape
    bs, ts, _ = x_sample.shape
    n_p = bp * tp
    n_s = bs * ts
    hd = HX * DX

    gate_lo = W_MAIN
    gate_hi = W_MAIN + 2 * H_C
    w_main = w_in[:, :, :gate_lo].astype(BF16)
    w_gate = jnp.pad(w_in[:, :, gate_lo:gate_hi], ((0, 0), (0, 0), (0, W_GATE - 2 * H_C))).astype(BF16)
    w_merge = w_in[:, :, gate_hi:].astype(BF16)
    bif = jnp.pad(b_if, ((0, 0), (0, W_GATE - 2 * H_C)))[:, None, :]
    w_a = w_out_a.astype(BF16)
    w_b = w_out_b.astype(BF16)
    w_c = w_out_c.astype(BF16)
    w_ob = w_o.astype(BF16)
    w_q = w_xq.astype(BF16)
    w_kv = w_xkv.astype(BF16)
    w_xob = w_xo.astype(BF16)
    w_fg = w_ffn_in[:, :, :D_FF].astype(BF16)
    w_fu = w_ffn_in[:, :, D_FF:].astype(BF16)
    w_fo = w_ffn_out.astype(BF16)

    x = jnp.concatenate([x_prompt.reshape(n_p, d), x_sample.reshape(n_s, d)], axis=0)

    mem_k, mem_v, mem_kb, mem_vb = _memkv(mem_prompt.reshape(bp * N_MEM, d), norm_mem_g[:, None, :], w_kv)

    zeros_ca = jnp.zeros((1, bp, K_A - 1, D_A), F32)
    zeros_cb = jnp.zeros((1, bp, K_B - 1, D_B), F32)
    zeros_c = jnp.zeros((1, bp, H_C, DH_C, DH_C), F32)
    zeros_n = jnp.zeros((bp, 1, H_C * DH_C), F32)
    zeros_m = jnp.zeros((bp, 1, W_GATE), F32)

    n_chunks_s = n_s // CHUNK
    outs = {k: [] for k in ("pa", "pb", "pn", "pm", "sa", "sb", "sn", "sm")}
    c1 = c2 = None
    for l in range(DEPTH):
        zm, zg, gates = _inproj(x, norm_mix_g[l][None, :], w_main[l], w_merge[l], w_gate[l])
        small = (conv_a_w[l], conv_b_w[l], conv_b_b[l][None, :], ln_b_g[l][None, :], ln_b_b[l][None, :],
                 bif[l], mlstm_norm_g[l][None, :])
        p_p, a1, b1, c1, n1, m1 = _mixer(
            zm, gates, zeros_ca, zeros_cb, zeros_c, zeros_n, zeros_m, *small, c1,
            row_block0=0, n_groups=bp, n_chunks=tp // CHUNK, seq_rows=CHUNK, carry=True,
            layer_in=0, layer_out=l)
        n0_rows = jnp.repeat(state_mlstm_n[l].reshape(bs, H_C * DH_C), ts, axis=0)
        m0_rows = jnp.repeat(jnp.pad(state_mlstm_m[l], ((0, 0), (0, W_GATE - H_C))), ts, axis=0)
        p_s, a2, b2, c2, n2, m2 = _mixer(
            zm, gates, state_conv_a, state_conv_b, state_mlstm_c,
            n0_rows.reshape(n_chunks_s, CHUNK, H_C * DH_C), m0_rows.reshape(n_chunks_s, CHUNK, W_GATE), *small, c2,
            row_block0=n_p // CHUNK, n_groups=1, n_chunks=n_chunks_s, seq_rows=ts, carry=False,
            layer_in=l, layer_out=l)
        x1, qx = _outproj(x, p_p, p_s, zg, w_a[l], w_b[l], w_c[l], w_ob[l], norm_x_g[l][None, :], w_q[l])
        c_p = _xattn_prompt(qx, mem_kb, mem_vb, l, n_groups=bp, rows_per_group=tp)
        c_s = _xattn_sample(qx, cache_mem_k, cache_mem_v, l, row_block0=n_p // XS_ROWS, n_seq=bs, seq_rows=ts)
        x = _ffn(x1, c_p, c_s, w_xob[l], norm_ffn_g[l][None, :], w_fg[l], w_fu[l], w_fo[l],
                 final_norm_g[None, :], final=(l == DEPTH - 1))
        outs["pa"].append(a1); outs["pb"].append(b1)
        outs["pn"].append(n1); outs["pm"].append(m1[:, 0, :H_C])
        outs["sa"].append(a2); outs["sb"].append(b2)
        outs["sn"].append(n2); outs["sm"].append(m2[:, 0, :H_C])

    y_prompt = x[0].reshape(bp, tp, d)
    y_sample = x[1].reshape(bs, ts, d)
    st = {k: jnp.stack(v) for k, v in outs.items()}
    return (y_prompt, y_sample,
            st["pa"], st["pb"], c1, st["pn"], st["pm"],
            mem_k, mem_v,
            st["sa"], st["sb"], c2, st["sn"], st["sm"])
```

```python
import functools

import jax
import jax.numpy as jnp
from jax import lax
from jax.experimental import pallas as pl
from jax.experimental.pallas import tpu as pltpu

D_MODEL = 1024
DEPTH = 4
D_A = 512
K_A = 3
D_B = 512
K_B = 31
H_C = 4
DH_C = 256
N_MEM = 256
HX = 4
DX = 256
D_FF = 2816
EPS = 1e-6

OFF_AB, OFF_AC, OFF_AX, OFF_BV, OFF_BG = 0, 512, 1024, 1536, 2048
OFF_Q, OFF_K, OFF_V, OFF_O = 2560, 3584, 4608, 5632
W_MAIN = 6656
W_MERGE = 3 * D_MODEL
W_GATE = 128
W_P = 2 * D_A + D_MODEL

CHUNK = 128
ROW_TILE = 512
VMEM_LIMIT_BYTES = 56 * 1024 * 1024

F32 = jnp.float32
BF16 = jnp.bfloat16


def _dot(a, b):
    return jnp.dot(a, b, preferred_element_type=F32)


def _dot_nt(a, b):
    return lax.dot_general(a, b, (((1,), (1,)), ((), ())), preferred_element_type=F32)


def _dot_tn(a, b):
    return lax.dot_general(a, b, (((0,), (0,)), ((), ())), preferred_element_type=F32)


def _dot_f32(a, b):
    return jnp.dot(a, b, preferred_element_type=F32, precision=lax.Precision.HIGHEST)


def _rmsnorm(x, g):
    ms = jnp.mean(x * x, axis=-1, keepdims=True)
    return x * lax.rsqrt(ms + EPS) * g


def _resident(shape, layer=None, col=0):
    nd = len(shape)
    if layer is None:
        return pl.BlockSpec(shape, lambda *_: (0,) * nd, pipeline_mode=pl.Buffered(1))
    return pl.BlockSpec((None,) + shape, lambda *_: (layer,) + (0,) * (nd - 1) + (col,),
                        pipeline_mode=pl.Buffered(1))


def _params(sem):
    return pltpu.CompilerParams(dimension_semantics=sem, vmem_limit_bytes=VMEM_LIMIT_BYTES)


def _inproj_kernel(x_ref, g_ref, wm_ref, wg_ref, wgate_ref, zm_ref, zg_ref, gates_ref):
    h = _rmsnorm(x_ref[...], g_ref[...]).astype(BF16)
    for j in range(W_MAIN // 512):
        sl = slice(j * 512, (j + 1) * 512)
        zm_ref[:, sl] = _dot(h, wm_ref[:, sl]).astype(BF16)
    for j in range(W_MERGE // 512):
        sl = slice(j * 512, (j + 1) * 512)
        zg_ref[:, sl] = _dot(h, wg_ref[:, sl]).astype(BF16)
    gates_ref[...] = _dot(h, wgate_ref[...])


def _inproj(x, g, wm, wg, wgate, layer):
    n = x.shape[0]
    tm = ROW_TILE
    return pl.pallas_call(
        _inproj_kernel,
        grid=(n // tm,),
        in_specs=[
            pl.BlockSpec((tm, D_MODEL), lambda i: (i, 0)),
            _resident((1, D_MODEL), layer),
            _resident((D_MODEL, W_MAIN), layer),
            _resident((D_MODEL, W_MERGE), layer),
            _resident((D_MODEL, W_GATE), layer),
        ],
        out_specs=[
            pl.BlockSpec((tm, W_MAIN), lambda i: (i, 0)),
            pl.BlockSpec((tm, W_MERGE), lambda i: (i, 0)),
            pl.BlockSpec((tm, W_GATE), lambda i: (i, 0)),
        ],
        out_shape=[
            jax.ShapeDtypeStruct((n, W_MAIN), BF16),
            jax.ShapeDtypeStruct((n, W_MERGE), BF16),
            jax.ShapeDtypeStruct((n, W_GATE), F32),
        ],
        compiler_params=_params(("parallel",)),
        name="inproj",
    )(x, g, wm, wg, wgate)


def _mixer_kernel(*refs, seq_rows, carry, has_acc):
    (zm_ref, gates_ref, ca0_ref, cb0_ref, c0_ref, n0_ref, m0_ref,
     caw_ref, cbw_ref, cbb_ref, lng_ref, lnb_ref, bif_ref, mng_ref) = refs[:14]
    refs = refs[15:] if has_acc else refs[14:]
    (p_ref, sa_ref, sb_ref, c_out_ref, n_out_ref, m_out_ref,
     xa_s, xb_s, xbr_s, aconv_s, bconv_s, c_s, n_s, m_s,
     num_s, qc_s, vw_s, winter_s, rinv_s, wlt_s, wprev_s, mnew_s) = refs
    L = CHUNK
    nseq = L // seq_rows
    t = pl.program_id(1)
    j = pl.program_id(2)

    def _when(cond):
        return (lambda f: f()) if nseq == 1 else pl.when(cond)

    @_when(j == 0)
    def _pre():
        a_b = zm_ref[:, OFF_AB:OFF_AB + D_A].astype(F32)
        ca = zm_ref[:, OFF_AC:OFF_AC + D_A].astype(F32) * zm_ref[:, OFF_AX:OFF_AX + D_A].astype(F32)
        cb = zm_ref[:, OFF_BV:OFF_BV + D_B].astype(F32) * jax.nn.sigmoid(
            zm_ref[:, OFF_BG:OFF_BG + D_B].astype(F32))

        if carry:
            @pl.when(t == 0)
            def _():
                xa_s[0:8, :] = jnp.zeros((8, D_A), F32)
                xb_s[0:32, :] = jnp.zeros((32, D_B), F32)
                xa_s[6:8, :] = ca0_ref[0, 0]
                xb_s[2:32, :] = cb0_ref[0, 0]
                c_s[...] = c0_ref[0, 0]
                n_s[0:1, :] = n0_ref[0]
                m_s[0:1, :] = m0_ref[0]

            xa_s[8:8 + L, :] = ca
            xb_s[32:32 + L, :] = cb
            for r in range(1, 8):
                xbr_s[r] = xb_s[r:r + 24 + L, :]
            for cblk in range(D_A // 128):
                cs = slice(cblk * 128, (cblk + 1) * 128)
                acc = caw_ref[0:1, cs] * xa_s[6:6 + L, cs]
                for k in range(1, K_A):
                    acc = acc + caw_ref[k:k + 1, cs] * xa_s[6 + k:6 + k + L, cs]
                aconv_s[:, cs] = acc
                acc = None
                for k in range(K_B):
                    a8, r = (2 + k) // 8 * 8, (2 + k) % 8
                    src = xb_s[a8:a8 + L, cs] if r == 0 else xbr_s[r, a8:a8 + L, cs]
                    term = cbw_ref[k:k + 1, cs] * src
                    acc = term if acc is None else acc + term
                bconv_s[:, cs] = acc
            sa_new = xa_s[6 + L:8 + L, :]
            sb_new = xb_s[2 + L:32 + L, :]
            xa_s[6:8, :] = sa_new
            xb_s[2:32, :] = sb_new
            sa_ref[0] = sa_new
            sb_ref[0] = sb_new
        else:
            xa_s[...] = jnp.zeros(xa_s.shape, F32)
            xb_s[...] = jnp.zeros(xb_s.shape, F32)
            aconv_s[...] = ca
            bconv_s[...] = cb
            for b in range(nseq):
                r0 = b * seq_rows
                xa_s[6:8, :] = ca0_ref[0, b]
                xa_s[8:8 + seq_rows, :] = aconv_s[r0:r0 + seq_rows, :]
                xb_s[2:32, :] = cb0_ref[0, b]
                xb_s[32:32 + seq_rows, :] = bconv_s[r0:r0 + seq_rows, :]
                acc = caw_ref[0:1, :] * xa_s[6:14, :]
                for k in range(1, K_A):
                    acc = acc + caw_ref[k:k + 1, :] * xa_s[6 + k:14 + k, :]
                sa_ref[b] = xa_s[6 + seq_rows:8 + seq_rows, :]
                aconv_s[r0:r0 + seq_rows, :] = acc[0:seq_rows]
                acc = cbw_ref[0:1, :] * xb_s[2:10, :]
                for k in range(1, K_B):
                    acc = acc + cbw_ref[k:k + 1, :] * xb_s[2 + k:10 + k, :]
                sb_ref[b] = xb_s[2 + seq_rows:32 + seq_rows, :]
                bconv_s[r0:r0 + seq_rows, :] = acc[0:seq_rows]

        p_ref[:, 0:D_A] = (a_b * aconv_s[...]).astype(BF16)
        bc = bconv_s[...] + cbb_ref[...]
        mu = jnp.mean(bc, axis=-1, keepdims=True)
        xc = bc - mu
        ln = xc * lax.rsqrt(jnp.mean(xc * xc, axis=-1, keepdims=True) + EPS) * lng_ref[...] + lnb_ref[...]
        p_ref[:, D_A:D_A + D_B] = (ln * jax.nn.sigmoid(ln)).astype(BF16)

        row = lax.broadcasted_iota(jnp.int32, (L, L), 0)
        col = lax.broadcasted_iota(jnp.int32, (L, L), 1)
        same = (row // seq_rows) == (col // seq_rows)
        causal = same & (col <= row)
        g = gates_ref[...] + bif_ref[...]
        logf = jnp.minimum(g, 0.0) - jnp.log1p(jnp.exp(-jnp.abs(g)))
        bt = _dot_f32(causal.astype(F32), logf)
        bt_h = pltpu.roll(bt, 128 - H_C, axis=1)
        if nseq == 1:
            btl_h = bt_h[L - 1:L, :]
        else:
            btl_h = pltpu.roll(_dot_f32(same.astype(F32), logf), 128 - H_C, axis=1)
        g_t = g.T
        bt_t = bt.T
        m_rows = m_s[0:1, :] if carry else m0_ref[0]
        inter = bt_h + m_rows
        lane = lax.broadcasted_iota(jnp.int32, (L, W_GATE), 1)

        dlogs = []
        rmax = jnp.zeros((L, W_GATE), F32)
        for h in range(H_C):
            dlog = jnp.where(causal, bt[:, H_C + h:H_C + h + 1] - bt_t[H_C + h:H_C + h + 1, :]
                             + g_t[h:h + 1, :], -jnp.inf)
            dlogs.append(dlog)
            rmax = jnp.where(lane == h, jnp.max(dlog, axis=-1, keepdims=True), rmax)
        m_t = jnp.maximum(inter, rmax)
        w_inter = jnp.exp(inter - m_t)
        floor = jnp.exp(-m_t)
        if nseq == 1:
            m_new = m_t[L - 1:L, :]
        else:
            last = (col == (row // seq_rows) * seq_rows + (seq_rows - 1)).astype(F32)
            m_new = _dot_f32(last, m_t)
        w_last = jnp.exp(btl_h - bt_h + g - m_new)
        w_prev = jnp.exp(inter - m_new)
        n_rows = n_s[0:1, :] if carry else n0_ref[0]

        den = jnp.zeros((L, W_GATE), F32)
        for h in range(H_C):
            hs = slice(h * DH_C, (h + 1) * DH_C)
            q = zm_ref[:, OFF_Q + h * DH_C:OFF_Q + (h + 1) * DH_C]
            k = zm_ref[:, OFF_K + h * DH_C:OFF_K + (h + 1) * DH_C]
            v = zm_ref[:, OFF_V + h * DH_C:OFF_V + (h + 1) * DH_C]
            s = _dot_nt(q, k) * (DH_C ** -0.5) * jnp.exp(dlogs[h] - m_t[:, h:h + 1])
            num_s[:, hs] = _dot(s.astype(BF16), v)
            qn_all = _dot_nt(q, jnp.broadcast_to(n_rows[:, hs], (L, DH_C)).astype(BF16))
            if nseq == 1:
                qn = qn_all[:, 0:1]
            else:
                qn = jnp.sum(jnp.where(row == col, qn_all, 0.0), axis=-1, keepdims=True)
            den_h = jnp.sum(s, axis=-1, keepdims=True) + w_inter[:, h:h + 1] * qn
            den = jnp.where(lane == h, den_h, den)
            vw_s[h] = (v.astype(F32) * w_last[:, h:h + 1]).astype(BF16)
        if nseq > 1:
            qc_s[...] = jnp.zeros(qc_s.shape, F32)
        winter_s[...] = w_inter
        rinv_s[...] = 1.0 / jnp.maximum(jnp.abs(den), floor)
        wlt_s[...] = w_last.T[0:16, :]
        wprev_s[...] = jnp.broadcast_to(w_prev, (L, W_GATE))
        mnew_s[...] = jnp.broadcast_to(m_new, (L, W_GATE))

    spp = c0_ref.shape[1]
    tile = spp * seq_rows if nseq > 1 else L
    for u in range(spp):
        if nseq == 1:
            wprev_row = wprev_s[L - 1:L, :]
            m_out_row = mnew_s[L - 1:L, :]
        else:
            sq = j * spp + u
            last_row = sq * seq_rows + (seq_rows - 1)
            lane_l = lax.broadcasted_iota(jnp.int32, (1, L), 1)
            row_l = lax.broadcasted_iota(jnp.int32, (L, 1), 0)
            seq_lanes = (lane_l // seq_rows) == sq
            seq_rows_mask = (row_l // seq_rows) == sq
            wprev_row = wprev_s[pl.ds(last_row, 1), :]
            m_out_row = mnew_s[pl.ds(last_row, 1), :]
        for h in range(H_C):
            hs = slice(h * DH_C, (h + 1) * DH_C)
            c_old = c_s[h] if carry else c0_ref[0, u, h]
            c_bf = c_old.astype(BF16)
            if nseq == 1:
                qc_s[:, hs] = _dot_nt(zm_ref[:, OFF_Q + h * DH_C:OFF_Q + (h + 1) * DH_C], c_bf)
            else:
                r0 = pl.multiple_of(j * tile, tile)
                q16 = zm_ref[pl.ds(r0, tile), OFF_Q + h * DH_C:OFF_Q + (h + 1) * DH_C]
                r = _dot_nt(q16, c_bf)
                rid = lax.broadcasted_iota(jnp.int32, (tile, 1), 0)
                qc_s[pl.ds(r0, tile), hs] = jnp.where((rid // seq_rows) == u, r, qc_s[pl.ds(r0, tile), hs])
            k = zm_ref[:, OFF_K + h * DH_C:OFF_K + (h + 1) * DH_C]
            if nseq == 1:
                vw = vw_s[h]
                wl = wlt_s[...]
            else:
                vw = jnp.where(seq_rows_mask, vw_s[h], jnp.zeros((L, DH_C), BF16))
                wl = jnp.where(seq_lanes, wlt_s[...], 0.0)
            kv = _dot_tn(vw, k)
            w_prev = wprev_row[:, h:h + 1]
            c_new = w_prev * c_old + kv * (DH_C ** -0.5)
            ksum = _dot(wl.astype(BF16), k)[h:h + 1, :]
            if carry:
                n_old = n_s[0:1, hs]
            else:
                n_old = n0_ref[0, pl.ds(sq * seq_rows, 1), hs]
            n_new = w_prev * n_old + ksum * (DH_C ** -0.5)
            if carry:
                c_s[h] = c_new
                n_s[0:1, hs] = n_new
            c_out_ref[0, u, h] = c_new
            n_out_ref[u, h:h + 1, :] = n_new
        if carry:
            m_s[0:1, :] = m_out_row
        m_out_ref[u] = m_out_row

    @_when(j == nseq // spp - 1)
    def _post():
        for h in range(H_C):
            hs = slice(h * DH_C, (h + 1) * DH_C)
            hh = (num_s[:, hs] + qc_s[:, hs] * winter_s[:, h:h + 1]) * rinv_s[:, h:h + 1]
            hn = hh * lax.rsqrt(jnp.mean(hh * hh, axis=-1, keepdims=True) + EPS) * mng_ref[:, hs]
            o = zm_ref[:, OFF_O + h * DH_C:OFF_O + (h + 1) * DH_C].astype(F32)
            p_ref[:, 2 * D_A + h * DH_C:2 * D_A + (h + 1) * DH_C] = (jax.nn.sigmoid(o) * hn).astype(BF16)


def _mixer(zm, gates, ca0, cb0, c0, n0, m0, caw, cbw, cbb, lng, lnb, bif, mng, c_acc, *,
           row_block0, n_groups, n_chunks, seq_rows, carry, layer, layer_in, layer_out):
    L = CHUNK
    nseq = L // seq_rows
    n_state = n_groups if carry else n_chunks * nseq
    rows = n_groups * n_chunks * L

    def tok(b, t, j):
        return (row_block0 + b * n_chunks + t, 0)

    def out_tok(b, t, j):
        return (b * n_chunks + t, 0)

    if carry:
        def seq_of(b, t, j):
            return b

        def conv_of(b, t, j):
            return b
        conv_blk = 1
        rows_blk = 1
        spp = 1
    else:
        spp = 16 // seq_rows

        def seq_of(b, t, j):
            return t * (nseq // spp) + j

        def conv_of(b, t, j):
            return t
        conv_blk = nseq
        rows_blk = L

    has_acc = c_acc is not None
    kern = functools.partial(_mixer_kernel, seq_rows=seq_rows, carry=carry, has_acc=has_acc)
    in_specs = [
        pl.BlockSpec((L, W_MAIN), tok),
        pl.BlockSpec((L, W_GATE), tok),
        pl.BlockSpec((1, conv_blk, K_A - 1, D_A), lambda *g: (layer_in, conv_of(*g), 0, 0)),
        pl.BlockSpec((1, conv_blk, K_B - 1, D_B), lambda *g: (layer_in, conv_of(*g), 0, 0)),
        pl.BlockSpec((1, spp, H_C, DH_C, DH_C), lambda *g: (layer_in, seq_of(*g), 0, 0, 0)),
        pl.BlockSpec((None, 1, rows_blk, D_MODEL), lambda *g: (layer_in, conv_of(*g), 0, 0)),
        pl.BlockSpec((None, 1, rows_blk, W_GATE), lambda *g: (layer_in, conv_of(*g), 0, 0)),
        _resident((K_A, D_A), layer),
        _resident((K_B, D_B), layer),
        _resident((1, D_B), layer),
        _resident((1, D_B), layer),
        _resident((1, D_B), layer),
        _resident((1, W_GATE), layer),
        _resident((1, D_MODEL), layer),
    ]
    args = [zm, gates, ca0, cb0, c0, n0, m0, caw, cbw, cbb, lng, lnb, bif, mng]
    aliases = {}
    if has_acc:
        in_specs.append(pl.BlockSpec(memory_space=pl.ANY))
        args.append(c_acc)
        aliases = {len(args) - 1: 3}
    return pl.pallas_call(
        kern,
        grid=(n_groups, n_chunks, nseq // spp),
        in_specs=in_specs,
        out_specs=[
            pl.BlockSpec((L, W_P), out_tok),
            pl.BlockSpec((conv_blk, K_A - 1, D_A), lambda *g: (conv_of(*g), 0, 0)),
            pl.BlockSpec((conv_blk, K_B - 1, D_B), lambda *g: (conv_of(*g), 0, 0)),
            pl.BlockSpec((1, spp, H_C, DH_C, DH_C), lambda *g: (layer_out, seq_of(*g), 0, 0, 0)),
            pl.BlockSpec((spp, H_C, DH_C), lambda *g: (seq_of(*g), 0, 0)),
            pl.BlockSpec((spp, 1, W_GATE), lambda *g: (seq_of(*g), 0, 0)),
        ],
        out_shape=[
            jax.ShapeDtypeStruct((rows, W_P), BF16),
            jax.ShapeDtypeStruct((n_state, K_A - 1, D_A), F32),
            jax.ShapeDtypeStruct((n_state, K_B - 1, D_B), F32),
            jax.ShapeDtypeStruct((DEPTH, n_state, H_C, DH_C, DH_C), F32),
            jax.ShapeDtypeStruct((n_state, H_C, DH_C), F32),
            jax.ShapeDtypeStruct((n_state, 1, W_GATE), F32),
        ],
        input_output_aliases=aliases,
        scratch_shapes=[
            pltpu.VMEM((8 + L, D_A) if carry else (16, D_A), F32),
            pltpu.VMEM((32 + L, D_B) if carry else (40, D_B), F32),
            pltpu.VMEM((8, 24 + L, D_B) if carry else (8, 8, 128), F32),
            pltpu.VMEM((L, D_A), F32),
            pltpu.VMEM((L, D_B), F32),
            pltpu.VMEM((H_C, DH_C, DH_C), F32),
            pltpu.VMEM((8, D_MODEL), F32),
            pltpu.VMEM((8, W_GATE), F32),
            pltpu.VMEM((L, D_MODEL), F32),
            pltpu.VMEM((L, D_MODEL), F32),
            pltpu.VMEM((H_C, L, DH_C), BF16),
            pltpu.VMEM((L, W_GATE), F32),
            pltpu.VMEM((L, W_GATE), F32),
            pltpu.VMEM((16, L), F32),
            pltpu.VMEM((L, W_GATE), F32),
            pltpu.VMEM((L, W_GATE), F32),
        ],
        compiler_params=_params(("arbitrary", "arbitrary", "arbitrary")),
        name="mixer_prompt" if carry else "mixer_sample",
    )(*args)


def _outproj_kernel(x_ref, pp_ref, ps_ref, zg_ref, wa_ref, wb_ref, wc_ref, wo_ref, gx_ref, wq_ref,
                    x1_ref, q_ref, *, n_prompt_tiles):
    i = pl.program_id(0)
    p = jnp.where(i >= n_prompt_tiles, ps_ref[...], pp_ref[...])
    y_a = _dot(p[:, 0:D_A], wa_ref[...])
    y_b = _dot(p[:, D_A:D_A + D_B], wb_ref[...])
    y_c = _dot(p[:, D_A + D_B:], wc_ref[...])
    u = (jax.nn.sigmoid(zg_ref[:, 0:D_MODEL].astype(F32)) * y_a
         + jax.nn.sigmoid(zg_ref[:, D_MODEL:2 * D_MODEL].astype(F32)) * y_b
         + jax.nn.sigmoid(zg_ref[:, 2 * D_MODEL:].astype(F32)) * y_c)
    x1 = x_ref[...] + _dot(u.astype(BF16), wo_ref[...])
    x1_ref[...] = x1
    q_ref[...] = _dot(_rmsnorm(x1, gx_ref[...]).astype(BF16), wq_ref[...]).astype(BF16)


def _outproj(x, p_p, p_s, zg, wa, wb, wc, wo, gx, wq, layer):
    n = x.shape[0]
    tm = ROW_TILE
    npt = p_p.shape[0] // tm
    return pl.pallas_call(
        functools.partial(_outproj_kernel, n_prompt_tiles=npt),
        grid=(n // tm,),
        in_specs=[
            pl.BlockSpec((tm, D_MODEL), lambda i: (i, 0)),
            pl.BlockSpec((tm, W_P), lambda i: (jnp.minimum(i, npt - 1), 0)),
            pl.BlockSpec((tm, W_P), lambda i: (0, 0)),
            pl.BlockSpec((tm, W_MERGE), lambda i: (i, 0)),
            _resident((D_A, D_MODEL), layer),
            _resident((D_B, D_MODEL), layer),
            _resident((D_MODEL, D_MODEL), layer),
            _resident((D_MODEL, D_MODEL), layer),
            _resident((1, D_MODEL), layer),
            _resident((D_MODEL, D_MODEL), layer),
        ],
        out_specs=[
            pl.BlockSpec((tm, D_MODEL), lambda i: (i, 0)),
            pl.BlockSpec((tm, D_MODEL), lambda i: (i, 0)),
        ],
        out_shape=[
            jax.ShapeDtypeStruct((n, D_MODEL), F32),
            jax.ShapeDtypeStruct((n, D_MODEL), BF16),
        ],
        compiler_params=_params(("parallel",)),
        name="outproj",
    )(x, p_p, p_s, zg, wa, wb, wc, wo, gx, wq)


def _memkv_kernel(mem_ref, g_ref, w_ref, k_ref, v_ref, kb_ref, vb_ref):
    h = _rmsnorm(mem_ref[...], g_ref[0]).astype(BF16)
    hd = HX * DX
    k = _dot(h, w_ref[0, :, 0:hd])
    v = _dot(h, w_ref[0, :, hd:])
    for bb in range(k_ref.shape[1]):
        for hh in range(HX):
            k_ref[0, bb, :, hh, :] = k[bb * N_MEM:(bb + 1) * N_MEM, hh * DX:(hh + 1) * DX]
            v_ref[0, bb, :, hh, :] = v[bb * N_MEM:(bb + 1) * N_MEM, hh * DX:(hh + 1) * DX]
    kb_ref[0] = k.astype(BF16)
    vb_ref[0] = v.astype(BF16)


def _memkv(mem, g, w):
    n = mem.shape[0]
    tm = ROW_TILE
    hd = HX * DX
    per = tm // N_MEM
    o_spec = pl.BlockSpec((1, tm, hd), lambda l, i: (l, i, 0))
    o5_spec = pl.BlockSpec((1, per, N_MEM, HX, DX), lambda l, i: (l, i, 0, 0, 0))
    return pl.pallas_call(
        _memkv_kernel,
        grid=(DEPTH, n // tm),
        in_specs=[
            pl.BlockSpec((tm, D_MODEL), lambda l, i: (i, 0)),
            pl.BlockSpec((1, 1, D_MODEL), lambda l, i: (l, 0, 0)),
            pl.BlockSpec((1, D_MODEL, 2 * hd), lambda l, i: (l, 0, 0)),
        ],
        out_specs=[o5_spec, o5_spec, o_spec, o_spec],
        out_shape=[
            jax.ShapeDtypeStruct((DEPTH, n // N_MEM, N_MEM, HX, DX), F32),
            jax.ShapeDtypeStruct((DEPTH, n // N_MEM, N_MEM, HX, DX), F32),
            jax.ShapeDtypeStruct((DEPTH, n, hd), BF16),
            jax.ShapeDtypeStruct((DEPTH, n, hd), BF16),
        ],
        compiler_params=_params(("arbitrary", "arbitrary")),
        name="memkv",
    )(mem, g, w)


def _attend(q, k, v):
    s = _dot_nt(q, k) * (DX ** -0.5)
    e = jnp.exp(s - jnp.max(s, axis=-1, keepdims=True))
    return _dot(e.astype(BF16), v) * (1.0 / jnp.sum(e, axis=-1, keepdims=True))


def _xattn_prompt_kernel(q_ref, k_ref, v_ref, o_ref):
    for h in range(HX):
        hs = slice(h * DX, (h + 1) * DX)
        o_ref[:, hs] = _attend(q_ref[:, hs], k_ref[0, :, hs], v_ref[0, :, hs]).astype(BF16)


def _xattn_prompt(q, kb, vb, layer, *, n_groups, rows_per_group):
    tq = ROW_TILE
    nt = rows_per_group // tq
    hd = HX * DX
    kv_spec = pl.BlockSpec((1, N_MEM, hd), lambda b, t: (layer, b, 0))
    return pl.pallas_call(
        _xattn_prompt_kernel,
        grid=(n_groups, nt),
        in_specs=[pl.BlockSpec((tq, hd), lambda b, t: (b * nt + t, 0)), kv_spec, kv_spec],
        out_specs=pl.BlockSpec((tq, hd), lambda b, t: (b * nt + t, 0)),
        out_shape=jax.ShapeDtypeStruct((n_groups * rows_per_group, hd), BF16),
        compiler_params=_params(("parallel", "parallel")),
        name="xattn_prompt",
    )(q, kb, vb)


def _xattn_sample_kernel(q_ref, k_ref, v_ref, o_ref, *, seq_rows):
    rows = q_ref.shape[0]
    qs = jnp.concatenate([q_ref[:, h * DX:(h + 1) * DX] for h in range(HX)], axis=0)
    rid = lax.broadcasted_iota(jnp.int32, (HX * rows, 1), 0)
    cid = lax.broadcasted_iota(jnp.int32, (HX * rows, N_MEM * HX), 1)
    own_head = (cid % HX) == (rid // rows)
    acc = jnp.zeros((HX * rows, DX), F32)
    for e in range(rows // seq_rows):
        k2 = k_ref[0, e].reshape(N_MEM * HX, DX).astype(BF16)
        v2 = v_ref[0, e].reshape(N_MEM * HX, DX).astype(BF16)
        s = jnp.where(own_head, _dot_nt(qs, k2) * (DX ** -0.5), -jnp.inf)
        p = jnp.exp(s - jnp.max(s, axis=-1, keepdims=True))
        o = _dot(p.astype(BF16), v2) * (1.0 / jnp.sum(p, axis=-1, keepdims=True))
        acc = jnp.where(((rid % rows) // seq_rows) == e, o, acc)
    for h in range(HX):
        o_ref[:, h * DX:(h + 1) * DX] = acc[h * rows:(h + 1) * rows].astype(BF16)


XS_ROWS = 16


def _xattn_sample(q, k, v, layer, *, row_block0, n_seq, seq_rows):
    rows = XS_ROWS
    per = rows // seq_rows
    hd = HX * DX
    kv_spec = pl.BlockSpec((1, per, N_MEM, HX, DX), lambda i: (layer, i, 0, 0, 0))
    return pl.pallas_call(
        functools.partial(_xattn_sample_kernel, seq_rows=seq_rows),
        grid=(n_seq // per,),
        in_specs=[pl.BlockSpec((rows, hd), lambda i: (row_block0 + i, 0)), kv_spec, kv_spec],
        out_specs=pl.BlockSpec((rows, hd), lambda i: (i, 0)),
        out_shape=jax.ShapeDtypeStruct((n_seq * seq_rows, hd), BF16),
        compiler_params=_params(("parallel",)),
        name="xattn_sample",
    )(q, k, v)


FF_BLOCK = 256


def _ffn_kernel(x_ref, cp_ref, cs_ref, wxo_ref, gf_ref, wg_ref, wu_ref, wout_ref, gfin_ref,
                *rest, n_prompt_tiles, final):
    act_s = rest[-1]
    i = pl.program_id(0)
    ctx = jnp.where(i >= n_prompt_tiles, cs_ref[...], cp_ref[...])
    x2 = x_ref[...] + _dot(ctx, wxo_ref[...])
    h = _rmsnorm(x2, gf_ref[...]).astype(BF16)
    for jb in range(D_FF // FF_BLOCK):
        sl = slice(jb * FF_BLOCK, (jb + 1) * FF_BLOCK)
        gate = _dot(h, wg_ref[:, sl])
        up = _dot(h, wu_ref[:, sl])
        act_s[:, sl] = (gate * jax.nn.sigmoid(gate) * up).astype(BF16)
    x3 = x2 + _dot(act_s[...], wout_ref[...])
    if final:
        y = _rmsnorm(x3, gfin_ref[...])
        yp_ref, ys_ref = rest[0], rest[1]

        @pl.when(i < n_prompt_tiles)
        def _():
            yp_ref[...] = y

        @pl.when(i >= n_prompt_tiles)
        def _():
            ys_ref[...] = y
    else:
        rest[0][...] = x3


def _ffn(x, c_p, c_s, wxo, gf, w_in, wout, gfin, layer, *, final):
    n = x.shape[0]
    tm = ROW_TILE
    npt = c_p.shape[0] // tm
    if final:
        assert n - npt * tm == tm
        out_specs = [pl.BlockSpec((tm, D_MODEL), lambda i: (jnp.minimum(i, npt - 1), 0)),
                     pl.BlockSpec((tm, D_MODEL), lambda i: (0, 0))]
        out_shape = [jax.ShapeDtypeStruct((npt * tm, D_MODEL), F32),
                     jax.ShapeDtypeStruct((tm, D_MODEL), F32)]
        sem = ("arbitrary",)
    else:
        out_specs = pl.BlockSpec((tm, D_MODEL), lambda i: (i, 0))
        out_shape = jax.ShapeDtypeStruct((n, D_MODEL), F32)
        sem = ("parallel",)
    return pl.pallas_call(
        functools.partial(_ffn_kernel, n_prompt_tiles=npt, final=final),
        grid=(n // tm,),
        in_specs=[
            pl.BlockSpec((tm, D_MODEL), lambda i: (i, 0)),
            pl.BlockSpec((tm, D_MODEL), lambda i: (jnp.minimum(i, npt - 1), 0)),
            pl.BlockSpec((tm, D_MODEL), lambda i: (0, 0)),
            _resident((D_MODEL, D_MODEL), layer),
            _resident((1, D_MODEL), layer),
            _resident((D_MODEL, D_FF), layer, col=0),
            _resident((D_MODEL, D_FF), layer, col=1),
            _resident((D_FF, D_MODEL), layer),
            _resident((1, D_MODEL)),
        ],
        out_specs=out_specs,
        out_shape=out_shape,
        scratch_shapes=[pltpu.VMEM((tm, D_FF), BF16)],
        compiler_params=_params(sem),
        name="ffn",
    )(x, c_p, c_s, wxo, gf, w_in, w_in, wout, gfin)


def kernel(x_prompt, x_sample, state_conv_a, state_conv_b, state_mlstm_c, state_mlstm_n, state_mlstm_m,
           cache_mem_k, cache_mem_v, mem_prompt, norm_mix_g, w_in, b_if, conv_a_w, w_out_a, conv_b_w,
           conv_b_b, ln_b_g, ln_b_b, w_out_b, mlstm_norm_g, w_out_c, w_o, norm_x_g, norm_mem_g, w_xq,
           w_xkv, w_xo, norm_ffn_g, w_ffn_in, w_ffn_out, final_norm_g):
    bp, tp, d = x_prompt.shape
    bs, ts, _ = x_sample.shape
    n_p = bp * tp
    n_s = bs * ts
    hd = HX * DX

    gate_lo = W_MAIN
    gate_hi = W_MAIN + 2 * H_C
    w_inb = w_in.astype(BF16)
    w_gate = jnp.pad(w_inb[:, :, gate_lo:gate_hi], ((0, 0), (0, 0), (0, W_GATE - 2 * H_C)))
    w_merge = w_inb[:, :, gate_hi:]
    bif = jnp.pad(b_if, ((0, 0), (0, W_GATE - 2 * H_C)))[:, None, :]
    w_a = w_out_a.astype(BF16)
    w_b = w_out_b.astype(BF16)
    w_c = w_out_c.astype(BF16)
    w_ob = w_o.astype(BF16)
    w_q = w_xq.astype(BF16)
    w_kv = w_xkv.astype(BF16)
    w_xob = w_xo.astype(BF16)
    w_fi = w_ffn_in.astype(BF16)
    w_fo = w_ffn_out.astype(BF16)
    g_mix = norm_mix_g[:, None, :]
    g_x = norm_x_g[:, None, :]
    g_ffn = norm_ffn_g[:, None, :]
    small = (conv_a_w, conv_b_w, conv_b_b[:, None, :], ln_b_g[:, None, :], ln_b_b[:, None, :],
             bif, mlstm_norm_g[:, None, :])

    x = jnp.concatenate([x_prompt.reshape(n_p, d), x_sample.reshape(n_s, d)], axis=0)

    mem_k, mem_v, mem_kb, mem_vb = _memkv(mem_prompt.reshape(bp * N_MEM, d), norm_mem_g[:, None, :], w_kv)

    zeros_ca = jnp.zeros((1, bp, K_A - 1, D_A), F32)
    zeros_cb = jnp.zeros((1, bp, K_B - 1, D_B), F32)
    zeros_c = jnp.zeros((1, bp, H_C, DH_C, DH_C), F32)
    zeros_n = jnp.zeros((1, bp, 1, H_C * DH_C), F32)
    zeros_m = jnp.zeros((1, bp, 1, W_GATE), F32)

    n_chunks_s = n_s // CHUNK
    n0_rows = jnp.repeat(state_mlstm_n.reshape(DEPTH, bs, H_C * DH_C), ts, axis=1)
    n0_rows = n0_rows.reshape(DEPTH, n_chunks_s, CHUNK, H_C * DH_C)
    m0_rows = jnp.repeat(jnp.pad(state_mlstm_m, ((0, 0), (0, 0), (0, W_GATE - H_C))), ts, axis=1)
    m0_rows = m0_rows.reshape(DEPTH, n_chunks_s, CHUNK, W_GATE)

    outs = {k: [] for k in ("pa", "pb", "pn", "pm", "sa", "sb", "sn", "sm")}
    c1 = c2 = None
    for l in range(DEPTH):
        zm, zg, gates = _inproj(x, g_mix, w_inb, w_merge, w_gate, l)
        p_p, a1, b1, c1, n1, m1 = _mixer(
            zm, gates, zeros_ca, zeros_cb, zeros_c, zeros_n, zeros_m, *small, c1,
            row_block0=0, n_groups=bp, n_chunks=tp // CHUNK, seq_rows=CHUNK, carry=True,
            layer=l, layer_in=0, layer_out=l)
        p_s, a2, b2, c2, n2, m2 = _mixer(
            zm, gates, state_conv_a, state_conv_b, state_mlstm_c, n0_rows, m0_rows, *small, c2,
            row_block0=n_p // CHUNK, n_groups=1, n_chunks=n_chunks_s, seq_rows=ts, carry=False,
            layer=l, layer_in=l, layer_out=l)
        x1, qx = _outproj(x, p_p, p_s, zg, w_a, w_b, w_c, w_ob, g_x, w_q, l)
        c_p = _xattn_prompt(qx, mem_kb, mem_vb, l, n_groups=bp, rows_per_group=tp)
        c_s = _xattn_sample(qx, cache_mem_k, cache_mem_v, l, row_block0=n_p // XS_ROWS, n_seq=bs, seq_rows=ts)
        x = _ffn(x1, c_p, c_s, w_xob, g_ffn, w_fi, w_fo, final_norm_g[None, :], l, final=(l == DEPTH - 1))
        outs["pa"].append(a1); outs["pb"].append(b1); outs["pn"].append(n1); outs["pm"].append(m1)
        outs["sa"].append(a2); outs["sb"].append(b2); outs["sn"].append(n2); outs["sm"].append(m2)

    y_prompt = x[0].reshape(bp, tp, d)
    y_sample = x[1].reshape(bs, ts, d)
    st = {k: jnp.stack(v) for k, v in outs.items()}
    return (y_prompt, y_sample,
            st["pa"], st["pb"], c1, st["pn"], st["pm"][:, :, 0, :H_C],
            mem_k, mem_v,
            st["sa"], st["sb"], c2, st["sn"], st["sm"][:, :, 0, :H_C])
```

```python
import functools

import jax
import jax.numpy as jnp
from jax import lax
from jax.experimental import pallas as pl
from jax.experimental.pallas import tpu as pltpu

D_MODEL = 1024
DEPTH = 4
D_A = 512
K_A = 3
D_B = 512
K_B = 31
H_C = 4
DH_C = 256
N_MEM = 256
HX = 4
DX = 256
D_FF = 2816
EPS = 1e-6

OFF_AB, OFF_AC, OFF_AX, OFF_BV, OFF_BG = 0, 512, 1024, 1536, 2048
OFF_Q, OFF_K, OFF_V, OFF_O = 2560, 3584, 4608, 5632
W_MAIN = 6656
W_MERGE = 3 * D_MODEL
W_GATE = 128
W_P = 2 * D_A + D_MODEL

CHUNK = 128
ROW_TILE = 512
VMEM_LIMIT_BYTES = 56 * 1024 * 1024

F32 = jnp.float32
BF16 = jnp.bfloat16


def _dot(a, b):
    return jnp.dot(a, b, preferred_element_type=F32)


def _dot_nt(a, b):
    return lax.dot_general(a, b, (((1,), (1,)), ((), ())), preferred_element_type=F32)


def _dot_tn(a, b):
    return lax.dot_general(a, b, (((0,), (0,)), ((), ())), preferred_element_type=F32)


def _dot_f32(a, b):
    return jnp.dot(a, b, preferred_element_type=F32, precision=lax.Precision.HIGHEST)


def _rmsnorm(x, g):
    ms = jnp.mean(x * x, axis=-1, keepdims=True)
    return x * lax.rsqrt(ms + EPS) * g


def _resident(shape, layer=None, col=0):
    nd = len(shape)
    if layer is None:
        return pl.BlockSpec(shape, lambda *_: (0,) * nd, pipeline_mode=pl.Buffered(1))
    return pl.BlockSpec((None,) + shape, lambda *_: (layer,) + (0,) * (nd - 1) + (col,),
                        pipeline_mode=pl.Buffered(1))


def _params(sem):
    return pltpu.CompilerParams(dimension_semantics=sem, vmem_limit_bytes=VMEM_LIMIT_BYTES)


def _inproj_kernel(x_ref, g_ref, wm_ref, wg_ref, wgate_ref, zm_ref, zg_ref, gates_ref):
    h = _rmsnorm(x_ref[...], g_ref[...]).astype(BF16)
    for j in range(W_MAIN // 512):
        sl = slice(j * 512, (j + 1) * 512)
        zm_ref[:, sl] = _dot(h, wm_ref[:, sl]).astype(BF16)
    for j in range(W_MERGE // 512):
        sl = slice(j * 512, (j + 1) * 512)
        zg_ref[:, sl] = _dot(h, wg_ref[:, sl]).astype(BF16)
    gates_ref[...] = _dot(h, wgate_ref[...])


def _inproj(x, g, wm, wg, wgate, layer):
    n = x.shape[0]
    tm = ROW_TILE
    return pl.pallas_call(
        _inproj_kernel,
        grid=(n // tm,),
        in_specs=[
            pl.BlockSpec((tm, D_MODEL), lambda i: (i, 0)),
            _resident((1, D_MODEL), layer),
            _resident((D_MODEL, W_MAIN), layer),
            _resident((D_MODEL, W_MERGE), layer),
            _resident((D_MODEL, W_GATE), layer),
        ],
        out_specs=[
            pl.BlockSpec((tm, W_MAIN), lambda i: (i, 0)),
            pl.BlockSpec((tm, W_MERGE), lambda i: (i, 0)),
            pl.BlockSpec((tm, W_GATE), lambda i: (i, 0)),
        ],
        out_shape=[
            jax.ShapeDtypeStruct((n, W_MAIN), BF16),
            jax.ShapeDtypeStruct((n, W_MERGE), BF16),
            jax.ShapeDtypeStruct((n, W_GATE), F32),
        ],
        compiler_params=_params(("parallel",)),
        name="inproj",
    )(x, g, wm, wg, wgate)


def _prompt_chunk(t, zm_ref, gates_ref, ca0_ref, cb0_ref, c0_ref, n0_ref, m0_ref,
                  caw_ref, cbw_ref, cbb_ref, lng_ref, lnb_ref, bif_ref, mng_ref,
                  p_ref, sa_ref, sb_ref, c_out_ref, n_out_ref, m_out_ref,
                  xa_s, xb_s, xbr_s, aconv_s, bconv_s, c_s, n_s, m_s):
    L = CHUNK

    @pl.when(t == 0)
    def _():
        xa_s[0:8, :] = jnp.zeros((8, D_A), F32)
        xb_s[0:32, :] = jnp.zeros((32, D_B), F32)
        xa_s[6:8, :] = ca0_ref[0, 0]
        xb_s[2:32, :] = cb0_ref[0, 0]
        c_s[...] = c0_ref[0, 0]
        n_s[0:1, :] = n0_ref[0]
        m_s[0:1, :] = m0_ref[0]

    xa_s[8:8 + L, :] = (zm_ref[:, OFF_AC:OFF_AC + D_A].astype(F32)
                        * zm_ref[:, OFF_AX:OFF_AX + D_A].astype(F32))
    xb_s[32:32 + L, :] = zm_ref[:, OFF_BV:OFF_BV + D_B].astype(F32) * jax.nn.sigmoid(
        zm_ref[:, OFF_BG:OFF_BG + D_B].astype(F32))
    for r in range(1, 8):
        xbr_s[r] = xb_s[r:r + 24 + L, :]

    def conv_block(cblk):
        cs = slice(cblk * 128, (cblk + 1) * 128)
        acc = caw_ref[0:1, cs] * xa_s[6:6 + L, cs]
        for k in range(1, K_A):
            acc = acc + caw_ref[k:k + 1, cs] * xa_s[6 + k:6 + k + L, cs]
        aconv_s[:, cs] = acc
        acc = None
        for k in range(K_B):
            a8, r = (2 + k) // 8 * 8, (2 + k) % 8
            src = xb_s[a8:a8 + L, cs] if r == 0 else xbr_s[r, a8:a8 + L, cs]
            term = cbw_ref[k:k + 1, cs] * src
            acc = term if acc is None else acc + term
        bconv_s[:, cs] = acc

    row = lax.broadcasted_iota(jnp.int32, (L, L), 0)
    col = lax.broadcasted_iota(jnp.int32, (L, L), 1)
    causal = col <= row
    g = gates_ref[...] + bif_ref[...]
    logf = jnp.minimum(g, 0.0) - jnp.log1p(jnp.exp(-jnp.abs(g)))
    bt = _dot_f32(causal.astype(F32), logf)
    bt_h = pltpu.roll(bt, 128 - H_C, axis=1)
    g_t = g.T
    bt_t = bt.T
    inter = bt_h + m_s[0:1, :]
    lane = lax.broadcasted_iota(jnp.int32, (L, W_GATE), 1)
    dlogs = []
    rmax = jnp.zeros((L, W_GATE), F32)
    for h in range(H_C):
        dlog = jnp.where(causal, bt[:, H_C + h:H_C + h + 1] - bt_t[H_C + h:H_C + h + 1, :]
                         + g_t[h:h + 1, :], -jnp.inf)
        dlogs.append(dlog)
        rmax = jnp.where(lane == h, jnp.max(dlog, axis=-1, keepdims=True), rmax)
    m_t = jnp.maximum(inter, rmax)
    w_inter = jnp.exp(inter - m_t)
    floor = jnp.exp(-m_t)
    m_new = m_t[L - 1:L, :]
    w_last = jnp.exp(bt_h[L - 1:L, :] - bt_h + g - m_new)
    w_prev = jnp.exp(inter[L - 1:L, :] - m_new)
    wl16 = w_last.T[0:16, :].astype(BF16)
    n_row = n_s[0:1, :]

    def head(h):
        hs = slice(h * DH_C, (h + 1) * DH_C)
        q = zm_ref[:, OFF_Q + h * DH_C:OFF_Q + (h + 1) * DH_C]
        k = zm_ref[:, OFF_K + h * DH_C:OFF_K + (h + 1) * DH_C]
        v = zm_ref[:, OFF_V + h * DH_C:OFF_V + (h + 1) * DH_C]
        c_old = c_s[h]
        s = _dot_nt(q, k) * (DH_C ** -0.5) * jnp.exp(dlogs[h] - m_t[:, h:h + 1])
        num = _dot(s.astype(BF16), v)
        qc = _dot_nt(q, c_old.astype(BF16))
        qn = _dot_nt(q, jnp.broadcast_to(n_row[:, hs], (L, DH_C)).astype(BF16))[:, 0:1]
        wi = w_inter[:, h:h + 1]
        den = jnp.sum(s, axis=-1, keepdims=True) + wi * qn
        rinv = 1.0 / jnp.maximum(jnp.abs(den), floor[:, h:h + 1])
        hh = (num + qc * wi) * rinv
        hn = hh * lax.rsqrt(jnp.mean(hh * hh, axis=-1, keepdims=True) + EPS) * mng_ref[:, hs]
        o = zm_ref[:, OFF_O + h * DH_C:OFF_O + (h + 1) * DH_C].astype(F32)
        p_ref[:, 2 * D_A + h * DH_C:2 * D_A + (h + 1) * DH_C] = (jax.nn.sigmoid(o) * hn).astype(BF16)
        vw = (v.astype(F32) * w_last[:, h:h + 1]).astype(BF16)
        wp = w_prev[:, h:h + 1]
        c_new = wp * c_old + _dot_tn(vw, k) * (DH_C ** -0.5)
        n_new = wp * n_row[:, hs] + _dot(wl16, k)[h:h + 1, :] * (DH_C ** -0.5)
        c_s[h] = c_new
        c_out_ref[0, 0, h] = c_new
        n_s[0:1, hs] = n_new
        n_out_ref[0, h:h + 1, :] = n_new

    conv_block(0)
    conv_block(1)
    head(0)
    conv_block(2)
    head(1)
    conv_block(3)
    head(2)

    sa_new = xa_s[6 + L:8 + L, :]
    sb_new = xb_s[2 + L:32 + L, :]
    xa_s[6:8, :] = sa_new
    xb_s[2:32, :] = sb_new
    sa_ref[0] = sa_new
    sb_ref[0] = sb_new
    p_ref[:, 0:D_A] = (zm_ref[:, OFF_AB:OFF_AB + D_A].astype(F32) * aconv_s[...]).astype(BF16)
    bc = bconv_s[...] + cbb_ref[...]
    mu = jnp.mean(bc, axis=-1, keepdims=True)
    xc = bc - mu
    ln = xc * lax.rsqrt(jnp.mean(xc * xc, axis=-1, keepdims=True) + EPS) * lng_ref[...] + lnb_ref[...]
    p_ref[:, D_A:D_A + D_B] = (ln * jax.nn.sigmoid(ln)).astype(BF16)

    head(3)
    m_s[0:1, :] = m_new
    m_out_ref[0] = m_new


def _mixer_kernel(*refs, seq_rows, carry, has_acc):
    (zm_ref, gates_ref, ca0_ref, cb0_ref, c0_ref, n0_ref, m0_ref,
     caw_ref, cbw_ref, cbb_ref, lng_ref, lnb_ref, bif_ref, mng_ref) = refs[:14]
    refs = refs[15:] if has_acc else refs[14:]
    (p_ref, sa_ref, sb_ref, c_out_ref, n_out_ref, m_out_ref,
     xa_s, xb_s, xbr_s, aconv_s, bconv_s, c_s, n_s, m_s,
     num_s, qc_s, vw_s, winter_s, rinv_s, wlt_s, wprev_s, mnew_s) = refs
    L = CHUNK
    nseq = L // seq_rows
    t = pl.program_id(1)
    j = pl.program_id(2)

    if nseq == 1:
        assert carry
        _prompt_chunk(t, zm_ref, gates_ref, ca0_ref, cb0_ref, c0_ref, n0_ref, m0_ref,
                      caw_ref, cbw_ref, cbb_ref, lng_ref, lnb_ref, bif_ref, mng_ref,
                      p_ref, sa_ref, sb_ref, c_out_ref, n_out_ref, m_out_ref,
                      xa_s, xb_s, xbr_s, aconv_s, bconv_s, c_s, n_s, m_s)
        return

    def _when(cond):
        return (lambda f: f()) if nseq == 1 else pl.when(cond)

    @_when(j == 0)
    def _pre():
        a_b = zm_ref[:, OFF_AB:OFF_AB + D_A].astype(F32)
        ca = zm_ref[:, OFF_AC:OFF_AC + D_A].astype(F32) * zm_ref[:, OFF_AX:OFF_AX + D_A].astype(F32)
        cb = zm_ref[:, OFF_BV:OFF_BV + D_B].astype(F32) * jax.nn.sigmoid(
            zm_ref[:, OFF_BG:OFF_BG + D_B].astype(F32))

        r_i = lax.broadcasted_iota(jnp.int32, (L, L), 0)
        c_i = lax.broadcasted_iota(jnp.int32, (L, L), 1)
        to_tok = (c_i == (r_i % nseq) * seq_rows + r_i // nseq).astype(F32)
        to_seq = (c_i == (r_i % seq_rows) * nseq + r_i // seq_rows).astype(F32)

        def short_conv(x, st_ref, w_ref, taps, new_ref):
            x_tok = _dot_f32(to_tok, x)
            window = [st_ref[i] for i in range(taps - 1)]
            window += [x_tok[tk * nseq:(tk + 1) * nseq] for tk in range(seq_rows)]
            outs_tok = []
            for tk in range(seq_rows):
                acc = w_ref[0:1, :] * window[tk]
                for k in range(1, taps):
                    acc = acc + w_ref[k:k + 1, :] * window[tk + k]
                outs_tok.append(acc)
            for i in range(taps - 1):
                new_ref[i] = window[seq_rows + i]
            return _dot_f32(to_seq, jnp.concatenate(outs_tok, axis=0))

        aconv_s[...] = short_conv(ca, ca0_ref, caw_ref, K_A, sa_ref)
        bconv_s[...] = short_conv(cb, cb0_ref, cbw_ref, K_B, sb_ref)

        p_ref[:, 0:D_A] = (a_b * aconv_s[...]).astype(BF16)
        bc = bconv_s[...] + cbb_ref[...]
        mu = jnp.mean(bc, axis=-1, keepdims=True)
        xc = bc - mu
        ln = xc * lax.rsqrt(jnp.mean(xc * xc, axis=-1, keepdims=True) + EPS) * lng_ref[...] + lnb_ref[...]
        p_ref[:, D_A:D_A + D_B] = (ln * jax.nn.sigmoid(ln)).astype(BF16)

        row = lax.broadcasted_iota(jnp.int32, (L, L), 0)
        col = lax.broadcasted_iota(jnp.int32, (L, L), 1)
        same = (row // seq_rows) == (col // seq_rows)
        causal = same & (col <= row)
        g = gates_ref[...] + bif_ref[...]
        logf = jnp.minimum(g, 0.0) - jnp.log1p(jnp.exp(-jnp.abs(g)))
        bt = _dot_f32(causal.astype(F32), logf)
        bt_h = pltpu.roll(bt, 128 - H_C, axis=1)
        if nseq == 1:
            btl_h = bt_h[L - 1:L, :]
        else:
            btl_h = pltpu.roll(_dot_f32(same.astype(F32), logf), 128 - H_C, axis=1)
        g_t = g.T
        bt_t = bt.T
        m_rows = m_s[0:1, :] if carry else m0_ref[0]
        inter = bt_h + m_rows
        lane = lax.broadcasted_iota(jnp.int32, (L, W_GATE), 1)

        dlogs = []
        rmax = jnp.zeros((L, W_GATE), F32)
        for h in range(H_C):
            dlog = jnp.where(causal, bt[:, H_C + h:H_C + h + 1] - bt_t[H_C + h:H_C + h + 1, :]
                             + g_t[h:h + 1, :], -jnp.inf)
            dlogs.append(dlog)
            rmax = jnp.where(lane == h, jnp.max(dlog, axis=-1, keepdims=True), rmax)
        m_t = jnp.maximum(inter, rmax)
        w_inter = jnp.exp(inter - m_t)
        floor = jnp.exp(-m_t)
        if nseq == 1:
            m_new = m_t[L - 1:L, :]
        else:
            last = (col == (row // seq_rows) * seq_rows + (seq_rows - 1)).astype(F32)
            m_new = _dot_f32(last, m_t)
        w_last = jnp.exp(btl_h - bt_h + g - m_new)
        w_prev = jnp.exp(inter - m_new)
        n_rows = n_s[0:1, :] if carry else n0_ref[0]

        den = jnp.zeros((L, W_GATE), F32)
        for h in range(H_C):
            hs = slice(h * DH_C, (h + 1) * DH_C)
            q = zm_ref[:, OFF_Q + h * DH_C:OFF_Q + (h + 1) * DH_C]
            k = zm_ref[:, OFF_K + h * DH_C:OFF_K + (h + 1) * DH_C]
            v = zm_ref[:, OFF_V + h * DH_C:OFF_V + (h + 1) * DH_C]
            s = _dot_nt(q, k) * (DH_C ** -0.5) * jnp.exp(dlogs[h] - m_t[:, h:h + 1])
            num_s[:, hs] = _dot(s.astype(BF16), v)
            qn_all = _dot_nt(q, jnp.broadcast_to(n_rows[:, hs], (L, DH_C)).astype(BF16))
            if nseq == 1:
                qn = qn_all[:, 0:1]
            else:
                qn = jnp.sum(jnp.where(row == col, qn_all, 0.0), axis=-1, keepdims=True)
            den_h = jnp.sum(s, axis=-1, keepdims=True) + w_inter[:, h:h + 1] * qn
            den = jnp.where(lane == h, den_h, den)
            vw_s[h] = (v.astype(F32) * w_last[:, h:h + 1]).astype(BF16)
        if nseq > 1:
            qc_s[...] = jnp.zeros(qc_s.shape, F32)
        winter_s[...] = w_inter
        rinv_s[...] = 1.0 / jnp.maximum(jnp.abs(den), floor)
        wlt_s[...] = w_last.T[0:16, :]
        wprev_s[...] = jnp.broadcast_to(w_prev, (L, W_GATE))
        mnew_s[...] = jnp.broadcast_to(m_new, (L, W_GATE))

    spp = c0_ref.shape[1]
    tile = spp * seq_rows if nseq > 1 else L
    for u in range(spp):
        if nseq == 1:
            wprev_row = wprev_s[L - 1:L, :]
            m_out_row = mnew_s[L - 1:L, :]
        else:
            sq = j * spp + u
            last_row = sq * seq_rows + (seq_rows - 1)
            lane_l = lax.broadcasted_iota(jnp.int32, (1, L), 1)
            row_l = lax.broadcasted_iota(jnp.int32, (L, 1), 0)
            seq_lanes = (lane_l // seq_rows) == sq
            seq_rows_mask = (row_l // seq_rows) == sq
            wprev_row = wprev_s[pl.ds(last_row, 1), :]
            m_out_row = mnew_s[pl.ds(last_row, 1), :]
        for h in range(H_C):
            hs = slice(h * DH_C, (h + 1) * DH_C)
            c_old = c_s[h] if carry else c0_ref[0, u, h]
            c_bf = c_old.astype(BF16)
            if nseq == 1:
                qc_s[:, hs] = _dot_nt(zm_ref[:, OFF_Q + h * DH_C:OFF_Q + (h + 1) * DH_C], c_bf)
            else:
                r0 = pl.multiple_of(j * tile, tile)
                q16 = zm_ref[pl.ds(r0, tile), OFF_Q + h * DH_C:OFF_Q + (h + 1) * DH_C]
                r = _dot_nt(q16, c_bf)
                rid = lax.broadcasted_iota(jnp.int32, (tile, 1), 0)
                qc_s[pl.ds(r0, tile), hs] = jnp.where((rid // seq_rows) == u, r, qc_s[pl.ds(r0, tile), hs])
            k = zm_ref[:, OFF_K + h * DH_C:OFF_K + (h + 1) * DH_C]
            if nseq == 1:
                vw = vw_s[h]
                wl = wlt_s[...]
            else:
                vw = jnp.where(seq_rows_mask, vw_s[h], jnp.zeros((L, DH_C), BF16))
                wl = jnp.where(seq_lanes, wlt_s[...], 0.0)
            kv = _dot_tn(vw, k)
            w_prev = wprev_row[:, h:h + 1]
            c_new = w_prev * c_old + kv * (DH_C ** -0.5)
            ksum = _dot(wl.astype(BF16), k)[h:h + 1, :]
            if carry:
                n_old = n_s[0:1, hs]
            else:
                n_old = n0_ref[0, pl.ds(sq * seq_rows, 1), hs]
            n_new = w_prev * n_old + ksum * (DH_C ** -0.5)
            if carry:
                c_s[h] = c_new
                n_s[0:1, hs] = n_new
            c_out_ref[0, u, h] = c_new
            n_out_ref[u, h:h + 1, :] = n_new
        if carry:
            m_s[0:1, :] = m_out_row
        m_out_ref[u] = m_out_row

    @_when(j == nseq // spp - 1)
    def _post():
        for h in range(H_C):
            hs = slice(h * DH_C, (h + 1) * DH_C)
            hh = (num_s[:, hs] + qc_s[:, hs] * winter_s[:, h:h + 1]) * rinv_s[:, h:h + 1]
            hn = hh * lax.rsqrt(jnp.mean(hh * hh, axis=-1, keepdims=True) + EPS) * mng_ref[:, hs]
            o = zm_ref[:, OFF_O + h * DH_C:OFF_O + (h + 1) * DH_C].astype(F32)
            p_ref[:, 2 * D_A + h * DH_C:2 * D_A + (h + 1) * DH_C] = (jax.nn.sigmoid(o) * hn).astype(BF16)


def _mixer(zm, gates, ca0, cb0, c0, n0, m0, caw, cbw, cbb, lng, lnb, bif, mng, c_acc, *,
           row_block0, n_groups, n_chunks, seq_rows, carry, layer, layer_in, layer_out):
    L = CHUNK
    nseq = L // seq_rows
    n_state = n_groups if carry else n_chunks * nseq
    rows = n_groups * n_chunks * L

    def tok(b, t, j):
        return (row_block0 + b * n_chunks + t, 0)

    def out_tok(b, t, j):
        return (b * n_chunks + t, 0)

    if carry:
        def seq_of(b, t, j):
            return b

        def conv_of(b, t, j):
            return b
        rows_blk = 1
        spp = 1

        def conv_in(taps, ch):
            return pl.BlockSpec((1, 1, taps - 1, ch), lambda *g: (layer_in, conv_of(*g), 0, 0))

        def conv_out(taps, ch):
            return (pl.BlockSpec((1, taps - 1, ch), lambda *g: (conv_of(*g), 0, 0)),
                    jax.ShapeDtypeStruct((n_state, taps - 1, ch), F32))
    else:
        spp = 16 // seq_rows

        def seq_of(b, t, j):
            return t * (nseq // spp) + j

        def conv_of(b, t, j):
            return t
        rows_blk = L

        def conv_in(taps, ch):
            return pl.BlockSpec((None, taps - 1, nseq, ch), lambda *g: (layer_in, 0, conv_of(*g), 0))

        def conv_out(taps, ch):
            return (pl.BlockSpec((taps - 1, nseq, ch), lambda *g: (0, conv_of(*g), 0)),
                    jax.ShapeDtypeStruct((taps - 1, n_state, ch), F32))

    has_acc = c_acc is not None
    kern = functools.partial(_mixer_kernel, seq_rows=seq_rows, carry=carry, has_acc=has_acc)
    in_specs = [
        pl.BlockSpec((L, W_MAIN), tok),
        pl.BlockSpec((L, W_GATE), tok),
        conv_in(K_A, D_A),
        conv_in(K_B, D_B),
        pl.BlockSpec((1, spp, H_C, DH_C, DH_C), lambda *g: (layer_in, seq_of(*g), 0, 0, 0)),
        pl.BlockSpec((None, 1, rows_blk, D_MODEL), lambda *g: (layer_in, conv_of(*g), 0, 0)),
        pl.BlockSpec((None, 1, rows_blk, W_GATE), lambda *g: (layer_in, conv_of(*g), 0, 0)),
        _resident((K_A, D_A), layer),
        _resident((K_B, D_B), layer),
        _resident((1, D_B), layer),
        _resident((1, D_B), layer),
        _resident((1, D_B), layer),
        _resident((1, W_GATE), layer),
        _resident((1, D_MODEL), layer),
    ]
    args = [zm, gates, ca0, cb0, c0, n0, m0, caw, cbw, cbb, lng, lnb, bif, mng]
    aliases = {}
    if has_acc:
        in_specs.append(pl.BlockSpec(memory_space=pl.ANY))
        args.append(c_acc)
        aliases = {len(args) - 1: 3}
    return pl.pallas_call(
        kern,
        grid=(n_groups, n_chunks, nseq // spp),
        in_specs=in_specs,
        out_specs=[
            pl.BlockSpec((L, W_P), out_tok),
            conv_out(K_A, D_A)[0],
            conv_out(K_B, D_B)[0],
            pl.BlockSpec((1, spp, H_C, DH_C, DH_C), lambda *g: (layer_out, seq_of(*g), 0, 0, 0)),
            pl.BlockSpec((spp, H_C, DH_C), lambda *g: (seq_of(*g), 0, 0)),
            pl.BlockSpec((spp, 1, W_GATE), lambda *g: (seq_of(*g), 0, 0)),
        ],
        out_shape=[
            jax.ShapeDtypeStruct((rows, W_P), BF16),
            conv_out(K_A, D_A)[1],
            conv_out(K_B, D_B)[1],
            jax.ShapeDtypeStruct((DEPTH, n_state, H_C, DH_C, DH_C), F32),
            jax.ShapeDtypeStruct((n_state, H_C, DH_C), F32),
            jax.ShapeDtypeStruct((n_state, 1, W_GATE), F32),
        ],
        input_output_aliases=aliases,
        scratch_shapes=[
            pltpu.VMEM((8 + L, D_A) if carry else (16, D_A), F32),
            pltpu.VMEM((32 + L, D_B) if carry else (40, D_B), F32),
            pltpu.VMEM((8, 24 + L, D_B) if carry else (8, 8, 128), F32),
            pltpu.VMEM((L, D_A), F32),
            pltpu.VMEM((L, D_B), F32),
            pltpu.VMEM((H_C, DH_C, DH_C), F32),
            pltpu.VMEM((8, D_MODEL), F32),
            pltpu.VMEM((8, W_GATE), F32),
            pltpu.VMEM((L, D_MODEL), F32),
            pltpu.VMEM((L, D_MODEL), F32),
            pltpu.VMEM((H_C, L, DH_C), BF16),
            pltpu.VMEM((L, W_GATE), F32),
            pltpu.VMEM((L, W_GATE), F32),
            pltpu.VMEM((16, L), F32),
            pltpu.VMEM((L, W_GATE), F32),
            pltpu.VMEM((L, W_GATE), F32),
        ],
        compiler_params=_params(("arbitrary", "arbitrary", "arbitrary")),
        name="mixer_prompt" if carry else "mixer_sample",
    )(*args)


def _outproj_kernel(x_ref, pp_ref, ps_ref, zg_ref, wa_ref, wb_ref, wc_ref, wo_ref, gx_ref, wq_ref,
                    x1_ref, q_ref, *, n_prompt_tiles):
    i = pl.program_id(0)
    p = jnp.where(i >= n_prompt_tiles, ps_ref[...], pp_ref[...])
    y_a = _dot(p[:, 0:D_A], wa_ref[...])
    y_b = _dot(p[:, D_A:D_A + D_B], wb_ref[...])
    y_c = _dot(p[:, D_A + D_B:], wc_ref[...])
    u = (jax.nn.sigmoid(zg_ref[:, 0:D_MODEL].astype(F32)) * y_a
         + jax.nn.sigmoid(zg_ref[:, D_MODEL:2 * D_MODEL].astype(F32)) * y_b
         + jax.nn.sigmoid(zg_ref[:, 2 * D_MODEL:].astype(F32)) * y_c)
    x1 = x_ref[...] + _dot(u.astype(BF16), wo_ref[...])
    x1_ref[...] = x1
    q_ref[...] = _dot(_rmsnorm(x1, gx_ref[...]).astype(BF16), wq_ref[...]).astype(BF16)


def _outproj(x, p_p, p_s, zg, wa, wb, wc, wo, gx, wq, layer):
    n = x.shape[0]
    tm = ROW_TILE
    npt = p_p.shape[0] // tm
    return pl.pallas_call(
        functools.partial(_outproj_kernel, n_prompt_tiles=npt),
        grid=(n // tm,),
        in_specs=[
            pl.BlockSpec((tm, D_MODEL), lambda i: (i, 0)),
            pl.BlockSpec((tm, W_P), lambda i: (jnp.minimum(i, npt - 1), 0)),
            pl.BlockSpec((tm, W_P), lambda i: (0, 0)),
            pl.BlockSpec((tm, W_MERGE), lambda i: (i, 0)),
            _resident((D_A, D_MODEL), layer),
            _resident((D_B, D_MODEL), layer),
            _resident((D_MODEL, D_MODEL), layer),
            _resident((D_MODEL, D_MODEL), layer),
            _resident((1, D_MODEL), layer),
            _resident((D_MODEL, D_MODEL), layer),
        ],
        out_specs=[
            pl.BlockSpec((tm, D_MODEL), lambda i: (i, 0)),
            pl.BlockSpec((tm, D_MODEL), lambda i: (i, 0)),
        ],
        out_shape=[
            jax.ShapeDtypeStruct((n, D_MODEL), F32),
            jax.ShapeDtypeStruct((n, D_MODEL), BF16),
        ],
        compiler_params=_params(("parallel",)),
        name="outproj",
    )(x, p_p, p_s, zg, wa, wb, wc, wo, gx, wq)


def _memkv_kernel(mem_ref, g_ref, w_ref, k_ref, v_ref, kb_ref, vb_ref):
    h = _rmsnorm(mem_ref[...], g_ref[0]).astype(BF16)
    hd = HX * DX
    k = _dot(h, w_ref[0, :, 0:hd])
    v = _dot(h, w_ref[0, :, hd:])
    for bb in range(k_ref.shape[1]):
        for hh in range(HX):
            k_ref[0, bb, :, hh, :] = k[bb * N_MEM:(bb + 1) * N_MEM, hh * DX:(hh + 1) * DX]
            v_ref[0, bb, :, hh, :] = v[bb * N_MEM:(bb + 1) * N_MEM, hh * DX:(hh + 1) * DX]
    kb_ref[0] = k.astype(BF16)
    vb_ref[0] = v.astype(BF16)


def _memkv(mem, g, w):
    n = mem.shape[0]
    tm = ROW_TILE
    hd = HX * DX
    per = tm // N_MEM
    o_spec = pl.BlockSpec((1, tm, hd), lambda l, i: (l, i, 0))
    o5_spec = pl.BlockSpec((1, per, N_MEM, HX, DX), lambda l, i: (l, i, 0, 0, 0))
    return pl.pallas_call(
        _memkv_kernel,
        grid=(DEPTH, n // tm),
        in_specs=[
            pl.BlockSpec((tm, D_MODEL), lambda l, i: (i, 0)),
            pl.BlockSpec((1, 1, D_MODEL), lambda l, i: (l, 0, 0)),
            pl.BlockSpec((1, D_MODEL, 2 * hd), lambda l, i: (l, 0, 0)),
        ],
        out_specs=[o5_spec, o5_spec, o_spec, o_spec],
        out_shape=[
            jax.ShapeDtypeStruct((DEPTH, n // N_MEM, N_MEM, HX, DX), F32),
            jax.ShapeDtypeStruct((DEPTH, n // N_MEM, N_MEM, HX, DX), F32),
            jax.ShapeDtypeStruct((DEPTH, n, hd), BF16),
            jax.ShapeDtypeStruct((DEPTH, n, hd), BF16),
        ],
        compiler_params=_params(("arbitrary", "arbitrary")),
        name="memkv",
    )(mem, g, w)


def _attend(q, k, v):
    s = _dot_nt(q, k) * (DX ** -0.5)
    e = jnp.exp(s - jnp.max(s, axis=-1, keepdims=True))
    return _dot(e.astype(BF16), v) * (1.0 / jnp.sum(e, axis=-1, keepdims=True))


def _xattn_prompt_kernel(q_ref, k_ref, v_ref, o_ref):
    for h in range(HX):
        hs = slice(h * DX, (h + 1) * DX)
        o_ref[:, hs] = _attend(q_ref[:, hs], k_ref[0, :, hs], v_ref[0, :, hs]).astype(BF16)


def _xattn_prompt(q, kb, vb, layer, *, n_groups, rows_per_group):
    tq = ROW_TILE
    nt = rows_per_group // tq
    hd = HX * DX
    kv_spec = pl.BlockSpec((1, N_MEM, hd), lambda b, t: (layer, b, 0))
    return pl.pallas_call(
        _xattn_prompt_kernel,
        grid=(n_groups, nt),
        in_specs=[pl.BlockSpec((tq, hd), lambda b, t: (b * nt + t, 0)), kv_spec, kv_spec],
        out_specs=pl.BlockSpec((tq, hd), lambda b, t: (b * nt + t, 0)),
        out_shape=jax.ShapeDtypeStruct((n_groups * rows_per_group, hd), BF16),
        compiler_params=_params(("parallel", "parallel")),
        name="xattn_prompt",
    )(q, kb, vb)


def _xattn_sample_kernel(q_ref, k_ref, v_ref, o_ref, *, seq_rows):
    rows = q_ref.shape[0]
    qs = jnp.concatenate([q_ref[:, h * DX:(h + 1) * DX] for h in range(HX)], axis=0)
    rid = lax.broadcasted_iota(jnp.int32, (HX * rows, 1), 0)
    cid = lax.broadcasted_iota(jnp.int32, (HX * rows, N_MEM * HX), 1)
    own_head = (cid % HX) == (rid // rows)
    acc = jnp.zeros((HX * rows, DX), F32)
    for e in range(rows // seq_rows):
        k2 = k_ref[0, e].reshape(N_MEM * HX, DX).astype(BF16)
        v2 = v_ref[0, e].reshape(N_MEM * HX, DX).astype(BF16)
        s = jnp.where(own_head, _dot_nt(qs, k2) * (DX ** -0.5), -jnp.inf)
        p = jnp.exp(s - jnp.max(s, axis=-1, keepdims=True))
        o = _dot(p.astype(BF16), v2) * (1.0 / jnp.sum(p, axis=-1, keepdims=True))
        acc = jnp.where(((rid % rows) // seq_rows) == e, o, acc)
    for h in range(HX):
        o_ref[:, h * DX:(h + 1) * DX] = acc[h * rows:(h + 1) * rows].astype(BF16)


XS_ROWS = 16


def _xattn_sample(q, k, v, layer, *, row_block0, n_seq, seq_rows):
    rows = XS_ROWS
    per = rows // seq_rows
    hd = HX * DX
    kv_spec = pl.BlockSpec((1, per, N_MEM, HX, DX), lambda i: (layer, i, 0, 0, 0))
    return pl.pallas_call(
        functools.partial(_xattn_sample_kernel, seq_rows=seq_rows),
        grid=(n_seq // per,),
        in_specs=[pl.BlockSpec((rows, hd), lambda i: (row_block0 + i, 0)), kv_spec, kv_spec],
        out_specs=pl.BlockSpec((rows, hd), lambda i: (i, 0)),
        out_shape=jax.ShapeDtypeStruct((n_seq * seq_rows, hd), BF16),
        compiler_params=_params(("parallel",)),
        name="xattn_sample",
    )(q, k, v)


FF_BLOCK = 256


def _ffn_kernel(x_ref, cp_ref, cs_ref, wxo_ref, gf_ref, wg_ref, wu_ref, wout_ref, gfin_ref,
                *rest, n_prompt_tiles, final):
    act_s = rest[-1]
    i = pl.program_id(0)
    ctx = jnp.where(i >= n_prompt_tiles, cs_ref[...], cp_ref[...])
    x2 = x_ref[...] + _dot(ctx, wxo_ref[...])
    h = _rmsnorm(x2, gf_ref[...]).astype(BF16)
    for jb in range(D_FF // FF_BLOCK):
        sl = slice(jb * FF_BLOCK, (jb + 1) * FF_BLOCK)
        gate = _dot(h, wg_ref[:, sl])
        up = _dot(h, wu_ref[:, sl])
        act_s[:, sl] = (gate * jax.nn.sigmoid(gate) * up).astype(BF16)
    x3 = x2 + _dot(act_s[...], wout_ref[...])
    if final:
        y = _rmsnorm(x3, gfin_ref[...])
        yp_ref, ys_ref = rest[0], rest[1]

        @pl.when(i < n_prompt_tiles)
        def _():
            yp_ref[...] = y

        @pl.when(i >= n_prompt_tiles)
        def _():
            ys_ref[...] = y
    else:
        rest[0][...] = x3


def _ffn(x, c_p, c_s, wxo, gf, w_in, wout, gfin, layer, *, final):
    n = x.shape[0]
    tm = ROW_TILE
    npt = c_p.shape[0] // tm
    if final:
        assert n - npt * tm == tm
        out_specs = [pl.BlockSpec((tm, D_MODEL), lambda i: (jnp.minimum(i, npt - 1), 0)),
                     pl.BlockSpec((tm, D_MODEL), lambda i: (0, 0))]
        out_shape = [jax.ShapeDtypeStruct((npt * tm, D_MODEL), F32),
                     jax.ShapeDtypeStruct((tm, D_MODEL), F32)]
        sem = ("arbitrary",)
    else:
        out_specs = pl.BlockSpec((tm, D_MODEL), lambda i: (i, 0))
        out_shape = jax.ShapeDtypeStruct((n, D_MODEL), F32)
        sem = ("parallel",)
    return pl.pallas_call(
        functools.partial(_ffn_kernel, n_prompt_tiles=npt, final=final),
        grid=(n // tm,),
        in_specs=[
            pl.BlockSpec((tm, D_MODEL), lambda i: (i, 0)),
            pl.BlockSpec((tm, D_MODEL), lambda i: (jnp.minimum(i, npt - 1), 0)),
            pl.BlockSpec((tm, D_MODEL), lambda i: (0, 0)),
            _resident((D_MODEL, D_MODEL), layer),
            _resident((1, D_MODEL), layer),
            _resident((D_MODEL, D_FF), layer, col=0),
            _resident((D_MODEL, D_FF), layer, col=1),
            _resident((D_FF, D_MODEL), layer),
            _resident((1, D_MODEL)),
        ],
        out_specs=out_specs,
        out_shape=out_shape,
        scratch_shapes=[pltpu.VMEM((tm, D_FF), BF16)],
        compiler_params=_params(sem),
        name="ffn",
    )(x, c_p, c_s, wxo, gf, w_in, w_in, wout, gfin)


def kernel(x_prompt, x_sample, state_conv_a, state_conv_b, state_mlstm_c, state_mlstm_n, state_mlstm_m,
           cache_mem_k, cache_mem_v, mem_prompt, norm_mix_g, w_in, b_if, conv_a_w, w_out_a, conv_b_w,
           conv_b_b, ln_b_g, ln_b_b, w_out_b, mlstm_norm_g, w_out_c, w_o, norm_x_g, norm_mem_g, w_xq,
           w_xkv, w_xo, norm_ffn_g, w_ffn_in, w_ffn_out, final_norm_g):
    bp, tp, d = x_prompt.shape
    bs, ts, _ = x_sample.shape
    n_p = bp * tp
    n_s = bs * ts
    hd = HX * DX

    gate_lo = W_MAIN
    gate_hi = W_MAIN + 2 * H_C
    w_inb = w_in.astype(BF16)
    w_gate = jnp.pad(w_inb[:, :, gate_lo:gate_hi], ((0, 0), (0, 0), (0, W_GATE - 2 * H_C)))
    w_merge = w_inb[:, :, gate_hi:]
    bif = jnp.pad(b_if, ((0, 0), (0, W_GATE - 2 * H_C)))[:, None, :]
    w_a = w_out_a.astype(BF16)
    w_b = w_out_b.astype(BF16)
    w_c = w_out_c.astype(BF16)
    w_ob = w_o.astype(BF16)
    w_q = w_xq.astype(BF16)
    w_kv = w_xkv.astype(BF16)
    w_xob = w_xo.astype(BF16)
    w_fi = w_ffn_in.astype(BF16)
    w_fo = w_ffn_out.astype(BF16)
    g_mix = norm_mix_g[:, None, :]
    g_x = norm_x_g[:, None, :]
    g_ffn = norm_ffn_g[:, None, :]
    small = (conv_a_w, conv_b_w, conv_b_b[:, None, :], ln_b_g[:, None, :], ln_b_b[:, None, :],
             bif, mlstm_norm_g[:, None, :])

    x = jnp.concatenate([x_prompt.reshape(n_p, d), x_sample.reshape(n_s, d)], axis=0)

    mem_k, mem_v, mem_kb, mem_vb = _memkv(mem_prompt.reshape(bp * N_MEM, d), norm_mem_g[:, None, :], w_kv)

    zeros_ca = jnp.zeros((1, bp, K_A - 1, D_A), F32)
    zeros_cb = jnp.zeros((1, bp, K_B - 1, D_B), F32)
    zeros_c = jnp.zeros((1, bp, H_C, DH_C, DH_C), F32)
    zeros_n = jnp.zeros((1, bp, 1, H_C * DH_C), F32)
    zeros_m = jnp.zeros((1, bp, 1, W_GATE), F32)

    n_chunks_s = n_s // CHUNK
    n0_rows = jnp.repeat(state_mlstm_n.reshape(DEPTH, bs, H_C * DH_C), ts, axis=1)
    n0_rows = n0_rows.reshape(DEPTH, n_chunks_s, CHUNK, H_C * DH_C)
    m0_rows = jnp.repeat(jnp.pad(state_mlstm_m, ((0, 0), (0, 0), (0, W_GATE - H_C))), ts, axis=1)
    m0_rows = m0_rows.reshape(DEPTH, n_chunks_s, CHUNK, W_GATE)

    conv_a_tm = jnp.transpose(state_conv_a, (0, 2, 1, 3))
    conv_b_tm = jnp.transpose(state_conv_b, (0, 2, 1, 3))

    outs = {k: [] for k in ("pa", "pb", "pn", "pm", "sa", "sb", "sn", "sm")}
    c1 = c2 = None
    for l in range(DEPTH):
        zm, zg, gates = _inproj(x, g_mix, w_inb, w_merge, w_gate, l)
        p_p, a1, b1, c1, n1, m1 = _mixer(
            zm, gates, zeros_ca, zeros_cb, zeros_c, zeros_n, zeros_m, *small, c1,
            row_block0=0, n_groups=bp, n_chunks=tp // CHUNK, seq_rows=CHUNK, carry=True,
            layer=l, layer_in=0, layer_out=l)
        p_s, a2, b2, c2, n2, m2 = _mixer(
            zm, gates, conv_a_tm, conv_b_tm, state_mlstm_c, n0_rows, m0_rows, *small, c2,
            row_block0=n_p // CHUNK, n_groups=1, n_chunks=n_chunks_s, seq_rows=ts, carry=False,
            layer=l, layer_in=l, layer_out=l)
        x1, qx = _outproj(x, p_p, p_s, zg, w_a, w_b, w_c, w_ob, g_x, w_q, l)
        c_p = _xattn_prompt(qx, mem_kb, mem_vb, l, n_groups=bp, rows_per_group=tp)
        c_s = _xattn_sample(qx, cache_mem_k, cache_mem_v, l, row_block0=n_p // XS_ROWS, n_seq=bs, seq_rows=ts)
        x = _ffn(x1, c_p, c_s, w_xob, g_ffn, w_fi, w_fo, final_norm_g[None, :], l, final=(l == DEPTH - 1))
        outs["pa"].append(a1); outs["pb"].append(b1); outs["pn"].append(n1); outs["pm"].append(m1)
        outs["sa"].append(a2); outs["sb"].append(b2); outs["sn"].append(n2); outs["sm"].append(m2)

    y_prompt = x[0].reshape(bp, tp, d)
    y_sample = x[1].reshape(bs, ts, d)
    st = {k: jnp.stack(v) for k, v in outs.items()}
    return (y_prompt, y_sample,
            st["pa"], st["pb"], c1, st["pn"], st["pm"][:, :, 0, :H_C],
            mem_k, mem_v,
            jnp.transpose(st["sa"], (0, 2, 1, 3)), jnp.transpose(st["sb"], (0, 2, 1, 3)), c2,
            st["sn"], st["sm"][:, :, 0, :H_C])
```

```python
import functools

import jax
import jax.numpy as jnp
from jax import lax
from jax.experimental import pallas as pl
from jax.experimental.pallas import tpu as pltpu

D_MODEL = 1024
DEPTH = 4
D_A = 512
K_A = 3
D_B = 512
K_B = 31
H_C = 4
DH_C = 256
N_MEM = 256
HX = 4
DX = 256
D_FF = 2816
EPS = 1e-6

OFF_AB, OFF_AC, OFF_AX, OFF_BV, OFF_BG = 0, 512, 1024, 1536, 2048
OFF_Q, OFF_K, OFF_V, OFF_O = 2560, 3584, 4608, 5632
W_MAIN = 6656
W_MERGE = 3 * D_MODEL
W_GATE = 128
W_P = 2 * D_A + D_MODEL

CHUNK = 128
PROMPT_PAIR = 1
ROW_TILE = 512
VMEM_LIMIT_BYTES = 56 * 1024 * 1024

F32 = jnp.float32
BF16 = jnp.bfloat16


def _dot(a, b):
    return jnp.dot(a, b, preferred_element_type=F32)


def _dot_nt(a, b):
    return lax.dot_general(a, b, (((1,), (1,)), ((), ())), preferred_element_type=F32)


def _dot_tn(a, b):
    return lax.dot_general(a, b, (((0,), (0,)), ((), ())), preferred_element_type=F32)


def _dot_f32(a, b):
    return jnp.dot(a, b, preferred_element_type=F32, precision=lax.Precision.HIGHEST)


def _rmsnorm(x, g):
    ms = jnp.mean(x * x, axis=-1, keepdims=True)
    return x * lax.rsqrt(ms + EPS) * g


def _resident(shape, layer=None, col=0):
    nd = len(shape)
    if layer is None:
        return pl.BlockSpec(shape, lambda *_: (0,) * nd, pipeline_mode=pl.Buffered(1))
    return pl.BlockSpec((None,) + shape, lambda *_: (layer,) + (0,) * (nd - 1) + (col,),
                        pipeline_mode=pl.Buffered(1))


def _params(sem):
    return pltpu.CompilerParams(dimension_semantics=sem, vmem_limit_bytes=VMEM_LIMIT_BYTES)


def _inproj_kernel(x_ref, g_ref, wm_ref, wg_ref, wgate_ref, zm_ref, zg_ref, gates_ref):
    h = _rmsnorm(x_ref[...], g_ref[...]).astype(BF16)
    for j in range(W_MAIN // 512):
        sl = slice(j * 512, (j + 1) * 512)
        zm_ref[:, sl] = _dot(h, wm_ref[:, sl]).astype(BF16)
    for j in range(W_MERGE // 512):
        sl = slice(j * 512, (j + 1) * 512)
        zg_ref[:, sl] = _dot(h, wg_ref[:, sl]).astype(BF16)
    gates_ref[...] = _dot(h, wgate_ref[...])


def _inproj(x, g, wm, wg, wgate, layer):
    n = x.shape[0]
    tm = ROW_TILE
    return pl.pallas_call(
        _inproj_kernel,
        grid=(n // tm,),
        in_specs=[
            pl.BlockSpec((tm, D_MODEL), lambda i: (i, 0)),
            _resident((1, D_MODEL), layer),
            _resident((D_MODEL, W_MAIN), layer),
            _resident((D_MODEL, W_MERGE), layer),
            _resident((D_MODEL, W_GATE), layer),
        ],
        out_specs=[
            pl.BlockSpec((tm, W_MAIN), lambda i: (i, 0)),
            pl.BlockSpec((tm, W_MERGE), lambda i: (i, 0)),
            pl.BlockSpec((tm, W_GATE), lambda i: (i, 0)),
        ],
        out_shape=[
            jax.ShapeDtypeStruct((n, W_MAIN), BF16),
            jax.ShapeDtypeStruct((n, W_MERGE), BF16),
            jax.ShapeDtypeStruct((n, W_GATE), F32),
        ],
        compiler_params=_params(("parallel",)),
        name="inproj",
    )(x, g, wm, wg, wgate)


def _prompt_init(ca0_ref, cb0_ref, c0_ref, n0_ref, m0_ref, xa_s, xb_s, c_s, n_s, m_s):
    xa_s[0:8, :] = jnp.zeros((8, D_A), F32)
    xb_s[0:32, :] = jnp.zeros((32, D_B), F32)
    xa_s[6:8, :] = ca0_ref[...]
    xb_s[2:32, :] = cb0_ref[...]
    c_s[...] = c0_ref[...]
    n_s[0:1, :] = n0_ref[...]
    m_s[0:1, :] = m0_ref[...]


def _prompt_chunk(zm_ref, gates_ref, caw_ref, cbw_ref, cbb_ref, lng_ref, lnb_ref, bif_ref, mng_ref,
                  p_ref, sa_ref, sb_ref, c_out_ref, n_out_ref, m_out_ref,
                  xa_s, xb_s, xbr_s, aconv_s, bconv_s, c_s, n_s, m_s):
    L = CHUNK

    xa_s[8:8 + L, :] = (zm_ref[:, OFF_AC:OFF_AC + D_A].astype(F32)
                        * zm_ref[:, OFF_AX:OFF_AX + D_A].astype(F32))
    xb_s[32:32 + L, :] = zm_ref[:, OFF_BV:OFF_BV + D_B].astype(F32) * jax.nn.sigmoid(
        zm_ref[:, OFF_BG:OFF_BG + D_B].astype(F32))
    for r in range(1, 8):
        xbr_s[r] = xb_s[r:r + 24 + L, :]

    def conv_block(cblk):
        cs = slice(cblk * 128, (cblk + 1) * 128)
        acc = caw_ref[0:1, cs] * xa_s[6:6 + L, cs]
        for k in range(1, K_A):
            acc = acc + caw_ref[k:k + 1, cs] * xa_s[6 + k:6 + k + L, cs]
        aconv_s[:, cs] = acc
        acc = None
        for k in range(K_B):
            a8, r = (2 + k) // 8 * 8, (2 + k) % 8
            src = xb_s[a8:a8 + L, cs] if r == 0 else xbr_s[r, a8:a8 + L, cs]
            term = cbw_ref[k:k + 1, cs] * src
            acc = term if acc is None else acc + term
        bconv_s[:, cs] = acc

    row = lax.broadcasted_iota(jnp.int32, (L, L), 0)
    col = lax.broadcasted_iota(jnp.int32, (L, L), 1)
    causal = col <= row
    g = gates_ref[...] + bif_ref[...]
    logf = jnp.minimum(g, 0.0) - jnp.log1p(jnp.exp(-jnp.abs(g)))
    bt = _dot_f32(causal.astype(F32), logf)
    bt_h = pltpu.roll(bt, 128 - H_C, axis=1)
    g_t = g.T
    bt_t = bt.T
    inter = bt_h + m_s[0:1, :]
    lane = lax.broadcasted_iota(jnp.int32, (L, W_GATE), 1)
    dlogs = []
    rmax = jnp.zeros((L, W_GATE), F32)
    for h in range(H_C):
        dlog = jnp.where(causal, bt[:, H_C + h:H_C + h + 1] - bt_t[H_C + h:H_C + h + 1, :]
                         + g_t[h:h + 1, :], -jnp.inf)
        dlogs.append(dlog)
        rmax = jnp.where(lane == h, jnp.max(dlog, axis=-1, keepdims=True), rmax)
    m_t = jnp.maximum(inter, rmax)
    w_inter = jnp.exp(inter - m_t)
    floor = jnp.exp(-m_t)
    m_new = m_t[L - 1:L, :]
    w_last = jnp.exp(bt_h[L - 1:L, :] - bt_h + g - m_new)
    w_prev = jnp.exp(inter[L - 1:L, :] - m_new)
    wl16 = w_last.T[0:16, :].astype(BF16)
    n_row = n_s[0:1, :]

    def head(h):
        hs = slice(h * DH_C, (h + 1) * DH_C)
        q = zm_ref[:, OFF_Q + h * DH_C:OFF_Q + (h + 1) * DH_C]
        k = zm_ref[:, OFF_K + h * DH_C:OFF_K + (h + 1) * DH_C]
        v = zm_ref[:, OFF_V + h * DH_C:OFF_V + (h + 1) * DH_C]
        c_old = c_s[h]
        s = _dot_nt(q, k) * (DH_C ** -0.5) * jnp.exp(dlogs[h] - m_t[:, h:h + 1])
        num = _dot(s.astype(BF16), v)
        qc = _dot_nt(q, c_old.astype(BF16))
        qn = _dot_nt(q, jnp.broadcast_to(n_row[:, hs], (L, DH_C)).astype(BF16))[:, 0:1]
        wi = w_inter[:, h:h + 1]
        den = jnp.sum(s, axis=-1, keepdims=True) + wi * qn
        rinv = 1.0 / jnp.maximum(jnp.abs(den), floor[:, h:h + 1])
        hh = (num + qc * wi) * rinv
        hn = hh * lax.rsqrt(jnp.mean(hh * hh, axis=-1, keepdims=True) + EPS) * mng_ref[:, hs]
        o = zm_ref[:, OFF_O + h * DH_C:OFF_O + (h + 1) * DH_C].astype(F32)
        p_ref[:, 2 * D_A + h * DH_C:2 * D_A + (h + 1) * DH_C] = (jax.nn.sigmoid(o) * hn).astype(BF16)
        vw = (v.astype(F32) * w_last[:, h:h + 1]).astype(BF16)
        wp = w_prev[:, h:h + 1]
        c_new = wp * c_old + _dot_tn(vw, k) * (DH_C ** -0.5)
        n_new = wp * n_row[:, hs] + _dot(wl16, k)[h:h + 1, :] * (DH_C ** -0.5)
        c_s[h] = c_new
        c_out_ref[h] = c_new
        n_s[0:1, hs] = n_new
        n_out_ref[h:h + 1, :] = n_new

    conv_block(0)
    conv_block(1)
    head(0)
    conv_block(2)
    head(1)
    conv_block(3)
    head(2)

    sa_new = xa_s[6 + L:8 + L, :]
    sb_new = xb_s[2 + L:32 + L, :]
    xa_s[6:8, :] = sa_new
    xb_s[2:32, :] = sb_new
    sa_ref[...] = sa_new
    sb_ref[...] = sb_new
    p_ref[:, 0:D_A] = (zm_ref[:, OFF_AB:OFF_AB + D_A].astype(F32) * aconv_s[...]).astype(BF16)
    bc = bconv_s[...] + cbb_ref[...]
    mu = jnp.mean(bc, axis=-1, keepdims=True)
    xc = bc - mu
    ln = xc * lax.rsqrt(jnp.mean(xc * xc, axis=-1, keepdims=True) + EPS) * lng_ref[...] + lnb_ref[...]
    p_ref[:, D_A:D_A + D_B] = (ln * jax.nn.sigmoid(ln)).astype(BF16)

    head(3)
    m_s[0:1, :] = m_new
    m_out_ref[...] = m_new


def _mixer_kernel(*refs, seq_rows, carry, has_acc):
    pair = PROMPT_PAIR if carry else 1
    zm_refs, gates_refs, refs = refs[:pair], refs[pair:2 * pair], refs[2 * pair:]
    (ca0_ref, cb0_ref, c0_ref, n0_ref, m0_ref,
     caw_ref, cbw_ref, cbb_ref, lng_ref, lnb_ref, bif_ref, mng_ref) = refs[:12]
    refs = refs[13:] if has_acc else refs[12:]
    (p_ref, sa_ref, sb_ref, c_out_ref, n_out_ref, m_out_ref,
     xa_s, xb_s, xbr_s, aconv_s, bconv_s, c_s, n_s, m_s,
     num_s, qc_s, vw_s, winter_s, rinv_s, wlt_s, wprev_s, mnew_s) = refs
    zm_ref, gates_ref = zm_refs[0], gates_refs[0]
    L = CHUNK
    nseq = L // seq_rows
    t = pl.program_id(1)
    j = pl.program_id(2)

    if carry:
        assert nseq == 1

        @pl.when(t == 0)
        def _():
            for u in range(pair):
                _prompt_init(ca0_ref.at[0, u], cb0_ref.at[0, u], c0_ref.at[0, u], n0_ref.at[u], m0_ref.at[u],
                             xa_s.at[u], xb_s.at[u], c_s.at[u], n_s.at[u], m_s.at[u])

        for u in range(pair):
            _prompt_chunk(zm_refs[u], gates_refs[u],
                          caw_ref, cbw_ref, cbb_ref, lng_ref, lnb_ref, bif_ref, mng_ref,
                          p_ref.at[u], sa_ref.at[u], sb_ref.at[u], c_out_ref.at[0, u], n_out_ref.at[u],
                          m_out_ref.at[u], xa_s.at[u], xb_s.at[u], xbr_s.at[u], aconv_s.at[u],
                          bconv_s.at[u], c_s.at[u], n_s.at[u], m_s.at[u])
        return

    def _when(cond):
        return (lambda f: f()) if nseq == 1 else pl.when(cond)

    @_when(j == 0)
    def _pre():
        a_b = zm_ref[:, OFF_AB:OFF_AB + D_A].astype(F32)
        ca = zm_ref[:, OFF_AC:OFF_AC + D_A].astype(F32) * zm_ref[:, OFF_AX:OFF_AX + D_A].astype(F32)
        cb = zm_ref[:, OFF_BV:OFF_BV + D_B].astype(F32) * jax.nn.sigmoid(
            zm_ref[:, OFF_BG:OFF_BG + D_B].astype(F32))

        r_i = lax.broadcasted_iota(jnp.int32, (L, L), 0)
        c_i = lax.broadcasted_iota(jnp.int32, (L, L), 1)
        to_tok = (c_i == (r_i % nseq) * seq_rows + r_i // nseq).astype(F32)
        to_seq = (c_i == (r_i % seq_rows) * nseq + r_i // seq_rows).astype(F32)

        def short_conv(x, st_ref, w_ref, taps, new_ref):
            x_tok = _dot_f32(to_tok, x)
            window = [st_ref[i] for i in range(taps - 1)]
            window += [x_tok[tk * nseq:(tk + 1) * nseq] for tk in range(seq_rows)]
            outs_tok = []
            for tk in range(seq_rows):
                acc = w_ref[0:1, :] * window[tk]
                for k in range(1, taps):
                    acc = acc + w_ref[k:k + 1, :] * window[tk + k]
                outs_tok.append(acc)
            for i in range(taps - 1):
                new_ref[i] = window[seq_rows + i]
            return _dot_f32(to_seq, jnp.concatenate(outs_tok, axis=0))

        aconv_s[...] = short_conv(ca, ca0_ref, caw_ref, K_A, sa_ref)
        bconv_s[...] = short_conv(cb, cb0_ref, cbw_ref, K_B, sb_ref)

        p_ref[:, 0:D_A] = (a_b * aconv_s[...]).astype(BF16)
        bc = bconv_s[...] + cbb_ref[...]
        mu = jnp.mean(bc, axis=-1, keepdims=True)
        xc = bc - mu
        ln = xc * lax.rsqrt(jnp.mean(xc * xc, axis=-1, keepdims=True) + EPS) * lng_ref[...] + lnb_ref[...]
        p_ref[:, D_A:D_A + D_B] = (ln * jax.nn.sigmoid(ln)).astype(BF16)

        row = lax.broadcasted_iota(jnp.int32, (L, L), 0)
        col = lax.broadcasted_iota(jnp.int32, (L, L), 1)
        same = (row // seq_rows) == (col // seq_rows)
        causal = same & (col <= row)
        g = gates_ref[...] + bif_ref[...]
        logf = jnp.minimum(g, 0.0) - jnp.log1p(jnp.exp(-jnp.abs(g)))
        bt = _dot_f32(causal.astype(F32), logf)
        bt_h = pltpu.roll(bt, 128 - H_C, axis=1)
        if nseq == 1:
            btl_h = bt_h[L - 1:L, :]
        else:
            btl_h = pltpu.roll(_dot_f32(same.astype(F32), logf), 128 - H_C, axis=1)
        g_t = g.T
        bt_t = bt.T
        m_rows = m_s[0:1, :] if carry else m0_ref[0]
        inter = bt_h + m_rows
        lane = lax.broadcasted_iota(jnp.int32, (L, W_GATE), 1)

        dlogs = []
        rmax = jnp.zeros((L, W_GATE), F32)
        for h in range(H_C):
            dlog = jnp.where(causal, bt[:, H_C + h:H_C + h + 1] - bt_t[H_C + h:H_C + h + 1, :]
                             + g_t[h:h + 1, :], -jnp.inf)
            dlogs.append(dlog)
            rmax = jnp.where(lane == h, jnp.max(dlog, axis=-1, keepdims=True), rmax)
        m_t = jnp.maximum(inter, rmax)
        w_inter = jnp.exp(inter - m_t)
        floor = jnp.exp(-m_t)
        if nseq == 1:
            m_new = m_t[L - 1:L, :]
        else:
            last = (col == (row // seq_rows) * seq_rows + (seq_rows - 1)).astype(F32)
            m_new = _dot_f32(last, m_t)
        w_last = jnp.exp(btl_h - bt_h + g - m_new)
        w_prev = jnp.exp(inter - m_new)
        n_rows = n_s[0:1, :] if carry else n0_ref[0]

        den = jnp.zeros((L, W_GATE), F32)
        for h in range(H_C):
            hs = slice(h * DH_C, (h + 1) * DH_C)
            q = zm_ref[:, OFF_Q + h * DH_C:OFF_Q + (h + 1) * DH_C]
            k = zm_ref[:, OFF_K + h * DH_C:OFF_K + (h + 1) * DH_C]
            v = zm_ref[:, OFF_V + h * DH_C:OFF_V + (h + 1) * DH_C]
            s = _dot_nt(q, k) * (DH_C ** -0.5) * jnp.exp(dlogs[h] - m_t[:, h:h + 1])
            num_s[:, hs] = _dot(s.astype(BF16), v)
            qn_all = _dot_nt(q, jnp.broadcast_to(n_rows[:, hs], (L, DH_C)).astype(BF16))
            if nseq == 1:
                qn = qn_all[:, 0:1]
            else:
                qn = jnp.sum(jnp.where(row == col, qn_all, 0.0), axis=-1, keepdims=True)
            den_h = jnp.sum(s, axis=-1, keepdims=True) + w_inter[:, h:h + 1] * qn
            den = jnp.where(lane == h, den_h, den)
            vw_s[h] = (v.astype(F32) * w_last[:, h:h + 1]).astype(BF16)
        if nseq > 1:
            qc_s[...] = jnp.zeros(qc_s.shape, F32)
        winter_s[...] = w_inter
        rinv_s[...] = 1.0 / jnp.maximum(jnp.abs(den), floor)
        wlt_s[...] = w_last.T[0:16, :]
        wprev_s[...] = jnp.broadcast_to(w_prev, (L, W_GATE))
        mnew_s[...] = jnp.broadcast_to(m_new, (L, W_GATE))

    spp = c0_ref.shape[1]
    tile = spp * seq_rows if nseq > 1 else L
    for u in range(spp):
        if nseq == 1:
            wprev_row = wprev_s[L - 1:L, :]
            m_out_row = mnew_s[L - 1:L, :]
        else:
            sq = j * spp + u
            last_row = sq * seq_rows + (seq_rows - 1)
            lane_l = lax.broadcasted_iota(jnp.int32, (1, L), 1)
            row_l = lax.broadcasted_iota(jnp.int32, (L, 1), 0)
            seq_lanes = (lane_l // seq_rows) == sq
            seq_rows_mask = (row_l // seq_rows) == sq
            wprev_row = wprev_s[pl.ds(last_row, 1), :]
            m_out_row = mnew_s[pl.ds(last_row, 1), :]
        for h in range(H_C):
            hs = slice(h * DH_C, (h + 1) * DH_C)
            c_old = c_s[h] if carry else c0_ref[0, u, h]
            c_bf = c_old.astype(BF16)
            if nseq == 1:
                qc_s[:, hs] = _dot_nt(zm_ref[:, OFF_Q + h * DH_C:OFF_Q + (h + 1) * DH_C], c_bf)
            else:
                r0 = pl.multiple_of(j * tile, tile)
                q16 = zm_ref[pl.ds(r0, tile), OFF_Q + h * DH_C:OFF_Q + (h + 1) * DH_C]
                r = _dot_nt(q16, c_bf)
                rid = lax.broadcasted_iota(jnp.int32, (tile, 1), 0)
                qc_s[pl.ds(r0, tile), hs] = jnp.where((rid // seq_rows) == u, r, qc_s[pl.ds(r0, tile), hs])
            k = zm_ref[:, OFF_K + h * DH_C:OFF_K + (h + 1) * DH_C]
            if nseq == 1:
                vw = vw_s[h]
                wl = wlt_s[...]
            else:
                vw = jnp.where(seq_rows_mask, vw_s[h], jnp.zeros((L, DH_C), BF16))
                wl = jnp.where(seq_lanes, wlt_s[...], 0.0)
            kv = _dot_tn(vw, k)
            w_prev = wprev_row[:, h:h + 1]
            c_new = w_prev * c_old + kv * (DH_C ** -0.5)
            ksum = _dot(wl.astype(BF16), k)[h:h + 1, :]
            if carry:
                n_old = n_s[0:1, hs]
            else:
                n_old = n0_ref[0, pl.ds(sq * seq_rows, 1), hs]
            n_new = w_prev * n_old + ksum * (DH_C ** -0.5)
            if carry:
                c_s[h] = c_new
                n_s[0:1, hs] = n_new
            c_out_ref[0, u, h] = c_new
            n_out_ref[u, h:h + 1, :] = n_new
        if carry:
            m_s[0:1, :] = m_out_row
        m_out_ref[u] = m_out_row

    @_when(j == nseq // spp - 1)
    def _post():
        for h in range(H_C):
            hs = slice(h * DH_C, (h + 1) * DH_C)
            hh = (num_s[:, hs] + qc_s[:, hs] * winter_s[:, h:h + 1]) * rinv_s[:, h:h + 1]
            hn = hh * lax.rsqrt(jnp.mean(hh * hh, axis=-1, keepdims=True) + EPS) * mng_ref[:, hs]
            o = zm_ref[:, OFF_O + h * DH_C:OFF_O + (h + 1) * DH_C].astype(F32)
            p_ref[:, 2 * D_A + h * DH_C:2 * D_A + (h + 1) * DH_C] = (jax.nn.sigmoid(o) * hn).astype(BF16)


def _mixer(zm, gates, ca0, cb0, c0, n0, m0, caw, cbw, cbb, lng, lnb, bif, mng, c_acc, *,
           row_block0, n_groups, n_chunks, seq_rows, carry, layer, layer_in, layer_out):
    L = CHUNK
    nseq = L // seq_rows
    n_state = n_groups if carry else n_chunks * nseq
    rows = n_groups * n_chunks * L

    if carry:
        pair = PROMPT_PAIR
        spp = pair

        def seq_of(b, t, j):
            return b

        def conv_of(b, t, j):
            return b
        rows_blk = 1

        def tok(u):
            return lambda b, t, j: (row_block0 + (b * pair + u) * n_chunks + t, 0)

        def conv_in(taps, ch):
            return pl.BlockSpec((1, pair, taps - 1, ch), lambda *g: (layer_in, conv_of(*g), 0, 0))

        def conv_out(taps, ch):
            return (pl.BlockSpec((pair, taps - 1, ch), lambda *g: (conv_of(*g), 0, 0)),
                    jax.ShapeDtypeStruct((n_state, taps - 1, ch), F32))
        p_spec = pl.BlockSpec((pair, L, W_P), lambda b, t, j: (b, t, 0))
        p_shape = jax.ShapeDtypeStruct((n_groups, n_chunks * L, W_P), BF16)
        grid = (n_groups // pair, n_chunks, 1)
        lead = (pair,)
    else:
        pair = 1
        spp = 16 // seq_rows

        def tok(u):
            return lambda b, t, j: (row_block0 + b * n_chunks + t, 0)
        p_spec = pl.BlockSpec((L, W_P), lambda b, t, j: (b * n_chunks + t, 0))
        p_shape = jax.ShapeDtypeStruct((rows, W_P), BF16)
        grid = (n_groups, n_chunks, nseq // spp)
        lead = ()

        def seq_of(b, t, j):
            return t * (nseq // spp) + j

        def conv_of(b, t, j):
            return t
        rows_blk = L

        def conv_in(taps, ch):
            return pl.BlockSpec((None, taps - 1, nseq, ch), lambda *g: (layer_in, 0, conv_of(*g), 0))

        def conv_out(taps, ch):
            return (pl.BlockSpec((taps - 1, nseq, ch), lambda *g: (0, conv_of(*g), 0)),
                    jax.ShapeDtypeStruct((taps - 1, n_state, ch), F32))

    has_acc = c_acc is not None
    kern = functools.partial(_mixer_kernel, seq_rows=seq_rows, carry=carry, has_acc=has_acc)
    in_specs = [pl.BlockSpec((L, W_MAIN), tok(u)) for u in range(pair)]
    in_specs += [pl.BlockSpec((L, W_GATE), tok(u)) for u in range(pair)]
    in_specs += [
        conv_in(K_A, D_A),
        conv_in(K_B, D_B),
        pl.BlockSpec((1, spp, H_C, DH_C, DH_C), lambda *g: (layer_in, seq_of(*g), 0, 0, 0)),
        pl.BlockSpec((None, pair, rows_blk, D_MODEL), lambda *g: (layer_in, conv_of(*g), 0, 0)),
        pl.BlockSpec((None, pair, rows_blk, W_GATE), lambda *g: (layer_in, conv_of(*g), 0, 0)),
        _resident((K_A, D_A), layer),
        _resident((K_B, D_B), layer),
        _resident((1, D_B), layer),
        _resident((1, D_B), layer),
        _resident((1, D_B), layer),
        _resident((1, W_GATE), layer),
        _resident((1, D_MODEL), layer),
    ]
    args = [zm] * pair + [gates] * pair + [ca0, cb0, c0, n0, m0, caw, cbw, cbb, lng, lnb, bif, mng]
    aliases = {}
    if has_acc:
        in_specs.append(pl.BlockSpec(memory_space=pl.ANY))
        args.append(c_acc)
        aliases = {len(args) - 1: 3}
    outs = pl.pallas_call(
        kern,
        grid=grid,
        in_specs=in_specs,
        out_specs=[
            p_spec,
            conv_out(K_A, D_A)[0],
            conv_out(K_B, D_B)[0],
            pl.BlockSpec((1, spp, H_C, DH_C, DH_C), lambda *g: (layer_out, seq_of(*g), 0, 0, 0)),
            pl.BlockSpec((spp, H_C, DH_C), lambda *g: (seq_of(*g), 0, 0)),
            pl.BlockSpec((spp, 1, W_GATE), lambda *g: (seq_of(*g), 0, 0)),
        ],
        out_shape=[
            p_shape,
            conv_out(K_A, D_A)[1],
            conv_out(K_B, D_B)[1],
            jax.ShapeDtypeStruct((DEPTH, n_state, H_C, DH_C, DH_C), F32),
            jax.ShapeDtypeStruct((n_state, H_C, DH_C), F32),
            jax.ShapeDtypeStruct((n_state, 1, W_GATE), F32),
        ],
        input_output_aliases=aliases,
        scratch_shapes=[
            pltpu.VMEM(lead + (8 + L, D_A), F32),
            pltpu.VMEM(lead + (32 + L, D_B), F32),
            pltpu.VMEM(lead + (8, 24 + L, D_B) if carry else (8, 8, 128), F32),
            pltpu.VMEM(lead + (L, D_A), F32),
            pltpu.VMEM(lead + (L, D_B), F32),
            pltpu.VMEM(lead + (H_C, DH_C, DH_C), F32),
            pltpu.VMEM(lead + (8, D_MODEL), F32),
            pltpu.VMEM(lead + (8, W_GATE), F32),
            pltpu.VMEM((L, D_MODEL), F32),
            pltpu.VMEM((L, D_MODEL), F32),
            pltpu.VMEM((H_C, L, DH_C), BF16),
            pltpu.VMEM((L, W_GATE), F32),
            pltpu.VMEM((L, W_GATE), F32),
            pltpu.VMEM((16, L), F32),
            pltpu.VMEM((L, W_GATE), F32),
            pltpu.VMEM((L, W_GATE), F32),
        ],
        compiler_params=_params(("arbitrary", "arbitrary", "arbitrary")),
        name="mixer_prompt" if carry else "mixer_sample",
    )(*args)
    outs = list(outs)
    outs[0] = outs[0].reshape(rows, W_P)
    return outs


def _outproj_kernel(x_ref, pp_ref, ps_ref, zg_ref, wa_ref, wb_ref, wc_ref, wo_ref, gx_ref, wq_ref,
                    x1_ref, q_ref, *, n_prompt_tiles):
    i = pl.program_id(0)
    p = jnp.where(i >= n_prompt_tiles, ps_ref[...], pp_ref[...])
    y_a = _dot(p[:, 0:D_A], wa_ref[...])
    y_b = _dot(p[:, D_A:D_A + D_B], wb_ref[...])
    y_c = _dot(p[:, D_A + D_B:], wc_ref[...])
    u = (jax.nn.sigmoid(zg_ref[:, 0:D_MODEL].astype(F32)) * y_a
         + jax.nn.sigmoid(zg_ref[:, D_MODEL:2 * D_MODEL].astype(F32)) * y_b
         + jax.nn.sigmoid(zg_ref[:, 2 * D_MODEL:].astype(F32)) * y_c)
    x1 = x_ref[...] + _dot(u.astype(BF16), wo_ref[...])
    x1_ref[...] = x1
    q_ref[...] = _dot(_rmsnorm(x1, gx_ref[...]).astype(BF16), wq_ref[...]).astype(BF16)


def _outproj(x, p_p, p_s, zg, wa, wb, wc, wo, gx, wq, layer):
    n = x.shape[0]
    tm = ROW_TILE
    npt = p_p.shape[0] // tm
    return pl.pallas_call(
        functools.partial(_outproj_kernel, n_prompt_tiles=npt),
        grid=(n // tm,),
        in_specs=[
            pl.BlockSpec((tm, D_MODEL), lambda i: (i, 0)),
            pl.BlockSpec((tm, W_P), lambda i: (jnp.minimum(i, npt - 1), 0)),
            pl.BlockSpec((tm, W_P), lambda i: (0, 0)),
            pl.BlockSpec((tm, W_MERGE), lambda i: (i, 0)),
            _resident((D_A, D_MODEL), layer),
            _resident((D_B, D_MODEL), layer),
            _resident((D_MODEL, D_MODEL), layer),
            _resident((D_MODEL, D_MODEL), layer),
            _resident((1, D_MODEL), layer),
            _resident((D_MODEL, D_MODEL), layer),
        ],
        out_specs=[
            pl.BlockSpec((tm, D_MODEL), lambda i: (i, 0)),
            pl.BlockSpec((tm, D_MODEL), lambda i: (i, 0)),
        ],
        out_shape=[
            jax.ShapeDtypeStruct((n, D_MODEL), F32),
            jax.ShapeDtypeStruct((n, D_MODEL), BF16),
        ],
        compiler_params=_params(("parallel",)),
        name="outproj",
    )(x, p_p, p_s, zg, wa, wb, wc, wo, gx, wq)


def _memkv_kernel(mem_ref, g_ref, w_ref, k_ref, v_ref, kb_ref, vb_ref):
    h = _rmsnorm(mem_ref[...], g_ref[0]).astype(BF16)
    hd = HX * DX
    k = _dot(h, w_ref[0, :, 0:hd])
    v = _dot(h, w_ref[0, :, hd:])
    for bb in range(k_ref.shape[1]):
        for hh in range(HX):
            k_ref[0, bb, :, hh, :] = k[bb * N_MEM:(bb + 1) * N_MEM, hh * DX:(hh + 1) * DX]
            v_ref[0, bb, :, hh, :] = v[bb * N_MEM:(bb + 1) * N_MEM, hh * DX:(hh + 1) * DX]
    kb_ref[0] = k.astype(BF16)
    vb_ref[0] = v.astype(BF16)


def _memkv(mem, g, w):
    n = mem.shape[0]
    tm = ROW_TILE
    hd = HX * DX
    per = tm // N_MEM
    o_spec = pl.BlockSpec((1, tm, hd), lambda l, i: (l, i, 0))
    o5_spec = pl.BlockSpec((1, per, N_MEM, HX, DX), lambda l, i: (l, i, 0, 0, 0))
    return pl.pallas_call(
        _memkv_kernel,
        grid=(DEPTH, n // tm),
        in_specs=[
            pl.BlockSpec((tm, D_MODEL), lambda l, i: (i, 0)),
            pl.BlockSpec((1, 1, D_MODEL), lambda l, i: (l, 0, 0)),
            pl.BlockSpec((1, D_MODEL, 2 * hd), lambda l, i: (l, 0, 0)),
        ],
        out_specs=[o5_spec, o5_spec, o_spec, o_spec],
        out_shape=[
            jax.ShapeDtypeStruct((DEPTH, n // N_MEM, N_MEM, HX, DX), F32),
            jax.ShapeDtypeStruct((DEPTH, n // N_MEM, N_MEM, HX, DX), F32),
            jax.ShapeDtypeStruct((DEPTH, n, hd), BF16),
            jax.ShapeDtypeStruct((DEPTH, n, hd), BF16),
        ],
        compiler_params=_params(("arbitrary", "arbitrary")),
        name="memkv",
    )(mem, g, w)


def _attend(q, k, v):
    s = _dot_nt(q, k) * (DX ** -0.5)
    e = jnp.exp(s - jnp.max(s, axis=-1, keepdims=True))
    return _dot(e.astype(BF16), v) * (1.0 / jnp.sum(e, axis=-1, keepdims=True))


XS_ROWS = 16


def _xattn_kernel(q_ref, k_ref, v_ref, qs_ref, ks_ref, vs_ref, o_ref, os_ref, *, seq_rows):
    for h in range(HX):
        hs = slice(h * DX, (h + 1) * DX)
        o_ref[:, hs] = _attend(q_ref[:, hs], k_ref[0, :, hs], v_ref[0, :, hs]).astype(BF16)
    _xattn_short(qs_ref, ks_ref, vs_ref, os_ref, seq_rows)


def _xattn(q, kb, vb, k_cache, v_cache, layer, *, n_groups, rows_per_group, seq_rows):
    tq = ROW_TILE
    nt = rows_per_group // tq
    hd = HX * DX
    per = XS_ROWS // seq_rows
    n_seq = k_cache.shape[1]
    assert n_groups * nt * per == n_seq
    qs_block0 = n_groups * rows_per_group // XS_ROWS
    kv_spec = pl.BlockSpec((1, N_MEM, hd), lambda b, t: (layer, b, 0))
    cache_spec = pl.BlockSpec((1, per, N_MEM, HX, DX), lambda b, t: (layer, b * nt + t, 0, 0, 0))
    return pl.pallas_call(
        functools.partial(_xattn_kernel, seq_rows=seq_rows),
        grid=(n_groups, nt),
        in_specs=[pl.BlockSpec((tq, hd), lambda b, t: (b * nt + t, 0)), kv_spec, kv_spec,
                  pl.BlockSpec((XS_ROWS, hd), lambda b, t: (qs_block0 + b * nt + t, 0)),
                  cache_spec, cache_spec],
        out_specs=[pl.BlockSpec((tq, hd), lambda b, t: (b * nt + t, 0)),
                   pl.BlockSpec((XS_ROWS, hd), lambda b, t: (b * nt + t, 0))],
        out_shape=[jax.ShapeDtypeStruct((n_groups * rows_per_group, hd), BF16),
                   jax.ShapeDtypeStruct((n_seq * seq_rows, hd), BF16)],
        compiler_params=_params(("parallel", "parallel")),
        name="xattn",
    )(q, kb, vb, q, k_cache, v_cache)


def _xattn_short(q_ref, k_ref, v_ref, o_ref, seq_rows):
    rows = q_ref.shape[0]
    qs = jnp.concatenate([q_ref[:, h * DX:(h + 1) * DX] for h in range(HX)], axis=0)
    rid = lax.broadcasted_iota(jnp.int32, (HX * rows, 1), 0)
    cid = lax.broadcasted_iota(jnp.int32, (HX * rows, N_MEM * HX), 1)
    own_head = (cid % HX) == (rid // rows)
    acc = jnp.zeros((HX * rows, DX), F32)
    for e in range(rows // seq_rows):
        k2 = k_ref[0, e].reshape(N_MEM * HX, DX).astype(BF16)
        v2 = v_ref[0, e].reshape(N_MEM * HX, DX).astype(BF16)
        s = jnp.where(own_head, _dot_nt(qs, k2) * (DX ** -0.5), -jnp.inf)
        p = jnp.exp(s - jnp.max(s, axis=-1, keepdims=True))
        o = _dot(p.astype(BF16), v2) * (1.0 / jnp.sum(p, axis=-1, keepdims=True))
        acc = jnp.where(((rid % rows) // seq_rows) == e, o, acc)
    for h in range(HX):
        o_ref[:, h * DX:(h + 1) * DX] = acc[h * rows:(h + 1) * rows].astype(BF16)


FF_BLOCK = 256


def _ffn_kernel(x_ref, cp_ref, cs_ref, wxo_ref, gf_ref, wg_ref, wu_ref, wout_ref, gfin_ref,
                *rest, n_prompt_tiles, final):
    act_s = rest[-1]
    i = pl.program_id(0)
    ctx = jnp.where(i >= n_prompt_tiles, cs_ref[...], cp_ref[...])
    x2 = x_ref[...] + _dot(ctx, wxo_ref[...])
    h = _rmsnorm(x2, gf_ref[...]).astype(BF16)
    for jb in range(D_FF // FF_BLOCK):
        sl = slice(jb * FF_BLOCK, (jb + 1) * FF_BLOCK)
        gate = _dot(h, wg_ref[:, sl])
        up = _dot(h, wu_ref[:, sl])
        act_s[:, sl] = (gate * jax.nn.sigmoid(gate) * up).astype(BF16)
    x3 = x2 + _dot(act_s[...], wout_ref[...])
    if final:
        y = _rmsnorm(x3, gfin_ref[...])
        yp_ref, ys_ref = rest[0], rest[1]

        @pl.when(i < n_prompt_tiles)
        def _():
            yp_ref[...] = y

        @pl.when(i >= n_prompt_tiles)
        def _():
            ys_ref[...] = y
    else:
        rest[0][...] = x3


def _ffn(x, c_p, c_s, wxo, gf, w_in, wout, gfin, layer, *, final):
    n = x.shape[0]
    tm = ROW_TILE
    npt = c_p.shape[0] // tm
    if final:
        assert n - npt * tm == tm
        out_specs = [pl.BlockSpec((tm, D_MODEL), lambda i: (jnp.minimum(i, npt - 1), 0)),
                     pl.BlockSpec((tm, D_MODEL), lambda i: (0, 0))]
        out_shape = [jax.ShapeDtypeStruct((npt * tm, D_MODEL), F32),
                     jax.ShapeDtypeStruct((tm, D_MODEL), F32)]
        sem = ("arbitrary",)
    else:
        out_specs = pl.BlockSpec((tm, D_MODEL), lambda i: (i, 0))
        out_shape = jax.ShapeDtypeStruct((n, D_MODEL), F32)
        sem = ("parallel",)
    return pl.pallas_call(
        functools.partial(_ffn_kernel, n_prompt_tiles=npt, final=final),
        grid=(n // tm,),
        in_specs=[
            pl.BlockSpec((tm, D_MODEL), lambda i: (i, 0)),
            pl.BlockSpec((tm, D_MODEL), lambda i: (jnp.minimum(i, npt - 1), 0)),
            pl.BlockSpec((tm, D_MODEL), lambda i: (0, 0)),
            _resident((D_MODEL, D_MODEL), layer),
            _resident((1, D_MODEL), layer),
            _resident((D_MODEL, D_FF), layer, col=0),
            _resident((D_MODEL, D_FF), layer, col=1),
            _resident((D_FF, D_MODEL), layer),
            _resident((1, D_MODEL)),
        ],
        out_specs=out_specs,
        out_shape=out_shape,
        scratch_shapes=[pltpu.VMEM((tm, D_FF), BF16)],
        compiler_params=_params(sem),
        name="ffn",
    )(x, c_p, c_s, wxo, gf, w_in, w_in, wout, gfin)


def kernel(x_prompt, x_sample, state_conv_a, state_conv_b, state_mlstm_c, state_mlstm_n, state_mlstm_m,
           cache_mem_k, cache_mem_v, mem_prompt, norm_mix_g, w_in, b_if, conv_a_w, w_out_a, conv_b_w,
           conv_b_b, ln_b_g, ln_b_b, w_out_b, mlstm_norm_g, w_out_c, w_o, norm_x_g, norm_mem_g, w_xq,
           w_xkv, w_xo, norm_ffn_g, w_ffn_in, w_ffn_out, final_norm_g):
    bp, tp, d = x_prompt.shape
    bs, ts, _ = x_sample.shape
    n_p = bp * tp
    n_s = bs * ts
    hd = HX * DX

    gate_lo = W_MAIN
    gate_hi = W_MAIN + 2 * H_C
    w_inb = w_in.astype(BF16)
    w_gate = jnp.pad(w_inb[:, :, gate_lo:gate_hi], ((0, 0), (0, 0), (0, W_GATE - 2 * H_C)))
    w_merge = w_inb[:, :, gate_hi:]
    bif = jnp.pad(b_if, ((0, 0), (0, W_GATE - 2 * H_C)))[:, None, :]
    w_a = w_out_a.astype(BF16)
    w_b = w_out_b.astype(BF16)
    w_c = w_out_c.astype(BF16)
    w_ob = w_o.astype(BF16)
    w_q = w_xq.astype(BF16)
    w_kv = w_xkv.astype(BF16)
    w_xob = w_xo.astype(BF16)
    w_fi = w_ffn_in.astype(BF16)
    w_fo = w_ffn_out.astype(BF16)
    g_mix = norm_mix_g[:, None, :]
    g_x = norm_x_g[:, None, :]
    g_ffn = norm_ffn_g[:, None, :]
    small = (conv_a_w, conv_b_w, conv_b_b[:, None, :], ln_b_g[:, None, :], ln_b_b[:, None, :],
             bif, mlstm_norm_g[:, None, :])

    x = jnp.concatenate([x_prompt.reshape(n_p, d), x_sample.reshape(n_s, d)], axis=0)

    mem_k, mem_v, mem_kb, mem_vb = _memkv(mem_prompt.reshape(bp * N_MEM, d), norm_mem_g[:, None, :], w_kv)

    zeros_ca = jnp.zeros((1, bp, K_A - 1, D_A), F32)
    zeros_cb = jnp.zeros((1, bp, K_B - 1, D_B), F32)
    zeros_c = jnp.zeros((1, bp, H_C, DH_C, DH_C), F32)
    zeros_n = jnp.zeros((1, bp, 1, H_C * DH_C), F32)
    zeros_m = jnp.zeros((1, bp, 1, W_GATE), F32)

    n_chunks_s = n_s // CHUNK
    n0_rows = jnp.repeat(state_mlstm_n.reshape(DEPTH, bs, H_C * DH_C), ts, axis=1)
    n0_rows = n0_rows.reshape(DEPTH, n_chunks_s, CHUNK, H_C * DH_C)
    m0_rows = jnp.repeat(jnp.pad(state_mlstm_m, ((0, 0), (0, 0), (0, W_GATE - H_C))), ts, axis=1)
    m0_rows = m0_rows.reshape(DEPTH, n_chunks_s, CHUNK, W_GATE)

    conv_a_tm = jnp.transpose(state_conv_a, (0, 2, 1, 3))
    conv_b_tm = jnp.transpose(state_conv_b, (0, 2, 1, 3))

    outs = {k: [] for k in ("pa", "pb", "pn", "pm", "sa", "sb", "sn", "sm")}
    c1 = c2 = None
    for l in range(DEPTH):
        zm, zg, gates = _inproj(x, g_mix, w_inb, w_merge, w_gate, l)
        p_p, a1, b1, c1, n1, m1 = _mixer(
            zm, gates, zeros_ca, zeros_cb, zeros_c, zeros_n, zeros_m, *small, c1,
            row_block0=0, n_groups=bp, n_chunks=tp // CHUNK, seq_rows=CHUNK, carry=True,
            layer=l, layer_in=0, layer_out=l)
        p_s, a2, b2, c2, n2, m2 = _mixer(
            zm, gates, conv_a_tm, conv_b_tm, state_mlstm_c, n0_rows, m0_rows, *small, c2,
            row_block0=n_p // CHUNK, n_groups=1, n_chunks=n_chunks_s, seq_rows=ts, carry=False,
            layer=l, layer_in=l, layer_out=l)
        x1, qx = _outproj(x, p_p, p_s, zg, w_a, w_b, w_c, w_ob, g_x, w_q, l)
        c_p, c_s = _xattn(qx, mem_kb, mem_vb, cache_mem_k, cache_mem_v, l,
                          n_groups=bp, rows_per_group=tp, seq_rows=ts)
        x = _ffn(x1, c_p, c_s, w_xob, g_ffn, w_fi, w_fo, final_norm_g[None, :], l, final=(l == DEPTH - 1))
        outs["pa"].append(a1); outs["pb"].append(b1); outs["pn"].append(n1); outs["pm"].append(m1)
        outs["sa"].append(a2); outs["sb"].append(b2); outs["sn"].append(n2); outs["sm"].append(m2)

    y_prompt = x[0].reshape(bp, tp, d)
    y_sample = x[1].reshape(bs, ts, d)
    st = {k: jnp.stack(v) for k, v in outs.items()}
    return (y_prompt, y_sample,
            st["pa"], st["pb"], c1, st["pn"], st["pm"][:, :, 0, :H_C],
            mem_k, mem_v,
            jnp.transpose(st["sa"], (0, 2, 1, 3)), jnp.transpose(st["sb"], (0, 2, 1, 3)), c2,
            st["sn"], st["sm"][:, :, 0, :H_C])
```

```python
import functools

import jax
import jax.numpy as jnp
from jax import lax
from jax.experimental import pallas as pl
from jax.experimental.pallas import tpu as pltpu

D_MODEL = 1024
DEPTH = 4
D_A = 512
K_A = 3
D_B = 512
K_B = 31
H_C = 4
DH_C = 256
N_MEM = 256
HX = 4
DX = 256
D_FF = 2816
EPS = 1e-6

OFF_AB, OFF_AC, OFF_AX, OFF_BV, OFF_BG = 0, 512, 1024, 1536, 2048
OFF_Q, OFF_K, OFF_V, OFF_O = 2560, 3584, 4608, 5632
W_MAIN = 6656
W_MERGE = 3 * D_MODEL
W_GATE = 128
W_P = 2 * D_A + D_MODEL

CHUNK = 128
PROMPT_PAIR = 1
PROMPT_CHUNKS_PER_STEP = 4
ROW_TILE = 512
VMEM_LIMIT_BYTES = 56 * 1024 * 1024

F32 = jnp.float32
BF16 = jnp.bfloat16


def _dot(a, b):
    return jnp.dot(a, b, preferred_element_type=F32)


def _dot_nt(a, b):
    return lax.dot_general(a, b, (((1,), (1,)), ((), ())), preferred_element_type=F32)


def _dot_tn(a, b):
    return lax.dot_general(a, b, (((0,), (0,)), ((), ())), preferred_element_type=F32)


def _dot_f32(a, b):
    return jnp.dot(a, b, preferred_element_type=F32, precision=lax.Precision.HIGHEST)


def _rmsnorm(x, g):
    ms = jnp.mean(x * x, axis=-1, keepdims=True)
    return x * lax.rsqrt(ms + EPS) * g


def _resident(shape, layer=None, col=0):
    nd = len(shape)
    if layer is None:
        return pl.BlockSpec(shape, lambda *_: (0,) * nd, pipeline_mode=pl.Buffered(1))
    return pl.BlockSpec((None,) + shape, lambda *_: (layer,) + (0,) * (nd - 1) + (col,),
                        pipeline_mode=pl.Buffered(1))


def _params(sem):
    return pltpu.CompilerParams(dimension_semantics=sem, vmem_limit_bytes=VMEM_LIMIT_BYTES)


def _inproj_kernel(x_ref, g_ref, wm_ref, wg_ref, wgate_ref, zm_ref, zg_ref, gates_ref):
    h = _rmsnorm(x_ref[...], g_ref[...]).astype(BF16)
    for j in range(W_MAIN // 512):
        sl = slice(j * 512, (j + 1) * 512)
        zm_ref[:, sl] = _dot(h, wm_ref[:, sl]).astype(BF16)
    for j in range(W_MERGE // 512):
        sl = slice(j * 512, (j + 1) * 512)
        zg_ref[:, sl] = _dot(h, wg_ref[:, sl]).astype(BF16)
    gates_ref[...] = _dot(h, wgate_ref[...])


def _inproj(x, g, wm, wg, wgate, layer):
    n = x.shape[0]
    tm = ROW_TILE
    return pl.pallas_call(
        _inproj_kernel,
        grid=(n // tm,),
        in_specs=[
            pl.BlockSpec((tm, D_MODEL), lambda i: (i, 0)),
            _resident((1, D_MODEL), layer),
            _resident((D_MODEL, W_MAIN), layer),
            _resident((D_MODEL, W_MERGE), layer),
            _resident((D_MODEL, W_GATE), layer),
        ],
        out_specs=[
            pl.BlockSpec((tm, W_MAIN), lambda i: (i, 0)),
            pl.BlockSpec((tm, W_MERGE), lambda i: (i, 0)),
            pl.BlockSpec((tm, W_GATE), lambda i: (i, 0)),
        ],
        out_shape=[
            jax.ShapeDtypeStruct((n, W_MAIN), BF16),
            jax.ShapeDtypeStruct((n, W_MERGE), BF16),
            jax.ShapeDtypeStruct((n, W_GATE), F32),
        ],
        compiler_params=_params(("parallel",)),
        name="inproj",
    )(x, g, wm, wg, wgate)


def _prompt_init(ca0_ref, cb0_ref, c0_ref, n0_ref, m0_ref, xa_s, xb_s, c_s, n_s, m_s):
    xa_s[0:8, :] = jnp.zeros((8, D_A), F32)
    xb_s[0:32, :] = jnp.zeros((32, D_B), F32)
    xa_s[6:8, :] = ca0_ref[...]
    xb_s[2:32, :] = cb0_ref[...]
    c_s[...] = c0_ref[...]
    n_s[0:1, :] = n0_ref[...]
    m_s[0:1, :] = m0_ref[...]


def _prompt_chunk(zm_ref, gates_ref, caw_ref, cbw_ref, cbb_ref, lng_ref, lnb_ref, bif_ref, mng_ref,
                  p_ref, sa_ref, sb_ref, c_out_ref, n_out_ref, m_out_ref,
                  xa_s, xb_s, xbr_s, aconv_s, bconv_s, c_s, n_s, m_s):
    L = CHUNK

    xa_s[8:8 + L, :] = (zm_ref[:, OFF_AC:OFF_AC + D_A].astype(F32)
                        * zm_ref[:, OFF_AX:OFF_AX + D_A].astype(F32))
    xb_s[32:32 + L, :] = zm_ref[:, OFF_BV:OFF_BV + D_B].astype(F32) * jax.nn.sigmoid(
        zm_ref[:, OFF_BG:OFF_BG + D_B].astype(F32))
    for r in range(1, 8):
        xbr_s[r] = xb_s[r:r + 24 + L, :]

    def conv_block(cblk):
        cs = slice(cblk * 128, (cblk + 1) * 128)
        acc = caw_ref[0:1, cs] * xa_s[6:6 + L, cs]
        for k in range(1, K_A):
            acc = acc + caw_ref[k:k + 1, cs] * xa_s[6 + k:6 + k + L, cs]
        aconv_s[:, cs] = acc
        acc = None
        for k in range(K_B):
            a8, r = (2 + k) // 8 * 8, (2 + k) % 8
            src = xb_s[a8:a8 + L, cs] if r == 0 else xbr_s[r, a8:a8 + L, cs]
            term = cbw_ref[k:k + 1, cs] * src
            acc = term if acc is None else acc + term
        bconv_s[:, cs] = acc

    row = lax.broadcasted_iota(jnp.int32, (L, L), 0)
    col = lax.broadcasted_iota(jnp.int32, (L, L), 1)
    causal = col <= row
    g = gates_ref[...] + bif_ref[...]
    logf = jnp.minimum(g, 0.0) - jnp.log1p(jnp.exp(-jnp.abs(g)))
    bt = _dot_f32(causal.astype(F32), logf)
    bt_h = pltpu.roll(bt, 128 - H_C, axis=1)
    g_t = g.T
    bt_t = bt.T
    inter = bt_h + m_s[0:1, :]
    lane = lax.broadcasted_iota(jnp.int32, (L, W_GATE), 1)
    dlogs = []
    rmax = jnp.zeros((L, W_GATE), F32)
    for h in range(H_C):
        dlog = jnp.where(causal, bt[:, H_C + h:H_C + h + 1] - bt_t[H_C + h:H_C + h + 1, :]
                         + g_t[h:h + 1, :], -jnp.inf)
        dlogs.append(dlog)
        rmax = jnp.where(lane == h, jnp.max(dlog, axis=-1, keepdims=True), rmax)
    m_t = jnp.maximum(inter, rmax)
    w_inter = jnp.exp(inter - m_t)
    floor = jnp.exp(-m_t)
    m_new = m_t[L - 1:L, :]
    w_last = jnp.exp(bt_h[L - 1:L, :] - bt_h + g - m_new)
    w_prev = jnp.exp(inter[L - 1:L, :] - m_new)
    wl16 = w_last.T[0:16, :].astype(BF16)
    n_row = n_s[0:1, :]

    def head(h):
        hs = slice(h * DH_C, (h + 1) * DH_C)
        q = zm_ref[:, OFF_Q + h * DH_C:OFF_Q + (h + 1) * DH_C]
        k = zm_ref[:, OFF_K + h * DH_C:OFF_K + (h + 1) * DH_C]
        v = zm_ref[:, OFF_V + h * DH_C:OFF_V + (h + 1) * DH_C]
        c_old = c_s[h]
        s = _dot_nt(q, k) * (DH_C ** -0.5) * jnp.exp(dlogs[h] - m_t[:, h:h + 1])
        num = _dot(s.astype(BF16), v)
        qc = _dot_nt(q, c_old.astype(BF16))
        qn = _dot_nt(q, jnp.broadcast_to(n_row[:, hs], (L, DH_C)).astype(BF16))[:, 0:1]
        wi = w_inter[:, h:h + 1]
        den = jnp.sum(s, axis=-1, keepdims=True) + wi * qn
        rinv = 1.0 / jnp.maximum(jnp.abs(den), floor[:, h:h + 1])
        hh = (num + qc * wi) * rinv
        hn = hh * lax.rsqrt(jnp.mean(hh * hh, axis=-1, keepdims=True) + EPS) * mng_ref[:, hs]
        o = zm_ref[:, OFF_O + h * DH_C:OFF_O + (h + 1) * DH_C].astype(F32)
        p_ref[:, 2 * D_A + h * DH_C:2 * D_A + (h + 1) * DH_C] = (jax.nn.sigmoid(o) * hn).astype(BF16)
        vw = (v.astype(F32) * w_last[:, h:h + 1]).astype(BF16)
        wp = w_prev[:, h:h + 1]
        c_new = wp * c_old + _dot_tn(vw, k) * (DH_C ** -0.5)
        n_new = wp * n_row[:, hs] + _dot(wl16, k)[h:h + 1, :] * (DH_C ** -0.5)
        c_s[h] = c_new
        c_out_ref[h] = c_new
        n_s[0:1, hs] = n_new
        n_out_ref[h:h + 1, :] = n_new

    conv_block(0)
    conv_block(1)
    head(0)
    conv_block(2)
    head(1)
    conv_block(3)
    head(2)

    sa_new = xa_s[6 + L:8 + L, :]
    sb_new = xb_s[2 + L:32 + L, :]
    xa_s[6:8, :] = sa_new
    xb_s[2:32, :] = sb_new
    sa_ref[...] = sa_new
    sb_ref[...] = sb_new
    p_ref[:, 0:D_A] = (zm_ref[:, OFF_AB:OFF_AB + D_A].astype(F32) * aconv_s[...]).astype(BF16)
    bc = bconv_s[...] + cbb_ref[...]
    mu = jnp.mean(bc, axis=-1, keepdims=True)
    xc = bc - mu
    ln = xc * lax.rsqrt(jnp.mean(xc * xc, axis=-1, keepdims=True) + EPS) * lng_ref[...] + lnb_ref[...]
    p_ref[:, D_A:D_A + D_B] = (ln * jax.nn.sigmoid(ln)).astype(BF16)

    head(3)
    m_s[0:1, :] = m_new
    m_out_ref[...] = m_new


def _mixer_kernel(*refs, seq_rows, carry, has_acc):
    pair = PROMPT_PAIR if carry else 1
    zm_refs, gates_refs, refs = refs[:pair], refs[pair:2 * pair], refs[2 * pair:]
    (ca0_ref, cb0_ref, c0_ref, n0_ref, m0_ref,
     caw_ref, cbw_ref, cbb_ref, lng_ref, lnb_ref, bif_ref, mng_ref) = refs[:12]
    refs = refs[13:] if has_acc else refs[12:]
    (p_ref, sa_ref, sb_ref, c_out_ref, n_out_ref, m_out_ref,
     xa_s, xb_s, xbr_s, aconv_s, bconv_s, c_s, n_s, m_s,
     num_s, qc_s, vw_s, winter_s, rinv_s, wlt_s, wprev_s, mnew_s) = refs
    zm_ref, gates_ref = zm_refs[0], gates_refs[0]
    L = CHUNK
    nseq = L // seq_rows
    t = pl.program_id(1)
    j = pl.program_id(2)

    if carry:
        assert nseq == 1

        @pl.when(t == 0)
        def _():
            for u in range(pair):
                _prompt_init(ca0_ref.at[0, u], cb0_ref.at[0, u], c0_ref.at[0, u], n0_ref.at[u], m0_ref.at[u],
                             xa_s.at[u], xb_s.at[u], c_s.at[u], n_s.at[u], m_s.at[u])

        def chunk(ci, _):
            rs = pl.ds(pl.multiple_of(ci * L, L), L)
            for u in range(pair):
                _prompt_chunk(zm_refs[u].at[rs], gates_refs[u].at[rs],
                              caw_ref, cbw_ref, cbb_ref, lng_ref, lnb_ref, bif_ref, mng_ref,
                              p_ref.at[u, rs], sa_ref.at[u], sb_ref.at[u], c_out_ref.at[0, u], n_out_ref.at[u],
                              m_out_ref.at[u], xa_s.at[u], xb_s.at[u], xbr_s.at[u], aconv_s.at[u],
                              bconv_s.at[u], c_s.at[u], n_s.at[u], m_s.at[u])
            return 0

        lax.fori_loop(0, zm_ref.shape[0] // L, chunk, 0)
        return

    def _when(cond):
        return (lambda f: f()) if nseq == 1 else pl.when(cond)

    @_when(j == 0)
    def _pre():
        a_b = zm_ref[:, OFF_AB:OFF_AB + D_A].astype(F32)
        ca = zm_ref[:, OFF_AC:OFF_AC + D_A].astype(F32) * zm_ref[:, OFF_AX:OFF_AX + D_A].astype(F32)
        cb = zm_ref[:, OFF_BV:OFF_BV + D_B].astype(F32) * jax.nn.sigmoid(
            zm_ref[:, OFF_BG:OFF_BG + D_B].astype(F32))

        r_i = lax.broadcasted_iota(jnp.int32, (L, L), 0)
        c_i = lax.broadcasted_iota(jnp.int32, (L, L), 1)
        to_tok = (c_i == (r_i % nseq) * seq_rows + r_i // nseq).astype(F32)
        to_seq = (c_i == (r_i % seq_rows) * nseq + r_i // seq_rows).astype(F32)

        def short_conv(x, st_ref, w_ref, taps, new_ref):
            x_tok = _dot_f32(to_tok, x)
            window = [st_ref[i] for i in range(taps - 1)]
            window += [x_tok[tk * nseq:(tk + 1) * nseq] for tk in range(seq_rows)]
            outs_tok = []
            for tk in range(seq_rows):
                acc = w_ref[0:1, :] * window[tk]
                for k in range(1, taps):
                    acc = acc + w_ref[k:k + 1, :] * window[tk + k]
                outs_tok.append(acc)
            for i in range(taps - 1):
                new_ref[i] = window[seq_rows + i]
            return _dot_f32(to_seq, jnp.concatenate(outs_tok, axis=0))

        aconv_s[...] = short_conv(ca, ca0_ref, caw_ref, K_A, sa_ref)
        bconv_s[...] = short_conv(cb, cb0_ref, cbw_ref, K_B, sb_ref)

        p_ref[:, 0:D_A] = (a_b * aconv_s[...]).astype(BF16)
        bc = bconv_s[...] + cbb_ref[...]
        mu = jnp.mean(bc, axis=-1, keepdims=True)
        xc = bc - mu
        ln = xc * lax.rsqrt(jnp.mean(xc * xc, axis=-1, keepdims=True) + EPS) * lng_ref[...] + lnb_ref[...]
        p_ref[:, D_A:D_A + D_B] = (ln * jax.nn.sigmoid(ln)).astype(BF16)

        row = lax.broadcasted_iota(jnp.int32, (L, L), 0)
        col = lax.broadcasted_iota(jnp.int32, (L, L), 1)
        same = (row // seq_rows) == (col // seq_rows)
        causal = same & (col <= row)
        g = gates_ref[...] + bif_ref[...]
        logf = jnp.minimum(g, 0.0) - jnp.log1p(jnp.exp(-jnp.abs(g)))
        bt = _dot_f32(causal.astype(F32), logf)
        bt_h = pltpu.roll(bt, 128 - H_C, axis=1)
        if nseq == 1:
            btl_h = bt_h[L - 1:L, :]
        else:
            btl_h = pltpu.roll(_dot_f32(same.astype(F32), logf), 128 - H_C, axis=1)
        g_t = g.T
        bt_t = bt.T
        m_rows = m_s[0:1, :] if carry else m0_ref[0]
        inter = bt_h + m_rows
        lane = lax.broadcasted_iota(jnp.int32, (L, W_GATE), 1)

        dlogs = []
        rmax = jnp.zeros((L, W_GATE), F32)
        for h in range(H_C):
            dlog = jnp.where(causal, bt[:, H_C + h:H_C + h + 1] - bt_t[H_C + h:H_C + h + 1, :]
                             + g_t[h:h + 1, :], -jnp.inf)
            dlogs.append(dlog)
            rmax = jnp.where(lane == h, jnp.max(dlog, axis=-1, keepdims=True), rmax)
        m_t = jnp.maximum(inter, rmax)
        w_inter = jnp.exp(inter - m_t)
        floor = jnp.exp(-m_t)
        if nseq == 1:
            m_new = m_t[L - 1:L, :]
        else:
            last = (col == (row // seq_rows) * seq_rows + (seq_rows - 1)).astype(F32)
            m_new = _dot_f32(last, m_t)
        w_last = jnp.exp(btl_h - bt_h + g - m_new)
        w_prev = jnp.exp(inter - m_new)
        n_rows = n_s[0:1, :] if carry else n0_ref[0]

        den = jnp.zeros((L, W_GATE), F32)
        for h in range(H_C):
            hs = slice(h * DH_C, (h + 1) * DH_C)
            q = zm_ref[:, OFF_Q + h * DH_C:OFF_Q + (h + 1) * DH_C]
            k = zm_ref[:, OFF_K + h * DH_C:OFF_K + (h + 1) * DH_C]
            v = zm_ref[:, OFF_V + h * DH_C:OFF_V + (h + 1) * DH_C]
            s = _dot_nt(q, k) * (DH_C ** -0.5) * jnp.exp(dlogs[h] - m_t[:, h:h + 1])
            num_s[:, hs] = _dot(s.astype(BF16), v)
            qn_all = _dot_nt(q, jnp.broadcast_to(n_rows[:, hs], (L, DH_C)).astype(BF16))
            if nseq == 1:
                qn = qn_all[:, 0:1]
            else:
                qn = jnp.sum(jnp.where(row == col, qn_all, 0.0), axis=-1, keepdims=True)
            den_h = jnp.sum(s, axis=-1, keepdims=True) + w_inter[:, h:h + 1] * qn
            den = jnp.where(lane == h, den_h, den)
            vw_s[h] = (v.astype(F32) * w_last[:, h:h + 1]).astype(BF16)
        if nseq > 1:
            qc_s[...] = jnp.zeros(qc_s.shape, F32)
        winter_s[...] = w_inter
        rinv_s[...] = 1.0 / jnp.maximum(jnp.abs(den), floor)
        wlt_s[...] = w_last.T[0:16, :]
        wprev_s[...] = jnp.broadcast_to(w_prev, (L, W_GATE))
        mnew_s[...] = jnp.broadcast_to(m_new, (L, W_GATE))

    spp = c0_ref.shape[1]
    tile = spp * seq_rows if nseq > 1 else L
    for u in range(spp):
        if nseq == 1:
            wprev_row = wprev_s[L - 1:L, :]
            m_out_row = mnew_s[L - 1:L, :]
        else:
            sq = j * spp + u
            last_row = sq * seq_rows + (seq_rows - 1)
            lane_l = lax.broadcasted_iota(jnp.int32, (1, L), 1)
            row_l = lax.broadcasted_iota(jnp.int32, (L, 1), 0)
            seq_lanes = (lane_l // seq_rows) == sq
            seq_rows_mask = (row_l // seq_rows) == sq
            wprev_row = wprev_s[pl.ds(last_row, 1), :]
            m_out_row = mnew_s[pl.ds(last_row, 1), :]
        for h in range(H_C):
            hs = slice(h * DH_C, (h + 1) * DH_C)
            c_old = c_s[h] if carry else c0_ref[0, u, h]
            c_bf = c_old.astype(BF16)
            if nseq == 1:
                qc_s[:, hs] = _dot_nt(zm_ref[:, OFF_Q + h * DH_C:OFF_Q + (h + 1) * DH_C], c_bf)
            else:
                r0 = pl.multiple_of(j * tile, tile)
                q16 = zm_ref[pl.ds(r0, tile), OFF_Q + h * DH_C:OFF_Q + (h + 1) * DH_C]
                r = _dot_nt(q16, c_bf)
                rid = lax.broadcasted_iota(jnp.int32, (tile, 1), 0)
                qc_s[pl.ds(r0, tile), hs] = jnp.where((rid // seq_rows) == u, r, qc_s[pl.ds(r0, tile), hs])
            k = zm_ref[:, OFF_K + h * DH_C:OFF_K + (h + 1) * DH_C]
            if nseq == 1:
                vw = vw_s[h]
                wl = wlt_s[...]
            else:
                vw = jnp.where(seq_rows_mask, vw_s[h], jnp.zeros((L, DH_C), BF16))
                wl = jnp.where(seq_lanes, wlt_s[...], 0.0)
            kv = _dot_tn(vw, k)
            w_prev = wprev_row[:, h:h + 1]
            c_new = w_prev * c_old + kv * (DH_C ** -0.5)
            ksum = _dot(wl.astype(BF16), k)[h:h + 1, :]
            if carry:
                n_old = n_s[0:1, hs]
            else:
                n_old = n0_ref[0, pl.ds(sq * seq_rows, 1), hs]
            n_new = w_prev * n_old + ksum * (DH_C ** -0.5)
            if carry:
                c_s[h] = c_new
                n_s[0:1, hs] = n_new
            c_out_ref[0, u, h] = c_new
            n_out_ref[u, h:h + 1, :] = n_new
        if carry:
            m_s[0:1, :] = m_out_row
        m_out_ref[u] = m_out_row

    @_when(j == nseq // spp - 1)
    def _post():
        for h in range(H_C):
            hs = slice(h * DH_C, (h + 1) * DH_C)
            hh = (num_s[:, hs] + qc_s[:, hs] * winter_s[:, h:h + 1]) * rinv_s[:, h:h + 1]
            hn = hh * lax.rsqrt(jnp.mean(hh * hh, axis=-1, keepdims=True) + EPS) * mng_ref[:, hs]
            o = zm_ref[:, OFF_O + h * DH_C:OFF_O + (h + 1) * DH_C].astype(F32)
            p_ref[:, 2 * D_A + h * DH_C:2 * D_A + (h + 1) * DH_C] = (jax.nn.sigmoid(o) * hn).astype(BF16)


def _mixer(zm, gates, ca0, cb0, c0, n0, m0, caw, cbw, cbb, lng, lnb, bif, mng, c_acc, *,
           row_block0, n_groups, n_chunks, seq_rows, carry, layer, layer_in, layer_out):
    L = CHUNK
    nseq = L // seq_rows
    n_state = n_groups if carry else n_chunks * nseq
    rows = n_groups * n_chunks * L

    if carry:
        pair = PROMPT_PAIR
        spp = pair

        def seq_of(b, t, j):
            return b

        def conv_of(b, t, j):
            return b
        rows_blk = 1

        cps = PROMPT_CHUNKS_PER_STEP
        assert n_chunks % cps == 0 and row_block0 % cps == 0
        tok_rows = cps * L

        def tok(u):
            return lambda b, t, j: (row_block0 // cps + (b * pair + u) * (n_chunks // cps) + t, 0)

        def conv_in(taps, ch):
            return pl.BlockSpec((1, pair, taps - 1, ch), lambda *g: (layer_in, conv_of(*g), 0, 0))

        def conv_out(taps, ch):
            return (pl.BlockSpec((pair, taps - 1, ch), lambda *g: (conv_of(*g), 0, 0)),
                    jax.ShapeDtypeStruct((n_state, taps - 1, ch), F32))
        p_spec = pl.BlockSpec((pair, tok_rows, W_P), lambda b, t, j: (b, t, 0))
        p_shape = jax.ShapeDtypeStruct((n_groups, n_chunks * L, W_P), BF16)
        grid = (n_groups // pair, n_chunks // cps, 1)
        lead = (pair,)
    else:
        pair = 1
        tok_rows = L
        spp = 16 // seq_rows

        def tok(u):
            return lambda b, t, j: (row_block0 + b * n_chunks + t, 0)
        p_spec = pl.BlockSpec((L, W_P), lambda b, t, j: (b * n_chunks + t, 0))
        p_shape = jax.ShapeDtypeStruct((rows, W_P), BF16)
        grid = (n_groups, n_chunks, nseq // spp)
        lead = ()

        def seq_of(b, t, j):
            return t * (nseq // spp) + j

        def conv_of(b, t, j):
            return t
        rows_blk = L

        def conv_in(taps, ch):
            return pl.BlockSpec((None, taps - 1, nseq, ch), lambda *g: (layer_in, 0, conv_of(*g), 0))

        def conv_out(taps, ch):
            return (pl.BlockSpec((taps - 1, nseq, ch), lambda *g: (0, conv_of(*g), 0)),
                    jax.ShapeDtypeStruct((taps - 1, n_state, ch), F32))

    has_acc = c_acc is not None
    kern = functools.partial(_mixer_kernel, seq_rows=seq_rows, carry=carry, has_acc=has_acc)
    in_specs = [pl.BlockSpec((tok_rows, W_MAIN), tok(u)) for u in range(pair)]
    in_specs += [pl.BlockSpec((tok_rows, W_GATE), tok(u)) for u in range(pair)]
    in_specs += [
        conv_in(K_A, D_A),
        conv_in(K_B, D_B),
        pl.BlockSpec((1, spp, H_C, DH_C, DH_C), lambda *g: (layer_in, seq_of(*g), 0, 0, 0)),
        pl.BlockSpec((None, pair, rows_blk, D_MODEL), lambda *g: (layer_in, conv_of(*g), 0, 0)),
        pl.BlockSpec((None, pair, rows_blk, W_GATE), lambda *g: (layer_in, conv_of(*g), 0, 0)),
        _resident((K_A, D_A), layer),
        _resident((K_B, D_B), layer),
        _resident((1, D_B), layer),
        _resident((1, D_B), layer),
        _resident((1, D_B), layer),
        _resident((1, W_GATE), layer),
        _resident((1, D_MODEL), layer),
    ]
    args = [zm] * pair + [gates] * pair + [ca0, cb0, c0, n0, m0, caw, cbw, cbb, lng, lnb, bif, mng]
    aliases = {}
    if has_acc:
        in_specs.append(pl.BlockSpec(memory_space=pl.ANY))
        args.append(c_acc)
        aliases = {len(args) - 1: 3}
    outs = pl.pallas_call(
        kern,
        grid=grid,
        in_specs=in_specs,
        out_specs=[
            p_spec,
            conv_out(K_A, D_A)[0],
            conv_out(K_B, D_B)[0],
            pl.BlockSpec((1, spp, H_C, DH_C, DH_C), lambda *g: (layer_out, seq_of(*g), 0, 0, 0)),
            pl.BlockSpec((spp, H_C, DH_C), lambda *g: (seq_of(*g), 0, 0)),
            pl.BlockSpec((spp, 1, W_GATE), lambda *g: (seq_of(*g), 0, 0)),
        ],
        out_shape=[
            p_shape,
            conv_out(K_A, D_A)[1],
            conv_out(K_B, D_B)[1],
            jax.ShapeDtypeStruct((DEPTH, n_state, H_C, DH_C, DH_C), F32),
            jax.ShapeDtypeStruct((n_state, H_C, DH_C), F32),
            jax.ShapeDtypeStruct((n_state, 1, W_GATE), F32),
        ],
        input_output_aliases=aliases,
        scratch_shapes=[
            pltpu.VMEM(lead + (8 + L, D_A), F32),
            pltpu.VMEM(lead + (32 + L, D_B), F32),
            pltpu.VMEM(lead + (8, 24 + L, D_B) if carry else (8, 8, 128), F32),
            pltpu.VMEM(lead + (L, D_A), F32),
            pltpu.VMEM(lead + (L, D_B), F32),
            pltpu.VMEM(lead + (H_C, DH_C, DH_C), F32),
            pltpu.VMEM(lead + (8, D_MODEL), F32),
            pltpu.VMEM(lead + (8, W_GATE), F32),
            pltpu.VMEM((L, D_MODEL), F32),
            pltpu.VMEM((L, D_MODEL), F32),
            pltpu.VMEM((H_C, L, DH_C), BF16),
            pltpu.VMEM((L, W_GATE), F32),
            pltpu.VMEM((L, W_GATE), F32),
            pltpu.VMEM((16, L), F32),
            pltpu.VMEM((L, W_GATE), F32),
            pltpu.VMEM((L, W_GATE), F32),
        ],
        compiler_params=_params(("arbitrary", "arbitrary", "arbitrary")),
        name="mixer_prompt" if carry else "mixer_sample",
    )(*args)
    outs = list(outs)
    outs[0] = outs[0].reshape(rows, W_P)
    return outs


def _outproj_kernel(x_ref, pp_ref, ps_ref, zg_ref, wa_ref, wb_ref, wc_ref, wo_ref, gx_ref, wq_ref,
                    x1_ref, q_ref, *, n_prompt_tiles):
    i = pl.program_id(0)
    p = jnp.where(i >= n_prompt_tiles, ps_ref[...], pp_ref[...])
    y_a = _dot(p[:, 0:D_A], wa_ref[...])
    y_b = _dot(p[:, D_A:D_A + D_B], wb_ref[...])
    y_c = _dot(p[:, D_A + D_B:], wc_ref[...])
    u = (jax.nn.sigmoid(zg_ref[:, 0:D_MODEL].astype(F32)) * y_a
         + jax.nn.sigmoid(zg_ref[:, D_MODEL:2 * D_MODEL].astype(F32)) * y_b
         + jax.nn.sigmoid(zg_ref[:, 2 * D_MODEL:].astype(F32)) * y_c)
    x1 = x_ref[...] + _dot(u.astype(BF16), wo_ref[...])
    x1_ref[...] = x1
    q_ref[...] = _dot(_rmsnorm(x1, gx_ref[...]).astype(BF16), wq_ref[...]).astype(BF16)


def _outproj(x, p_p, p_s, zg, wa, wb, wc, wo, gx, wq, layer):
    n = x.shape[0]
    tm = ROW_TILE
    npt = p_p.shape[0] // tm
    return pl.pallas_call(
        functools.partial(_outproj_kernel, n_prompt_tiles=npt),
        grid=(n // tm,),
        in_specs=[
            pl.BlockSpec((tm, D_MODEL), lambda i: (i, 0)),
            pl.BlockSpec((tm, W_P), lambda i: (jnp.minimum(i, npt - 1), 0)),
            pl.BlockSpec((tm, W_P), lambda i: (0, 0)),
            pl.BlockSpec((tm, W_MERGE), lambda i: (i, 0)),
            _resident((D_A, D_MODEL), layer),
            _resident((D_B, D_MODEL), layer),
            _resident((D_MODEL, D_MODEL), layer),
            _resident((D_MODEL, D_MODEL), layer),
            _resident((1, D_MODEL), layer),
            _resident((D_MODEL, D_MODEL), layer),
        ],
        out_specs=[
            pl.BlockSpec((tm, D_MODEL), lambda i: (i, 0)),
            pl.BlockSpec((tm, D_MODEL), lambda i: (i, 0)),
        ],
        out_shape=[
            jax.ShapeDtypeStruct((n, D_MODEL), F32),
            jax.ShapeDtypeStruct((n, D_MODEL), BF16),
        ],
        compiler_params=_params(("parallel",)),
        name="outproj",
    )(x, p_p, p_s, zg, wa, wb, wc, wo, gx, wq)


def _memkv_kernel(mem_ref, g_ref, w_ref, k_ref, v_ref, kb_ref, vb_ref):
    h = _rmsnorm(mem_ref[...], g_ref[0]).astype(BF16)
    hd = HX * DX
    k = _dot(h, w_ref[0, :, 0:hd])
    v = _dot(h, w_ref[0, :, hd:])
    for bb in range(k_ref.shape[1]):
        for hh in range(HX):
            k_ref[0, bb, :, hh, :] = k[bb * N_MEM:(bb + 1) * N_MEM, hh * DX:(hh + 1) * DX]
            v_ref[0, bb, :, hh, :] = v[bb * N_MEM:(bb + 1) * N_MEM, hh * DX:(hh + 1) * DX]
    kb_ref[0] = k.astype(BF16)
    vb_ref[0] = v.astype(BF16)


def _memkv(mem, g, w):
    n = mem.shape[0]
    tm = ROW_TILE
    hd = HX * DX
    per = tm // N_MEM
    o_spec = pl.BlockSpec((1, tm, hd), lambda l, i: (l, i, 0))
    o5_spec = pl.BlockSpec((1, per, N_MEM, HX, DX), lambda l, i: (l, i, 0, 0, 0))
    return pl.pallas_call(
        _memkv_kernel,
        grid=(DEPTH, n // tm),
        in_specs=[
            pl.BlockSpec((tm, D_MODEL), lambda l, i: (i, 0)),
            pl.BlockSpec((1, 1, D_MODEL), lambda l, i: (l, 0, 0)),
            pl.BlockSpec((1, D_MODEL, 2 * hd), lambda l, i: (l, 0, 0)),
        ],
        out_specs=[o5_spec, o5_spec, o_spec, o_spec],
        out_shape=[
            jax.ShapeDtypeStruct((DEPTH, n // N_MEM, N_MEM, HX, DX), F32),
            jax.ShapeDtypeStruct((DEPTH, n // N_MEM, N_MEM, HX, DX), F32),
            jax.ShapeDtypeStruct((DEPTH, n, hd), BF16),
            jax.ShapeDtypeStruct((DEPTH, n, hd), BF16),
        ],
        compiler_params=_params(("arbitrary", "arbitrary")),
        name="memkv",
    )(mem, g, w)


def _attend(q, k, v):
    s = _dot_nt(q, k) * (DX ** -0.5)
    e = jnp.exp(s - jnp.max(s, axis=-1, keepdims=True))
    return _dot(e.astype(BF16), v) * (1.0 / jnp.sum(e, axis=-1, keepdims=True))


XS_ROWS = 16


def _xattn_kernel(q_ref, k_ref, v_ref, qs_ref, ks_ref, vs_ref, o_ref, os_ref, *, seq_rows):
    for h in range(HX):
        hs = slice(h * DX, (h + 1) * DX)
        o_ref[:, hs] = _attend(q_ref[:, hs], k_ref[0, :, hs], v_ref[0, :, hs]).astype(BF16)
    _xattn_short(qs_ref, ks_ref, vs_ref, os_ref, seq_rows)


def _xattn(q, kb, vb, k_cache, v_cache, layer, *, n_groups, rows_per_group, seq_rows):
    tq = ROW_TILE
    nt = rows_per_group // tq
    hd = HX * DX
    per = XS_ROWS // seq_rows
    n_seq = k_cache.shape[1]
    assert n_groups * nt * per == n_seq
    qs_block0 = n_groups * rows_per_group // XS_ROWS
    kv_spec = pl.BlockSpec((1, N_MEM, hd), lambda b, t: (layer, b, 0))
    cache_spec = pl.BlockSpec((1, per, N_MEM, HX, DX), lambda b, t: (layer, b * nt + t, 0, 0, 0))
    return pl.pallas_call(
        functools.partial(_xattn_kernel, seq_rows=seq_rows),
        grid=(n_groups, nt),
        in_specs=[pl.BlockSpec((tq, hd), lambda b, t: (b * nt + t, 0)), kv_spec, kv_spec,
                  pl.BlockSpec((XS_ROWS, hd), lambda b, t: (qs_block0 + b * nt + t, 0)),
                  cache_spec, cache_spec],
        out_specs=[pl.BlockSpec((tq, hd), lambda b, t: (b * nt + t, 0)),
                   pl.BlockSpec((XS_ROWS, hd), lambda b, t: (b * nt + t, 0))],
        out_shape=[jax.ShapeDtypeStruct((n_groups * rows_per_group, hd), BF16),
                   jax.ShapeDtypeStruct((n_seq * seq_rows, hd), BF16)],
        compiler_params=_params(("parallel", "parallel")),
        name="xattn",
    )(q, kb, vb, q, k_cache, v_cache)


def _xattn_short(q_ref, k_ref, v_ref, o_ref, seq_rows):
    rows = q_ref.shape[0]
    qs = jnp.concatenate([q_ref[:, h * DX:(h + 1) * DX] for h in range(HX)], axis=0)
    rid = lax.broadcasted_iota(jnp.int32, (HX * rows, 1), 0)
    cid = lax.broadcasted_iota(jnp.int32, (HX * rows, N_MEM * HX), 1)
    own_head = (cid % HX) == (rid // rows)
    acc = jnp.zeros((HX * rows, DX), F32)
    for e in range(rows // seq_rows):
        k2 = k_ref[0, e].reshape(N_MEM * HX, DX).astype(BF16)
        v2 = v_ref[0, e].reshape(N_MEM * HX, DX).astype(BF16)
        s = jnp.where(own_head, _dot_nt(qs, k2) * (DX ** -0.5), -jnp.inf)
        p = jnp.exp(s - jnp.max(s, axis=-1, keepdims=True))
        o = _dot(p.astype(BF16), v2) * (1.0 / jnp.sum(p, axis=-1, keepdims=True))
        acc = jnp.where(((rid % rows) // seq_rows) == e, o, acc)
    for h in range(HX):
        o_ref[:, h * DX:(h + 1) * DX] = acc[h * rows:(h + 1) * rows].astype(BF16)


FF_BLOCK = 256


def _ffn_kernel(x_ref, cp_ref, cs_ref, wxo_ref, gf_ref, wg_ref, wu_ref, wout_ref, gfin_ref,
                *rest, n_prompt_tiles, final):
    act_s = rest[-1]
    i = pl.program_id(0)
    ctx = jnp.where(i >= n_prompt_tiles, cs_ref[...], cp_ref[...])
    x2 = x_ref[...] + _dot(ctx, wxo_ref[...])
    h = _rmsnorm(x2, gf_ref[...]).astype(BF16)
    for jb in range(D_FF // FF_BLOCK):
        sl = slice(jb * FF_BLOCK, (jb + 1) * FF_BLOCK)
        gate = _dot(h, wg_ref[:, sl])
        up = _dot(h, wu_ref[:, sl])
        act_s[:, sl] = (gate * jax.nn.sigmoid(gate) * up).astype(BF16)
    x3 = x2 + _dot(act_s[...], wout_ref[...])
    if final:
        y = _rmsnorm(x3, gfin_ref[...])
        yp_ref, ys_ref = rest[0], rest[1]

        @pl.when(i < n_prompt_tiles)
        def _():
            yp_ref[...] = y

        @pl.when(i >= n_prompt_tiles)
        def _():
            ys_ref[...] = y
    else:
        rest[0][...] = x3


def _ffn(x, c_p, c_s, wxo, gf, w_in, wout, gfin, layer, *, final):
    n = x.shape[0]
    tm = ROW_TILE
    npt = c_p.shape[0] // tm
    if final:
        assert n - npt * tm == tm
        out_specs = [pl.BlockSpec((tm, D_MODEL), lambda i: (jnp.minimum(i, npt - 1), 0)),
                     pl.BlockSpec((tm, D_MODEL), lambda i: (0, 0))]
        out_shape = [jax.ShapeDtypeStruct((npt * tm, D_MODEL), F32),
                     jax.ShapeDtypeStruct((tm, D_MODEL), F32)]
        sem = ("arbitrary",)
    else:
        out_specs = pl.BlockSpec((tm, D_MODEL), lambda i: (i, 0))
        out_shape = jax.ShapeDtypeStruct((n, D_MODEL), F32)
        sem = ("parallel",)
    return pl.pallas_call(
        functools.partial(_ffn_kernel, n_prompt_tiles=npt, final=final),
        grid=(n // tm,),
        in_specs=[
            pl.BlockSpec((tm, D_MODEL), lambda i: (i, 0)),
            pl.BlockSpec((tm, D_MODEL), lambda i: (jnp.minimum(i, npt - 1), 0)),
            pl.BlockSpec((tm, D_MODEL), lambda i: (0, 0)),
            _resident((D_MODEL, D_MODEL), layer),
            _resident((1, D_MODEL), layer),
            _resident((D_MODEL, D_FF), layer, col=0),
            _resident((D_MODEL, D_FF), layer, col=1),
            _resident((D_FF, D_MODEL), layer),
            _resident((1, D_MODEL)),
        ],
        out_specs=out_specs,
        out_shape=out_shape,
        scratch_shapes=[pltpu.VMEM((tm, D_FF), BF16)],
        compiler_params=_params(sem),
        name="ffn",
    )(x, c_p, c_s, wxo, gf, w_in, w_in, wout, gfin)


def kernel(x_prompt, x_sample, state_conv_a, state_conv_b, state_mlstm_c, state_mlstm_n, state_mlstm_m,
           cache_mem_k, cache_mem_v, mem_prompt, norm_mix_g, w_in, b_if, conv_a_w, w_out_a, conv_b_w,
           conv_b_b, ln_b_g, ln_b_b, w_out_b, mlstm_norm_g, w_out_c, w_o, norm_x_g, norm_mem_g, w_xq,
           w_xkv, w_xo, norm_ffn_g, w_ffn_in, w_ffn_out, final_norm_g):
    bp, tp, d = x_prompt.shape
    bs, ts, _ = x_sample.shape
    n_p = bp * tp
    n_s = bs * ts
    hd = HX * DX

    gate_lo = W_MAIN
    gate_hi = W_MAIN + 2 * H_C
    w_inb = w_in.astype(BF16)
    w_gate = jnp.pad(w_inb[:, :, gate_lo:gate_hi], ((0, 0), (0, 0), (0, W_GATE - 2 * H_C)))
    w_merge = w_inb[:, :, gate_hi:]
    bif = jnp.pad(b_if, ((0, 0), (0, W_GATE - 2 * H_C)))[:, None, :]
    w_a = w_out_a.astype(BF16)
    w_b = w_out_b.astype(BF16)
    w_c = w_out_c.astype(BF16)
    w_ob = w_o.astype(BF16)
    w_q = w_xq.astype(BF16)
    w_kv = w_xkv.astype(BF16)
    w_xob = w_xo.astype(BF16)
    w_fi = w_ffn_in.astype(BF16)
    w_fo = w_ffn_out.astype(BF16)
    g_mix = norm_mix_g[:, None, :]
    g_x = norm_x_g[:, None, :]
    g_ffn = norm_ffn_g[:, None, :]
    small = (conv_a_w, conv_b_w, conv_b_b[:, None, :], ln_b_g[:, None, :], ln_b_b[:, None, :],
             bif, mlstm_norm_g[:, None, :])

    x = jnp.concatenate([x_prompt.reshape(n_p, d), x_sample.reshape(n_s, d)], axis=0)

    mem_k, mem_v, mem_kb, mem_vb = _memkv(mem_prompt.reshape(bp * N_MEM, d), norm_mem_g[:, None, :], w_kv)

    zeros_ca = jnp.zeros((1, bp, K_A - 1, D_A), F32)
    zeros_cb = jnp.zeros((1, bp, K_B - 1, D_B), F32)
    zeros_c = jnp.zeros((1, bp, H_C, DH_C, DH_C), F32)
    zeros_n = jnp.zeros((1, bp, 1, H_C * DH_C), F32)
    zeros_m = jnp.zeros((1, bp, 1, W_GATE), F32)

    n_chunks_s = n_s // CHUNK
    n0_rows = jnp.repeat(state_mlstm_n.reshape(DEPTH, bs, H_C * DH_C), ts, axis=1)
    n0_rows = n0_rows.reshape(DEPTH, n_chunks_s, CHUNK, H_C * DH_C)
    m0_rows = jnp.repeat(jnp.pad(state_mlstm_m, ((0, 0), (0, 0), (0, W_GATE - H_C))), ts, axis=1)
    m0_rows = m0_rows.reshape(DEPTH, n_chunks_s, CHUNK, W_GATE)

    conv_a_tm = jnp.transpose(state_conv_a, (0, 2, 1, 3))
    conv_b_tm = jnp.transpose(state_conv_b, (0, 2, 1, 3))

    outs = {k: [] for k in ("pa", "pb", "pn", "pm", "sa", "sb", "sn", "sm")}
    c1 = c2 = None
    for l in range(DEPTH):
        zm, zg, gates = _inproj(x, g_mix, w_inb, w_merge, w_gate, l)
        p_p, a1, b1, c1, n1, m1 = _mixer(
            zm, gates, zeros_ca, zeros_cb, zeros_c, zeros_n, zeros_m, *small, c1,
            row_block0=0, n_groups=bp, n_chunks=tp // CHUNK, seq_rows=CHUNK, carry=True,
            layer=l, layer_in=0, layer_out=l)
        p_s, a2, b2, c2, n2, m2 = _mixer(
            zm, gates, conv_a_tm, conv_b_tm, state_mlstm_c, n0_rows, m0_rows, *small, c2,
            row_block0=n_p // CHUNK, n_groups=1, n_chunks=n_chunks_s, seq_rows=ts, carry=False,
            layer=l, layer_in=l, layer_out=l)
        x1, qx = _outproj(x, p_p, p_s, zg, w_a, w_b, w_c, w_ob, g_x, w_q, l)
        c_p, c_s = _xattn(qx, mem_kb, mem_vb, cache_mem_k, cache_mem_v, l,
                          n_groups=bp, rows_per_group=tp, seq_rows=ts)
        x = _ffn(x1, c_p, c_s, w_xob, g_ffn, w_fi, w_fo, final_norm_g[None, :], l, final=(l == DEPTH - 1))
        outs["pa"].append(a1); outs["pb"].append(b1); outs["pn"].append(n1); outs["pm"].append(m1)
        outs["sa"].append(a2); outs["sb"].append(b2); outs["sn"].append(n2); outs["sm"].append(m2)

    y_prompt = x[0].reshape(bp, tp, d)
    y_sample = x[1].reshape(bs, ts, d)
    st = {k: jnp.stack(v) for k, v in outs.items()}
    return (y_prompt, y_sample,
            st["pa"], st["pb"], c1, st["pn"], st["pm"][:, :, 0, :H_C],
            mem_k, mem_v,
            jnp.transpose(st["sa"], (0, 2, 1, 3)), jnp.transpose(st["sb"], (0, 2, 1, 3)), c2,
            st["sn"], st["sm"][:, :, 0, :H_C])
```

```python
import functools

import jax
import jax.numpy as jnp
from jax import lax
from jax.experimental import pallas as pl
from jax.experimental.pallas import tpu as pltpu

D_MODEL = 1024
DEPTH = 4
D_A = 512
K_A = 3
D_B = 512
K_B = 31
H_C = 4
DH_C = 256
N_MEM = 256
HX = 4
DX = 256
D_FF = 2816
EPS = 1e-6

OFF_AB, OFF_AC, OFF_AX, OFF_BV, OFF_BG = 0, 512, 1024, 1536, 2048
OFF_Q, OFF_K, OFF_V, OFF_O = 2560, 3584, 4608, 5632
W_MAIN = 6656
W_MERGE = 3 * D_MODEL
W_GATE = 128
W_P = 2 * D_A + D_MODEL

CHUNK = 128
SAMPLE_ROWS_PER_STEP = 32
PROMPT_PAIR = 1
PROMPT_CHUNKS_PER_STEP = 1
ROW_TILE = 512
VMEM_LIMIT_BYTES = 56 * 1024 * 1024

F32 = jnp.float32
BF16 = jnp.bfloat16


def _dot(a, b):
    return jnp.dot(a, b, preferred_element_type=F32)


def _dot_nt(a, b):
    return lax.dot_general(a, b, (((1,), (1,)), ((), ())), preferred_element_type=F32)


def _dot_tn(a, b):
    return lax.dot_general(a, b, (((0,), (0,)), ((), ())), preferred_element_type=F32)


def _dot_f32(a, b):
    return jnp.dot(a, b, preferred_element_type=F32, precision=lax.Precision.HIGHEST)


def _rmsnorm(x, g):
    ms = jnp.mean(x * x, axis=-1, keepdims=True)
    return x * lax.rsqrt(ms + EPS) * g


def _resident(shape, layer=None, col=0):
    nd = len(shape)
    if layer is None:
        return pl.BlockSpec(shape, lambda *_: (0,) * nd, pipeline_mode=pl.Buffered(1))
    return pl.BlockSpec((None,) + shape, lambda *_: (layer,) + (0,) * (nd - 1) + (col,),
                        pipeline_mode=pl.Buffered(1))


def _params(sem):
    return pltpu.CompilerParams(dimension_semantics=sem, vmem_limit_bytes=VMEM_LIMIT_BYTES)


def _inproj_kernel(x_ref, g_ref, wm_ref, wg_ref, wgate_ref, zm_ref, zg_ref, gates_ref):
    h = _rmsnorm(x_ref[...], g_ref[...]).astype(BF16)
    for j in range(W_MAIN // 512):
        sl = slice(j * 512, (j + 1) * 512)
        zm_ref[:, sl] = _dot(h, wm_ref[:, sl]).astype(BF16)
    for j in range(W_MERGE // 512):
        sl = slice(j * 512, (j + 1) * 512)
        zg_ref[:, sl] = _dot(h, wg_ref[:, sl]).astype(BF16)
    gates_ref[...] = _dot(h, wgate_ref[...])


def _inproj(x, g, wm, wg, wgate, layer):
    n = x.shape[0]
    tm = ROW_TILE
    return pl.pallas_call(
        _inproj_kernel,
        grid=(n // tm,),
        in_specs=[
            pl.BlockSpec((tm, D_MODEL), lambda i: (i, 0)),
            _resident((1, D_MODEL), layer),
            _resident((D_MODEL, W_MAIN), layer),
            _resident((D_MODEL, W_MERGE), layer),
            _resident((D_MODEL, W_GATE), layer),
        ],
        out_specs=[
            pl.BlockSpec((tm, W_MAIN), lambda i: (i, 0)),
            pl.BlockSpec((tm, W_MERGE), lambda i: (i, 0)),
            pl.BlockSpec((tm, W_GATE), lambda i: (i, 0)),
        ],
        out_shape=[
            jax.ShapeDtypeStruct((n, W_MAIN), BF16),
            jax.ShapeDtypeStruct((n, W_MERGE), BF16),
            jax.ShapeDtypeStruct((n, W_GATE), F32),
        ],
        compiler_params=_params(("parallel",)),
        name="inproj",
    )(x, g, wm, wg, wgate)


def _prompt_init(ca0_ref, cb0_ref, c0_ref, n0_ref, m0_ref, xa_s, xb_s, c_s, n_s, m_s):
    xa_s[0:8, :] = jnp.zeros((8, D_A), F32)
    xb_s[0:32, :] = jnp.zeros((32, D_B), F32)
    xa_s[6:8, :] = ca0_ref[...]
    xb_s[2:32, :] = cb0_ref[...]
    c_s[...] = c0_ref[...]
    n_s[0:1, :] = n0_ref[...]
    m_s[0:1, :] = m0_ref[...]


def _prompt_chunk(zm_ref, gates_ref, caw_ref, cbw_ref, cbb_ref, lng_ref, lnb_ref, bif_ref, mng_ref,
                  p_ref, sa_ref, sb_ref, c_out_ref, n_out_ref, m_out_ref,
                  xa_s, xb_s, xbr_s, aconv_s, bconv_s, c_s, n_s, m_s):
    L = CHUNK

    xa_s[8:8 + L, :] = (zm_ref[:, OFF_AC:OFF_AC + D_A].astype(F32)
                        * zm_ref[:, OFF_AX:OFF_AX + D_A].astype(F32))
    xb_s[32:32 + L, :] = zm_ref[:, OFF_BV:OFF_BV + D_B].astype(F32) * jax.nn.sigmoid(
        zm_ref[:, OFF_BG:OFF_BG + D_B].astype(F32))
    for r in range(1, 8):
        xbr_s[r] = xb_s[r:r + 24 + L, :]

    def conv_block(cblk):
        cs = slice(cblk * 128, (cblk + 1) * 128)
        acc = caw_ref[0:1, cs] * xa_s[6:6 + L, cs]
        for k in range(1, K_A):
            acc = acc + caw_ref[k:k + 1, cs] * xa_s[6 + k:6 + k + L, cs]
        aconv_s[:, cs] = acc
        acc = None
        for k in range(K_B):
            a8, r = (2 + k) // 8 * 8, (2 + k) % 8
            src = xb_s[a8:a8 + L, cs] if r == 0 else xbr_s[r, a8:a8 + L, cs]
            term = cbw_ref[k:k + 1, cs] * src
            acc = term if acc is None else acc + term
        bconv_s[:, cs] = acc

    row = lax.broadcasted_iota(jnp.int32, (L, L), 0)
    col = lax.broadcasted_iota(jnp.int32, (L, L), 1)
    causal = col <= row
    g = gates_ref[...] + bif_ref[...]
    logf = jnp.minimum(g, 0.0) - jnp.log1p(jnp.exp(-jnp.abs(g)))
    bt = _dot_f32(causal.astype(F32), logf)
    bt_h = pltpu.roll(bt, 128 - H_C, axis=1)
    g_t = g.T
    bt_t = bt.T
    inter = bt_h + m_s[0:1, :]
    lane = lax.broadcasted_iota(jnp.int32, (L, W_GATE), 1)
    dlogs = []
    rmax = jnp.zeros((L, W_GATE), F32)
    for h in range(H_C):
        dlog = jnp.where(causal, bt[:, H_C + h:H_C + h + 1] - bt_t[H_C + h:H_C + h + 1, :]
                         + g_t[h:h + 1, :], -jnp.inf)
        dlogs.append(dlog)
        rmax = jnp.where(lane == h, jnp.max(dlog, axis=-1, keepdims=True), rmax)
    m_t = jnp.maximum(inter, rmax)
    w_inter = jnp.exp(inter - m_t)
    floor = jnp.exp(-m_t)
    m_new = m_t[L - 1:L, :]
    w_last = jnp.exp(bt_h[L - 1:L, :] - bt_h + g - m_new)
    w_prev = jnp.exp(inter[L - 1:L, :] - m_new)
    wl16 = w_last.T[0:16, :].astype(BF16)
    n_row = n_s[0:1, :]

    def head(h):
        hs = slice(h * DH_C, (h + 1) * DH_C)
        q = zm_ref[:, OFF_Q + h * DH_C:OFF_Q + (h + 1) * DH_C]
        k = zm_ref[:, OFF_K + h * DH_C:OFF_K + (h + 1) * DH_C]
        v = zm_ref[:, OFF_V + h * DH_C:OFF_V + (h + 1) * DH_C]
        c_old = c_s[h]
        s = _dot_nt(q, k) * (DH_C ** -0.5) * jnp.exp(dlogs[h] - m_t[:, h:h + 1])
        num = _dot(s.astype(BF16), v)
        qc = _dot_nt(q, c_old.astype(BF16))
        qn = _dot_nt(q, jnp.broadcast_to(n_row[:, hs], (L, DH_C)).astype(BF16))[:, 0:1]
        wi = w_inter[:, h:h + 1]
        den = jnp.sum(s, axis=-1, keepdims=True) + wi * qn
        rinv = 1.0 / jnp.maximum(jnp.abs(den), floor[:, h:h + 1])
        hh = (num + qc * wi) * rinv
        hn = hh * lax.rsqrt(jnp.mean(hh * hh, axis=-1, keepdims=True) + EPS) * mng_ref[:, hs]
        o = zm_ref[:, OFF_O + h * DH_C:OFF_O + (h + 1) * DH_C].astype(F32)
        p_ref[:, 2 * D_A + h * DH_C:2 * D_A + (h + 1) * DH_C] = (jax.nn.sigmoid(o) * hn).astype(BF16)
        vw = (v.astype(F32) * w_last[:, h:h + 1]).astype(BF16)
        wp = w_prev[:, h:h + 1]
        c_new = wp * c_old + _dot_tn(vw, k) * (DH_C ** -0.5)
        n_new = wp * n_row[:, hs] + _dot(wl16, k)[h:h + 1, :] * (DH_C ** -0.5)
        c_s[h] = c_new
        c_out_ref[h] = c_new
        n_s[0:1, hs] = n_new
        n_out_ref[h:h + 1, :] = n_new

    conv_block(0)
    conv_block(1)
    head(0)
    conv_block(2)
    head(1)
    conv_block(3)
    head(2)

    sa_new = xa_s[6 + L:8 + L, :]
    sb_new = xb_s[2 + L:32 + L, :]
    xa_s[6:8, :] = sa_new
    xb_s[2:32, :] = sb_new
    sa_ref[...] = sa_new
    sb_ref[...] = sb_new
    p_ref[:, 0:D_A] = (zm_ref[:, OFF_AB:OFF_AB + D_A].astype(F32) * aconv_s[...]).astype(BF16)
    bc = bconv_s[...] + cbb_ref[...]
    mu = jnp.mean(bc, axis=-1, keepdims=True)
    xc = bc - mu
    ln = xc * lax.rsqrt(jnp.mean(xc * xc, axis=-1, keepdims=True) + EPS) * lng_ref[...] + lnb_ref[...]
    p_ref[:, D_A:D_A + D_B] = (ln * jax.nn.sigmoid(ln)).astype(BF16)

    head(3)
    m_s[0:1, :] = m_new
    m_out_ref[...] = m_new


def _mixer_kernel(*refs, seq_rows, carry, has_acc):
    pair = PROMPT_PAIR if carry else 1
    zm_refs, gates_refs, refs = refs[:pair], refs[pair:2 * pair], refs[2 * pair:]
    (ca0_ref, cb0_ref, c0_ref, n0_ref, m0_ref,
     caw_ref, cbw_ref, cbb_ref, lng_ref, lnb_ref, bif_ref, mng_ref) = refs[:12]
    refs = refs[13:] if has_acc else refs[12:]
    p_ref, sa_ref, sb_ref, c_out_ref, n_out_ref, m_out_ref = refs[:6]
    zm_ref, gates_ref = zm_refs[0], gates_refs[0]
    L = CHUNK
    nseq = L // seq_rows
    t = pl.program_id(1)
    j = pl.program_id(2)

    if carry:
        assert nseq == 1
        xa_s, xb_s, xbr_s, aconv_s, bconv_s, c_s, n_s, m_s = refs[6:]

        @pl.when(t == 0)
        def _():
            for u in range(pair):
                _prompt_init(ca0_ref.at[0, u], cb0_ref.at[0, u], c0_ref.at[0, u], n0_ref.at[u], m0_ref.at[u],
                             xa_s.at[u], xb_s.at[u], c_s.at[u], n_s.at[u], m_s.at[u])

        def chunk(ci, _):
            rs = pl.ds(ci * L if isinstance(ci, int) else pl.multiple_of(ci * L, L), L)
            for u in range(pair):
                _prompt_chunk(zm_refs[u].at[rs], gates_refs[u].at[rs],
                              caw_ref, cbw_ref, cbb_ref, lng_ref, lnb_ref, bif_ref, mng_ref,
                              p_ref.at[u, rs], sa_ref.at[u], sb_ref.at[u], c_out_ref.at[0, u], n_out_ref.at[u],
                              m_out_ref.at[u], xa_s.at[u], xb_s.at[u], xbr_s.at[u], aconv_s.at[u],
                              bconv_s.at[u], c_s.at[u], n_s.at[u], m_s.at[u])
            return 0

        n_inner = zm_ref.shape[0] // L
        if n_inner == 1:
            chunk(0, 0)
        else:
            lax.fori_loop(0, n_inner, chunk, 0)
        return

    aconv_s, bconv_s, num_s, qc_s, vw_s, winter_s, rinv_s, wlt_s, wprev_s, mnew_s = refs[6:]

    @pl.when(j == 0)
    def _pre():
        a_b = zm_ref[:, OFF_AB:OFF_AB + D_A].astype(F32)
        ca = zm_ref[:, OFF_AC:OFF_AC + D_A].astype(F32) * zm_ref[:, OFF_AX:OFF_AX + D_A].astype(F32)
        cb = zm_ref[:, OFF_BV:OFF_BV + D_B].astype(F32) * jax.nn.sigmoid(
            zm_ref[:, OFF_BG:OFF_BG + D_B].astype(F32))

        r_i = lax.broadcasted_iota(jnp.int32, (L, L), 0)
        c_i = lax.broadcasted_iota(jnp.int32, (L, L), 1)
        to_tok = (c_i == (r_i % nseq) * seq_rows + r_i // nseq).astype(F32)
        to_seq = (c_i == (r_i % seq_rows) * nseq + r_i // seq_rows).astype(F32)

        def short_conv(x, st_ref, w_ref, taps, new_ref):
            x_tok = _dot_f32(to_tok, x)
            window = [st_ref[i] for i in range(taps - 1)]
            window += [x_tok[tk * nseq:(tk + 1) * nseq] for tk in range(seq_rows)]
            outs_tok = []
            for tk in range(seq_rows):
                acc = w_ref[0:1, :] * window[tk]
                for k in range(1, taps):
                    acc = acc + w_ref[k:k + 1, :] * window[tk + k]
                outs_tok.append(acc)
            for i in range(taps - 1):
                new_ref[i] = window[seq_rows + i]
            return _dot_f32(to_seq, jnp.concatenate(outs_tok, axis=0))

        aconv_s[...] = short_conv(ca, ca0_ref, caw_ref, K_A, sa_ref)
        bconv_s[...] = short_conv(cb, cb0_ref, cbw_ref, K_B, sb_ref)

        p_ref[:, 0:D_A] = (a_b * aconv_s[...]).astype(BF16)
        bc = bconv_s[...] + cbb_ref[...]
        mu = jnp.mean(bc, axis=-1, keepdims=True)
        xc = bc - mu
        ln = xc * lax.rsqrt(jnp.mean(xc * xc, axis=-1, keepdims=True) + EPS) * lng_ref[...] + lnb_ref[...]
        p_ref[:, D_A:D_A + D_B] = (ln * jax.nn.sigmoid(ln)).astype(BF16)

        row = lax.broadcasted_iota(jnp.int32, (L, L), 0)
        col = lax.broadcasted_iota(jnp.int32, (L, L), 1)
        same = (row // seq_rows) == (col // seq_rows)
        causal = same & (col <= row)
        g = gates_ref[...] + bif_ref[...]
        logf = jnp.minimum(g, 0.0) - jnp.log1p(jnp.exp(-jnp.abs(g)))
        bt = _dot_f32(causal.astype(F32), logf)
        bt_h = pltpu.roll(bt, 128 - H_C, axis=1)
        btl_h = pltpu.roll(_dot_f32(same.astype(F32), logf), 128 - H_C, axis=1)
        g_t = g.T
        bt_t = bt.T
        inter = bt_h + m0_ref[0]
        lane = lax.broadcasted_iota(jnp.int32, (L, W_GATE), 1)

        dlogs = []
        rmax = jnp.zeros((L, W_GATE), F32)
        for h in range(H_C):
            dlog = jnp.where(causal, bt[:, H_C + h:H_C + h + 1] - bt_t[H_C + h:H_C + h + 1, :]
                             + g_t[h:h + 1, :], -jnp.inf)
            dlogs.append(dlog)
            rmax = jnp.where(lane == h, jnp.max(dlog, axis=-1, keepdims=True), rmax)
        m_t = jnp.maximum(inter, rmax)
        w_inter = jnp.exp(inter - m_t)
        floor = jnp.exp(-m_t)
        last = (col == (row // seq_rows) * seq_rows + (seq_rows - 1)).astype(F32)
        m_new = _dot_f32(last, m_t)
        w_last = jnp.exp(btl_h - bt_h + g - m_new)
        w_prev = jnp.exp(inter - m_new)
        n_rows = n0_ref[0]

        den = jnp.zeros((L, W_GATE), F32)
        for h in range(H_C):
            hs = slice(h * DH_C, (h + 1) * DH_C)
            q = zm_ref[:, OFF_Q + h * DH_C:OFF_Q + (h + 1) * DH_C]
            k = zm_ref[:, OFF_K + h * DH_C:OFF_K + (h + 1) * DH_C]
            v = zm_ref[:, OFF_V + h * DH_C:OFF_V + (h + 1) * DH_C]
            s = _dot_nt(q, k) * (DH_C ** -0.5) * jnp.exp(dlogs[h] - m_t[:, h:h + 1])
            num_s[:, hs] = _dot(s.astype(BF16), v)
            qn_all = _dot_nt(q, n_rows[:, hs].astype(BF16))
            qn = jnp.sum(jnp.where(row == col, qn_all, 0.0), axis=-1, keepdims=True)
            den_h = jnp.sum(s, axis=-1, keepdims=True) + w_inter[:, h:h + 1] * qn
            den = jnp.where(lane == h, den_h, den)
            vw_s[h] = (v.astype(F32) * w_last[:, h:h + 1]).astype(BF16)
        qc_s[...] = jnp.zeros(qc_s.shape, F32)
        winter_s[...] = w_inter
        rinv_s[...] = 1.0 / jnp.maximum(jnp.abs(den), floor)
        wlt_s[...] = w_last.T[0:16, :]
        wprev_s[...] = w_prev
        mnew_s[...] = m_new

    spp = c0_ref.shape[1]
    tile = spp * seq_rows
    r0 = pl.multiple_of(j * tile, tile)
    rid = lax.broadcasted_iota(jnp.int32, (tile, 1), 0)
    lane_l = lax.broadcasted_iota(jnp.int32, (1, L), 1)
    row_l = lax.broadcasted_iota(jnp.int32, (L, 1), 0)
    for u in range(spp):
        sq = j * spp + u
        last_row = sq * seq_rows + (seq_rows - 1)
        seq_lanes = (lane_l // seq_rows) == sq
        seq_rows_mask = (row_l // seq_rows) == sq
        wprev_row = wprev_s[pl.ds(last_row, 1), :]
        for h in range(H_C):
            hs = slice(h * DH_C, (h + 1) * DH_C)
            c_old = c0_ref[0, u, h]
            q16 = zm_ref[pl.ds(r0, tile), OFF_Q + h * DH_C:OFF_Q + (h + 1) * DH_C]
            r = _dot_nt(q16, c_old.astype(BF16))
            qc_s[pl.ds(r0, tile), hs] = jnp.where((rid // seq_rows) == u, r, qc_s[pl.ds(r0, tile), hs])
            k = zm_ref[:, OFF_K + h * DH_C:OFF_K + (h + 1) * DH_C]
            vw = jnp.where(seq_rows_mask, vw_s[h], jnp.zeros((L, DH_C), BF16))
            wl = jnp.where(seq_lanes, wlt_s[...], 0.0)
            w_prev = wprev_row[:, h:h + 1]
            c_out_ref[0, u, h] = w_prev * c_old + _dot_tn(vw, k) * (DH_C ** -0.5)
            ksum = _dot(wl.astype(BF16), k)[h:h + 1, :]
            n_old = n0_ref[0, pl.ds(sq * seq_rows, 1), hs]
            n_out_ref[u, h:h + 1, :] = w_prev * n_old + ksum * (DH_C ** -0.5)
        m_out_ref[u] = mnew_s[pl.ds(last_row, 1), :]

    @pl.when(j == nseq // spp - 1)
    def _post():
        for h in range(H_C):
            hs = slice(h * DH_C, (h + 1) * DH_C)
            hh = (num_s[:, hs] + qc_s[:, hs] * winter_s[:, h:h + 1]) * rinv_s[:, h:h + 1]
            hn = hh * lax.rsqrt(jnp.mean(hh * hh, axis=-1, keepdims=True) + EPS) * mng_ref[:, hs]
            o = zm_ref[:, OFF_O + h * DH_C:OFF_O + (h + 1) * DH_C].astype(F32)
            p_ref[:, 2 * D_A + h * DH_C:2 * D_A + (h + 1) * DH_C] = (jax.nn.sigmoid(o) * hn).astype(BF16)


def _mixer(zm, gates, ca0, cb0, c0, n0, m0, caw, cbw, cbb, lng, lnb, bif, mng, c_acc, *,
           row_block0, n_groups, n_chunks, seq_rows, carry, layer, layer_in, layer_out):
    L = CHUNK
    nseq = L // seq_rows
    n_state = n_groups if carry else n_chunks * nseq
    rows = n_groups * n_chunks * L

    if carry:
        pair = PROMPT_PAIR
        spp = pair

        def seq_of(b, t, j):
            return b

        def conv_of(b, t, j):
            return b
        rows_blk = 1

        cps = PROMPT_CHUNKS_PER_STEP
        assert n_chunks % cps == 0 and row_block0 % cps == 0
        tok_rows = cps * L

        def tok(u):
            return lambda b, t, j: (row_block0 // cps + (b * pair + u) * (n_chunks // cps) + t, 0)

        def conv_in(taps, ch):
            return pl.BlockSpec((1, pair, taps - 1, ch), lambda *g: (layer_in, conv_of(*g), 0, 0))

        def conv_out(taps, ch):
            return (pl.BlockSpec((pair, taps - 1, ch), lambda *g: (conv_of(*g), 0, 0)),
                    jax.ShapeDtypeStruct((n_state, taps - 1, ch), F32))
        p_spec = pl.BlockSpec((pair, tok_rows, W_P), lambda b, t, j: (b, t, 0))
        p_shape = jax.ShapeDtypeStruct((n_groups, n_chunks * L, W_P), BF16)
        grid = (n_groups // pair, n_chunks // cps, 1)
        lead = (pair,)
    else:
        pair = 1
        tok_rows = L
        spp = SAMPLE_ROWS_PER_STEP // seq_rows

        def tok(u):
            return lambda b, t, j: (row_block0 + b * n_chunks + t, 0)
        p_spec = pl.BlockSpec((L, W_P), lambda b, t, j: (b * n_chunks + t, 0))
        p_shape = jax.ShapeDtypeStruct((rows, W_P), BF16)
        grid = (n_groups, n_chunks, nseq // spp)
        lead = ()

        def seq_of(b, t, j):
            return t * (nseq // spp) + j

        def conv_of(b, t, j):
            return t
        rows_blk = L

        def conv_in(taps, ch):
            return pl.BlockSpec((None, taps - 1, nseq, ch), lambda *g: (layer_in, 0, conv_of(*g), 0))

        def conv_out(taps, ch):
            return (pl.BlockSpec((taps - 1, nseq, ch), lambda *g: (0, conv_of(*g), 0)),
                    jax.ShapeDtypeStruct((taps - 1, n_state, ch), F32))

    has_acc = c_acc is not None
    kern = functools.partial(_mixer_kernel, seq_rows=seq_rows, carry=carry, has_acc=has_acc)
    in_specs = [pl.BlockSpec((tok_rows, W_MAIN), tok(u)) for u in range(pair)]
    in_specs += [pl.BlockSpec((tok_rows, W_GATE), tok(u)) for u in range(pair)]
    in_specs += [
        conv_in(K_A, D_A),
        conv_in(K_B, D_B),
        pl.BlockSpec((1, spp, H_C, DH_C, DH_C), lambda *g: (layer_in, seq_of(*g), 0, 0, 0)),
        pl.BlockSpec((None, pair, rows_blk, D_MODEL), lambda *g: (layer_in, conv_of(*g), 0, 0)),
        pl.BlockSpec((None, pair, rows_blk, W_GATE), lambda *g: (layer_in, conv_of(*g), 0, 0)),
        _resident((K_A, D_A), layer),
        _resident((K_B, D_B), layer),
        _resident((1, D_B), layer),
        _resident((1, D_B), layer),
        _resident((1, D_B), layer),
        _resident((1, W_GATE), layer),
        _resident((1, D_MODEL), layer),
    ]
    args = [zm] * pair + [gates] * pair + [ca0, cb0, c0, n0, m0, caw, cbw, cbb, lng, lnb, bif, mng]
    aliases = {}
    if has_acc:
        in_specs.append(pl.BlockSpec(memory_space=pl.ANY))
        args.append(c_acc)
        aliases = {len(args) - 1: 3}
    outs = pl.pallas_call(
        kern,
        grid=grid,
        in_specs=in_specs,
        out_specs=[
            p_spec,
            conv_out(K_A, D_A)[0],
            conv_out(K_B, D_B)[0],
            pl.BlockSpec((1, spp, H_C, DH_C, DH_C), lambda *g: (layer_out, seq_of(*g), 0, 0, 0)),
            pl.BlockSpec((spp, H_C, DH_C), lambda *g: (seq_of(*g), 0, 0)),
            pl.BlockSpec((spp, 1, W_GATE), lambda *g: (seq_of(*g), 0, 0)),
        ],
        out_shape=[
            p_shape,
            conv_out(K_A, D_A)[1],
            conv_out(K_B, D_B)[1],
            jax.ShapeDtypeStruct((DEPTH, n_state, H_C, DH_C, DH_C), F32),
            jax.ShapeDtypeStruct((n_state, H_C, DH_C), F32),
            jax.ShapeDtypeStruct((n_state, 1, W_GATE), F32),
        ],
        input_output_aliases=aliases,
        scratch_shapes=[
            pltpu.VMEM(lead + (8 + L, D_A), F32),
            pltpu.VMEM(lead + (32 + L, D_B), F32),
            pltpu.VMEM(lead + (8, 24 + L, D_B), F32),
            pltpu.VMEM(lead + (L, D_A), F32),
            pltpu.VMEM(lead + (L, D_B), F32),
            pltpu.VMEM(lead + (H_C, DH_C, DH_C), F32),
            pltpu.VMEM(lead + (8, D_MODEL), F32),
            pltpu.VMEM(lead + (8, W_GATE), F32),
        ] if carry else [
            pltpu.VMEM((L, D_A), F32),
            pltpu.VMEM((L, D_B), F32),
            pltpu.VMEM((L, D_MODEL), F32),
            pltpu.VMEM((L, D_MODEL), F32),
            pltpu.VMEM((H_C, L, DH_C), BF16),
            pltpu.VMEM((L, W_GATE), F32),
            pltpu.VMEM((L, W_GATE), F32),
            pltpu.VMEM((16, L), F32),
            pltpu.VMEM((L, W_GATE), F32),
            pltpu.VMEM((L, W_GATE), F32),
        ],
        compiler_params=_params(("arbitrary", "arbitrary", "arbitrary")),
        name="mixer_prompt" if carry else "mixer_sample",
    )(*args)
    outs = list(outs)
    outs[0] = outs[0].reshape(rows, W_P)
    return outs


def _outproj_kernel(x_ref, pp_ref, ps_ref, zg_ref, wa_ref, wb_ref, wc_ref, wo_ref, gx_ref, wq_ref,
                    x1_ref, q_ref, *, n_prompt_tiles):
    i = pl.program_id(0)
    p = jnp.where(i >= n_prompt_tiles, ps_ref[...], pp_ref[...])
    y_a = _dot(p[:, 0:D_A], wa_ref[...])
    y_b = _dot(p[:, D_A:D_A + D_B], wb_ref[...])
    y_c = _dot(p[:, D_A + D_B:], wc_ref[...])
    u = (jax.nn.sigmoid(zg_ref[:, 0:D_MODEL].astype(F32)) * y_a
         + jax.nn.sigmoid(zg_ref[:, D_MODEL:2 * D_MODEL].astype(F32)) * y_b
         + jax.nn.sigmoid(zg_ref[:, 2 * D_MODEL:].astype(F32)) * y_c)
    x1 = x_ref[...] + _dot(u.astype(BF16), wo_ref[...])
    x1_ref[...] = x1
    q_ref[...] = _dot(_rmsnorm(x1, gx_ref[...]).astype(BF16), wq_ref[...]).astype(BF16)


def _outproj(x, p_p, p_s, zg, wa, wb, wc, wo, gx, wq, layer):
    n = x.shape[0]
    tm = ROW_TILE
    npt = p_p.shape[0] // tm
    return pl.pallas_call(
        functools.partial(_outproj_kernel, n_prompt_tiles=npt),
        grid=(n // tm,),
        in_specs=[
            pl.BlockSpec((tm, D_MODEL), lambda i: (i, 0)),
            pl.BlockSpec((tm, W_P), lambda i: (jnp.minimum(i, npt - 1), 0)),
            pl.BlockSpec((tm, W_P), lambda i: (0, 0)),
            pl.BlockSpec((tm, W_MERGE), lambda i: (i, 0)),
            _resident((D_A, D_MODEL), layer),
            _resident((D_B, D_MODEL), layer),
            _resident((D_MODEL, D_MODEL), layer),
            _resident((D_MODEL, D_MODEL), layer),
            _resident((1, D_MODEL), layer),
            _resident((D_MODEL, D_MODEL), layer),
        ],
        out_specs=[
            pl.BlockSpec((tm, D_MODEL), lambda i: (i, 0)),
            pl.BlockSpec((tm, D_MODEL), lambda i: (i, 0)),
        ],
        out_shape=[
            jax.ShapeDtypeStruct((n, D_MODEL), F32),
            jax.ShapeDtypeStruct((n, D_MODEL), BF16),
        ],
        compiler_params=_params(("parallel",)),
        name="outproj",
    )(x, p_p, p_s, zg, wa, wb, wc, wo, gx, wq)


def _memkv_kernel(mem_ref, g_ref, w_ref, k_ref, v_ref, kb_ref, vb_ref):
    h = _rmsnorm(mem_ref[...], g_ref[0]).astype(BF16)
    hd = HX * DX
    k = _dot(h, w_ref[0, :, 0:hd])
    v = _dot(h, w_ref[0, :, hd:])
    for bb in range(k_ref.shape[1]):
        for hh in range(HX):
            k_ref[0, bb, :, hh, :] = k[bb * N_MEM:(bb + 1) * N_MEM, hh * DX:(hh + 1) * DX]
            v_ref[0, bb, :, hh, :] = v[bb * N_MEM:(bb + 1) * N_MEM, hh * DX:(hh + 1) * DX]
    kb_ref[0] = k.astype(BF16)
    vb_ref[0] = v.astype(BF16)


def _memkv(mem, g, w):
    n = mem.shape[0]
    tm = ROW_TILE
    hd = HX * DX
    per = tm // N_MEM
    o_spec = pl.BlockSpec((1, tm, hd), lambda l, i: (l, i, 0))
    o5_spec = pl.BlockSpec((1, per, N_MEM, HX, DX), lambda l, i: (l, i, 0, 0, 0))
    return pl.pallas_call(
        _memkv_kernel,
        grid=(DEPTH, n // tm),
        in_specs=[
            pl.BlockSpec((tm, D_MODEL), lambda l, i: (i, 0)),
            pl.BlockSpec((1, 1, D_MODEL), lambda l, i: (l, 0, 0)),
            pl.BlockSpec((1, D_MODEL, 2 * hd), lambda l, i: (l, 0, 0)),
        ],
        out_specs=[o5_spec, o5_spec, o_spec, o_spec],
        out_shape=[
            jax.ShapeDtypeStruct((DEPTH, n // N_MEM, N_MEM, HX, DX), F32),
            jax.ShapeDtypeStruct((DEPTH, n // N_MEM, N_MEM, HX, DX), F32),
            jax.ShapeDtypeStruct((DEPTH, n, hd), BF16),
            jax.ShapeDtypeStruct((DEPTH, n, hd), BF16),
        ],
        compiler_params=_params(("arbitrary", "arbitrary")),
        name="memkv",
    )(mem, g, w)


def _attend(q, k, v):
    s = _dot_nt(q, k) * (DX ** -0.5)
    e = jnp.exp(s - jnp.max(s, axis=-1, keepdims=True))
    return _dot(e.astype(BF16), v) * (1.0 / jnp.sum(e, axis=-1, keepdims=True))


XS_ROWS = 16


def _xattn_kernel(q_ref, k_ref, v_ref, qs_ref, ks_ref, vs_ref, o_ref, os_ref, *, seq_rows):
    for h in range(HX):
        hs = slice(h * DX, (h + 1) * DX)
        o_ref[:, hs] = _attend(q_ref[:, hs], k_ref[0, :, hs], v_ref[0, :, hs]).astype(BF16)
    _xattn_short(qs_ref, ks_ref, vs_ref, os_ref, seq_rows)


def _xattn(q, kb, vb, k_cache, v_cache, layer, *, n_groups, rows_per_group, seq_rows):
    tq = ROW_TILE
    nt = rows_per_group // tq
    hd = HX * DX
    per = XS_ROWS // seq_rows
    n_seq = k_cache.shape[1]
    assert n_groups * nt * per == n_seq
    qs_block0 = n_groups * rows_per_group // XS_ROWS
    kv_spec = pl.BlockSpec((1, N_MEM, hd), lambda b, t: (layer, b, 0))
    cache_spec = pl.BlockSpec((1, per, N_MEM, HX, DX), lambda b, t: (layer, b * nt + t, 0, 0, 0))
    return pl.pallas_call(
        functools.partial(_xattn_kernel, seq_rows=seq_rows),
        grid=(n_groups, nt),
        in_specs=[pl.BlockSpec((tq, hd), lambda b, t: (b * nt + t, 0)), kv_spec, kv_spec,
                  pl.BlockSpec((XS_ROWS, hd), lambda b, t: (qs_block0 + b * nt + t, 0)),
                  cache_spec, cache_spec],
        out_specs=[pl.BlockSpec((tq, hd), lambda b, t: (b * nt + t, 0)),
                   pl.BlockSpec((XS_ROWS, hd), lambda b, t: (b * nt + t, 0))],
        out_shape=[jax.ShapeDtypeStruct((n_groups * rows_per_group, hd), BF16),
                   jax.ShapeDtypeStruct((n_seq * seq_rows, hd), BF16)],
        compiler_params=_params(("parallel", "parallel")),
        name="xattn",
    )(q, kb, vb, q, k_cache, v_cache)


def _xattn_short(q_ref, k_ref, v_ref, o_ref, seq_rows):
    rows = q_ref.shape[0]
    qs = jnp.concatenate([q_ref[:, h * DX:(h + 1) * DX] for h in range(HX)], axis=0)
    rid = lax.broadcasted_iota(jnp.int32, (HX * rows, 1), 0)
    cid = lax.broadcasted_iota(jnp.int32, (HX * rows, N_MEM * HX), 1)
    own_head = (cid % HX) == (rid // rows)
    acc = jnp.zeros((HX * rows, DX), F32)
    for e in range(rows // seq_rows):
        k2 = k_ref[0, e].reshape(N_MEM * HX, DX).astype(BF16)
        v2 = v_ref[0, e].reshape(N_MEM * HX, DX).astype(BF16)
        s = jnp.where(own_head, _dot_nt(qs, k2) * (DX ** -0.5), -jnp.inf)
        p = jnp.exp(s - jnp.max(s, axis=-1, keepdims=True))
        o = _dot(p.astype(BF16), v2) * (1.0 / jnp.sum(p, axis=-1, keepdims=True))
        acc = jnp.where(((rid % rows) // seq_rows) == e, o, acc)
    for h in range(HX):
        o_ref[:, h * DX:(h + 1) * DX] = acc[h * rows:(h + 1) * rows].astype(BF16)


FF_BLOCK = 256


def _ffn_kernel(x_ref, cp_ref, cs_ref, wxo_ref, gf_ref, wg_ref, wu_ref, wout_ref, gfin_ref,
                *rest, n_prompt_tiles, final):
    act_s = rest[-1]
    i = pl.program_id(0)
    ctx = jnp.where(i >= n_prompt_tiles, cs_ref[...], cp_ref[...])
    x2 = x_ref[...] + _dot(ctx, wxo_ref[...])
    h = _rmsnorm(x2, gf_ref[...]).astype(BF16)
    for jb in range(D_FF // FF_BLOCK):
        sl = slice(jb * FF_BLOCK, (jb + 1) * FF_BLOCK)
        gate = _dot(h, wg_ref[:, sl])
        up = _dot(h, wu_ref[:, sl])
        act_s[:, sl] = (gate * jax.nn.sigmoid(gate) * up).astype(BF16)
    x3 = x2 + _dot(act_s[...], wout_ref[...])
    if final:
        y = _rmsnorm(x3, gfin_ref[...])
        yp_ref, ys_ref = rest[0], rest[1]

        @pl.when(i < n_prompt_tiles)
        def _():
            yp_ref[...] = y

        @pl.when(i >= n_prompt_tiles)
        def _():
            ys_ref[...] = y
    else:
        rest[0][...] = x3


def _ffn(x, c_p, c_s, wxo, gf, w_in, wout, gfin, layer, *, final):
    n = x.shape[0]
    tm = ROW_TILE
    npt = c_p.shape[0] // tm
    if final:
        assert n - npt * tm == tm
        out_specs = [pl.BlockSpec((tm, D_MODEL), lambda i: (jnp.minimum(i, npt - 1), 0)),
                     pl.BlockSpec((tm, D_MODEL), lambda i: (0, 0))]
        out_shape = [jax.ShapeDtypeStruct((npt * tm, D_MODEL), F32),
                     jax.ShapeDtypeStruct((tm, D_MODEL), F32)]
        sem = ("arbitrary",)
    else:
        out_specs = pl.BlockSpec((tm, D_MODEL), lambda i: (i, 0))
        out_shape = jax.ShapeDtypeStruct((n, D_MODEL), F32)
        sem = ("parallel",)
    return pl.pallas_call(
        functools.partial(_ffn_kernel, n_prompt_tiles=npt, final=final),
        grid=(n // tm,),
        in_specs=[
            pl.BlockSpec((tm, D_MODEL), lambda i: (i, 0)),
            pl.BlockSpec((tm, D_MODEL), lambda i: (jnp.minimum(i, npt - 1), 0)),
            pl.BlockSpec((tm, D_MODEL), lambda i: (0, 0)),
            _resident((D_MODEL, D_MODEL), layer),
            _resident((1, D_MODEL), layer),
            _resident((D_MODEL, D_FF), layer, col=0),
            _resident((D_MODEL, D_FF), layer, col=1),
            _resident((D_FF, D_MODEL), layer),
            _resident((1, D_MODEL)),
        ],
        out_specs=out_specs,
        out_shape=out_shape,
        scratch_shapes=[pltpu.VMEM((tm, D_FF), BF16)],
        compiler_params=_params(sem),
        name="ffn",
    )(x, c_p, c_s, wxo, gf, w_in, w_in, wout, gfin)


def kernel(x_prompt, x_sample, state_conv_a, state_conv_b, state_mlstm_c, state_mlstm_n, state_mlstm_m,
           cache_mem_k, cache_mem_v, mem_prompt, norm_mix_g, w_in, b_if, conv_a_w, w_out_a, conv_b_w,
           conv_b_b, ln_b_g, ln_b_b, w_out_b, mlstm_norm_g, w_out_c, w_o, norm_x_g, norm_mem_g, w_xq,
           w_xkv, w_xo, norm_ffn_g, w_ffn_in, w_ffn_out, final_norm_g):
    bp, tp, d = x_prompt.shape
    bs, ts, _ = x_sample.shape
    n_p = bp * tp
    n_s = bs * ts
    hd = HX * DX

    gate_lo = W_MAIN
    gate_hi = W_MAIN + 2 * H_C
    w_inb = w_in.astype(BF16)
    w_gate = jnp.pad(w_inb[:, :, gate_lo:gate_hi], ((0, 0), (0, 0), (0, W_GATE - 2 * H_C)))
    w_merge = w_inb[:, :, gate_hi:]
    bif = jnp.pad(b_if, ((0, 0), (0, W_GATE - 2 * H_C)))[:, None, :]
    w_a = w_out_a.astype(BF16)
    w_b = w_out_b.astype(BF16)
    w_c = w_out_c.astype(BF16)
    w_ob = w_o.astype(BF16)
    w_q = w_xq.astype(BF16)
    w_kv = w_xkv.astype(BF16)
    w_xob = w_xo.astype(BF16)
    w_fi = w_ffn_in.astype(BF16)
    w_fo = w_ffn_out.astype(BF16)
    g_mix = norm_mix_g[:, None, :]
    g_x = norm_x_g[:, None, :]
    g_ffn = norm_ffn_g[:, None, :]
    small = (conv_a_w, conv_b_w, conv_b_b[:, None, :], ln_b_g[:, None, :], ln_b_b[:, None, :],
             bif, mlstm_norm_g[:, None, :])

    x = jnp.concatenate([x_prompt.reshape(n_p, d), x_sample.reshape(n_s, d)], axis=0)

    mem_k, mem_v, mem_kb, mem_vb = _memkv(mem_prompt.reshape(bp * N_MEM, d), norm_mem_g[:, None, :], w_kv)

    zeros_ca = jnp.zeros((1, bp, K_A - 1, D_A), F32)
    zeros_cb = jnp.zeros((1, bp, K_B - 1, D_B), F32)
    zeros_c = jnp.zeros((1, bp, H_C, DH_C, DH_C), F32)
    zeros_n = jnp.zeros((1, bp, 1, H_C * DH_C), F32)
    zeros_m = jnp.zeros((1, bp, 1, W_GATE), F32)

    n_chunks_s = n_s // CHUNK
    n0_rows = jnp.repeat(state_mlstm_n.reshape(DEPTH, bs, H_C * DH_C), ts, axis=1)
    n0_rows = n0_rows.reshape(DEPTH, n_chunks_s, CHUNK, H_C * DH_C)
    m0_rows = jnp.repeat(jnp.pad(state_mlstm_m, ((0, 0), (0, 0), (0, W_GATE - H_C))), ts, axis=1)
    m0_rows = m0_rows.reshape(DEPTH, n_chunks_s, CHUNK, W_GATE)

    conv_a_tm = jnp.transpose(state_conv_a, (0, 2, 1, 3))
    conv_b_tm = jnp.transpose(state_conv_b, (0, 2, 1, 3))

    outs = {k: [] for k in ("pa", "pb", "pn", "pm", "sa", "sb", "sn", "sm")}
    c1 = c2 = None
    for l in range(DEPTH):
        zm, zg, gates = _inproj(x, g_mix, w_inb, w_merge, w_gate, l)
        p_p, a1, b1, c1, n1, m1 = _mixer(
            zm, gates, zeros_ca, zeros_cb, zeros_c, zeros_n, zeros_m, *small, c1,
            row_block0=0, n_groups=bp, n_chunks=tp // CHUNK, seq_rows=CHUNK, carry=True,
            layer=l, layer_in=0, layer_out=l)
        p_s, a2, b2, c2, n2, m2 = _mixer(
            zm, gates, conv_a_tm, conv_b_tm, state_mlstm_c, n0_rows, m0_rows, *small, c2,
            row_block0=n_p // CHUNK, n_groups=1, n_chunks=n_chunks_s, seq_rows=ts, carry=False,
            layer=l, layer_in=l, layer_out=l)
        x1, qx = _outproj(x, p_p, p_s, zg, w_a, w_b, w_c, w_ob, g_x, w_q, l)
        c_p, c_s = _xattn(qx, mem_kb, mem_vb, cache_mem_k, cache_mem_v, l,
                          n_groups=bp, rows_per_group=tp, seq_rows=ts)
        x = _ffn(x1, c_p, c_s, w_xob, g_ffn, w_fi, w_fo, final_norm_g[None, :], l, final=(l == DEPTH - 1))
        outs["pa"].append(a1); outs["pb"].append(b1); outs["pn"].append(n1); outs["pm"].append(m1)
        outs["sa"].append(a2); outs["sb"].append(b2); outs["sn"].append(n2); outs["sm"].append(m2)

    y_prompt = x[0].reshape(bp, tp, d)
    y_sample = x[1].reshape(bs, ts, d)
    st = {k: jnp.stack(v) for k, v in outs.items()}
    return (y_prompt, y_sample,
            st["pa"], st["pb"], c1, st["pn"], st["pm"][:, :, 0, :H_C],
            mem_k, mem_v,
            jnp.transpose(st["sa"], (0, 2, 1, 3)), jnp.transpose(st["sb"], (0, 2, 1, 3)), c2,
            st["sn"], st["sm"][:, :, 0, :H_C])
```

```python
import functools

import jax
import jax.numpy as jnp
from jax import lax
from jax.experimental import pallas as pl
from jax.experimental.pallas import tpu as pltpu

D_MODEL = 1024
DEPTH = 4
D_A = 512
K_A = 3
D_B = 512
K_B = 31
H_C = 4
DH_C = 256
N_MEM = 256
HX = 4
DX = 256
D_FF = 2816
EPS = 1e-6

OFF_AB, OFF_AC, OFF_AX, OFF_BV, OFF_BG = 0, 512, 1024, 1536, 2048
OFF_Q, OFF_K, OFF_V, OFF_O = 2560, 3584, 4608, 5632
W_MAIN = 6656
W_MERGE = 3 * D_MODEL
W_GATE = 128
W_P = 2 * D_A + D_MODEL

CHUNK = 128
SAMPLE_ROWS_PER_STEP = 32
PROMPT_PAIR = 1
PROMPT_CHUNKS_PER_STEP = 1
ROW_TILE = 512
VMEM_LIMIT_BYTES = 56 * 1024 * 1024

F32 = jnp.float32
BF16 = jnp.bfloat16


def _dot(a, b):
    return jnp.dot(a, b, preferred_element_type=F32)


def _dot_nt(a, b):
    return lax.dot_general(a, b, (((1,), (1,)), ((), ())), preferred_element_type=F32)


def _dot_tn(a, b):
    return lax.dot_general(a, b, (((0,), (0,)), ((), ())), preferred_element_type=F32)


def _dot_f32(a, b):
    return jnp.dot(a, b, preferred_element_type=F32, precision=lax.Precision.HIGHEST)


def _rmsnorm(x, g):
    ms = jnp.mean(x * x, axis=-1, keepdims=True)
    return x * lax.rsqrt(ms + EPS) * g


def _resident(shape, layer=None, col=0):
    nd = len(shape)
    if layer is None:
        return pl.BlockSpec(shape, lambda *_: (0,) * nd, pipeline_mode=pl.Buffered(1))
    return pl.BlockSpec((None,) + shape, lambda *_: (layer,) + (0,) * (nd - 1) + (col,),
                        pipeline_mode=pl.Buffered(1))


def _params(sem):
    return pltpu.CompilerParams(dimension_semantics=sem, vmem_limit_bytes=VMEM_LIMIT_BYTES)


def _inproj_kernel(x_ref, g_ref, wm_ref, wg_ref, wgate_ref, zm_ref, zg_ref, gates_ref):
    h = _rmsnorm(x_ref[...], g_ref[...]).astype(BF16)
    for j in range(W_MAIN // 512):
        sl = slice(j * 512, (j + 1) * 512)
        zm_ref[:, sl] = _dot(h, wm_ref[:, sl]).astype(BF16)
    for j in range(W_MERGE // 512):
        sl = slice(j * 512, (j + 1) * 512)
        zg_ref[:, sl] = _dot(h, wg_ref[:, sl]).astype(BF16)
    gates_ref[...] = _dot(h, wgate_ref[...])


def _inproj(x, g, wm, wg, wgate, layer):
    n = x.shape[0]
    tm = ROW_TILE
    return pl.pallas_call(
        _inproj_kernel,
        grid=(n // tm,),
        in_specs=[
            pl.BlockSpec((tm, D_MODEL), lambda i: (i, 0)),
            _resident((1, D_MODEL), layer),
            _resident((D_MODEL, W_MAIN), layer),
            _resident((D_MODEL, W_MERGE), layer),
            _resident((D_MODEL, W_GATE), layer),
        ],
        out_specs=[
            pl.BlockSpec((tm, W_MAIN), lambda i: (i, 0)),
            pl.BlockSpec((tm, W_MERGE), lambda i: (i, 0)),
            pl.BlockSpec((tm, W_GATE), lambda i: (i, 0)),
        ],
        out_shape=[
            jax.ShapeDtypeStruct((n, W_MAIN), BF16),
            jax.ShapeDtypeStruct((n, W_MERGE), BF16),
            jax.ShapeDtypeStruct((n, W_GATE), F32),
        ],
        compiler_params=_params(("parallel",)),
        name="inproj",
    )(x, g, wm, wg, wgate)


def _prompt_init(ca0_ref, cb0_ref, c0_ref, n0_ref, m0_ref, xa_s, xb_s, c_s, n_s, m_s):
    xa_s[0:8, :] = jnp.zeros((8, D_A), F32)
    xb_s[0:32, :] = jnp.zeros((32, D_B), F32)
    xa_s[6:8, :] = ca0_ref[...]
    xb_s[2:32, :] = cb0_ref[...]
    c_s[...] = c0_ref[...]
    n_s[0:1, :] = n0_ref[...]
    m_s[0:1, :] = m0_ref[...]


def _prompt_chunk(zm_ref, gates_ref, caw_ref, cbw_ref, cbb_ref, lng_ref, lnb_ref, bif_ref, mng_ref,
                  p_ref, sa_ref, sb_ref, c_out_ref, n_out_ref, m_out_ref,
                  xa_s, xb_s, xbr_s, aconv_s, bconv_s, c_s, n_s, m_s):
    L = CHUNK

    xa_s[8:8 + L, :] = (zm_ref[:, OFF_AC:OFF_AC + D_A].astype(F32)
                        * zm_ref[:, OFF_AX:OFF_AX + D_A].astype(F32))
    xb_s[32:32 + L, :] = zm_ref[:, OFF_BV:OFF_BV + D_B].astype(F32) * jax.nn.sigmoid(
        zm_ref[:, OFF_BG:OFF_BG + D_B].astype(F32))
    for r in range(1, 8):
        xbr_s[r] = xb_s[r:r + 24 + L, :]

    def conv_block(cblk):
        cs = slice(cblk * 128, (cblk + 1) * 128)
        acc = caw_ref[0:1, cs] * xa_s[6:6 + L, cs]
        for k in range(1, K_A):
            acc = acc + caw_ref[k:k + 1, cs] * xa_s[6 + k:6 + k + L, cs]
        aconv_s[:, cs] = acc
        acc = None
        for k in range(K_B):
            a8, r = (2 + k) // 8 * 8, (2 + k) % 8
            src = xb_s[a8:a8 + L, cs] if r == 0 else xbr_s[r, a8:a8 + L, cs]
            term = cbw_ref[k:k + 1, cs] * src
            acc = term if acc is None else acc + term
        bconv_s[:, cs] = acc

    row = lax.broadcasted_iota(jnp.int32, (L, L), 0)
    col = lax.broadcasted_iota(jnp.int32, (L, L), 1)
    causal = col <= row
    g = gates_ref[...] + bif_ref[...]
    logf = jnp.minimum(g, 0.0) - jnp.log1p(jnp.exp(-jnp.abs(g)))
    bt = _dot_f32(causal.astype(F32), logf)
    bt_h = pltpu.roll(bt, 128 - H_C, axis=1)
    g_t = g.T
    bt_t = bt.T
    inter = bt_h + m_s[0:1, :]
    lane = lax.broadcasted_iota(jnp.int32, (L, W_GATE), 1)
    dlogs = []
    rmax = jnp.zeros((L, W_GATE), F32)
    for h in range(H_C):
        dlog = jnp.where(causal, bt[:, H_C + h:H_C + h + 1] - bt_t[H_C + h:H_C + h + 1, :]
                         + g_t[h:h + 1, :], -jnp.inf)
        dlogs.append(dlog)
        rmax = jnp.where(lane == h, jnp.max(dlog, axis=-1, keepdims=True), rmax)
    m_t = jnp.maximum(inter, rmax)
    w_inter = jnp.exp(inter - m_t)
    floor = jnp.exp(-m_t)
    m_new = m_t[L - 1:L, :]
    w_last = jnp.exp(bt_h[L - 1:L, :] - bt_h + g - m_new)
    w_prev = jnp.exp(inter[L - 1:L, :] - m_new)
    wl16 = w_last.T[0:16, :].astype(BF16)
    n_row = n_s[0:1, :]

    def head(h):
        hs = slice(h * DH_C, (h + 1) * DH_C)
        q = zm_ref[:, OFF_Q + h * DH_C:OFF_Q + (h + 1) * DH_C]
        k = zm_ref[:, OFF_K + h * DH_C:OFF_K + (h + 1) * DH_C]
        v = zm_ref[:, OFF_V + h * DH_C:OFF_V + (h + 1) * DH_C]
        c_old = c_s[h]
        s = _dot_nt(q, k) * (DH_C ** -0.5) * jnp.exp(dlogs[h] - m_t[:, h:h + 1])
        num = _dot(s.astype(BF16), v)
        qc = _dot_nt(q, c_old.astype(BF16))
        qn = _dot_nt(q, jnp.broadcast_to(n_row[:, hs], (L, DH_C)).astype(BF16))[:, 0:1]
        wi = w_inter[:, h:h + 1]
        den = jnp.sum(s, axis=-1, keepdims=True) + wi * qn
        rinv = 1.0 / jnp.maximum(jnp.abs(den), floor[:, h:h + 1])
        hh = (num + qc * wi) * rinv
        hn = hh * lax.rsqrt(jnp.mean(hh * hh, axis=-1, keepdims=True) + EPS) * mng_ref[:, hs]
        o = zm_ref[:, OFF_O + h * DH_C:OFF_O + (h + 1) * DH_C].astype(F32)
        p_ref[:, 2 * D_A + h * DH_C:2 * D_A + (h + 1) * DH_C] = (jax.nn.sigmoid(o) * hn).astype(BF16)
        vw = (v.astype(F32) * w_last[:, h:h + 1]).astype(BF16)
        wp = w_prev[:, h:h + 1]
        c_new = wp * c_old + _dot_tn(vw, k) * (DH_C ** -0.5)
        n_new = wp * n_row[:, hs] + _dot(wl16, k)[h:h + 1, :] * (DH_C ** -0.5)
        c_s[h] = c_new
        c_out_ref[h] = c_new
        n_s[0:1, hs] = n_new
        n_out_ref[h:h + 1, :] = n_new

    conv_block(0)
    conv_block(1)
    head(0)
    conv_block(2)
    head(1)
    conv_block(3)
    head(2)

    sa_new = xa_s[6 + L:8 + L, :]
    sb_new = xb_s[2 + L:32 + L, :]
    xa_s[6:8, :] = sa_new
    xb_s[2:32, :] = sb_new
    sa_ref[...] = sa_new
    sb_ref[...] = sb_new
    p_ref[:, 0:D_A] = (zm_ref[:, OFF_AB:OFF_AB + D_A].astype(F32) * aconv_s[...]).astype(BF16)
    bc = bconv_s[...] + cbb_ref[...]
    mu = jnp.mean(bc, axis=-1, keepdims=True)
    xc = bc - mu
    ln = xc * lax.rsqrt(jnp.mean(xc * xc, axis=-1, keepdims=True) + EPS) * lng_ref[...] + lnb_ref[...]
    p_ref[:, D_A:D_A + D_B] = (ln * jax.nn.sigmoid(ln)).astype(BF16)

    head(3)
    m_s[0:1, :] = m_new
    m_out_ref[...] = m_new


def _mixer_kernel(*refs, seq_rows, carry, has_acc):
    pair = PROMPT_PAIR if carry else 1
    zm_refs, gates_refs, refs = refs[:pair], refs[pair:2 * pair], refs[2 * pair:]
    (ca0_ref, cb0_ref, c0_ref, n0_ref, m0_ref,
     caw_ref, cbw_ref, cbb_ref, lng_ref, lnb_ref, bif_ref, mng_ref) = refs[:12]
    refs = refs[13:] if has_acc else refs[12:]
    p_ref, sa_ref, sb_ref, c_out_ref, n_out_ref, m_out_ref = refs[:6]
    zm_ref, gates_ref = zm_refs[0], gates_refs[0]
    L = CHUNK
    nseq = L // seq_rows
    t = pl.program_id(1)
    j = pl.program_id(2)

    if carry:
        assert nseq == 1
        xa_s, xb_s, xbr_s, aconv_s, bconv_s, c_s, n_s, m_s = refs[6:]

        @pl.when(t == 0)
        def _():
            for u in range(pair):
                _prompt_init(ca0_ref.at[0, u], cb0_ref.at[0, u], c0_ref.at[0, u], n0_ref.at[u], m0_ref.at[u],
                             xa_s.at[u], xb_s.at[u], c_s.at[u], n_s.at[u], m_s.at[u])

        def chunk(ci, _):
            rs = pl.ds(ci * L if isinstance(ci, int) else pl.multiple_of(ci * L, L), L)
            for u in range(pair):
                _prompt_chunk(zm_refs[u].at[rs], gates_refs[u].at[rs],
                              caw_ref, cbw_ref, cbb_ref, lng_ref, lnb_ref, bif_ref, mng_ref,
                              p_ref.at[u, rs], sa_ref.at[u], sb_ref.at[u], c_out_ref.at[0, u], n_out_ref.at[u],
                              m_out_ref.at[u], xa_s.at[u], xb_s.at[u], xbr_s.at[u], aconv_s.at[u],
                              bconv_s.at[u], c_s.at[u], n_s.at[u], m_s.at[u])
            return 0

        n_inner = zm_ref.shape[0] // L
        if n_inner == 1:
            chunk(0, 0)
        else:
            lax.fori_loop(0, n_inner, chunk, 0)
        return

    aconv_s, bconv_s, num_s, qc_s, vw_s, winter_s, rinv_s, wlt_s, wprev_s, mnew_s = refs[6:]

    @pl.when(j == 0)
    def _pre():
        a_b = zm_ref[:, OFF_AB:OFF_AB + D_A].astype(F32)
        ca = zm_ref[:, OFF_AC:OFF_AC + D_A].astype(F32) * zm_ref[:, OFF_AX:OFF_AX + D_A].astype(F32)
        cb = zm_ref[:, OFF_BV:OFF_BV + D_B].astype(F32) * jax.nn.sigmoid(
            zm_ref[:, OFF_BG:OFF_BG + D_B].astype(F32))

        r_i = lax.broadcasted_iota(jnp.int32, (L, L), 0)
        c_i = lax.broadcasted_iota(jnp.int32, (L, L), 1)
        to_tok = (c_i == (r_i % nseq) * seq_rows + r_i // nseq).astype(F32)
        to_seq = (c_i == (r_i % seq_rows) * nseq + r_i // seq_rows).astype(F32)

        def short_conv(x, st_ref, w_ref, taps, new_ref):
            x_tok = _dot_f32(to_tok, x)
            window = [st_ref[i] for i in range(taps - 1)]
            window += [x_tok[tk * nseq:(tk + 1) * nseq] for tk in range(seq_rows)]
            outs_tok = []
            for tk in range(seq_rows):
                acc = w_ref[0:1, :] * window[tk]
                for k in range(1, taps):
                    acc = acc + w_ref[k:k + 1, :] * window[tk + k]
                outs_tok.append(acc)
            for i in range(taps - 1):
                new_ref[i] = window[seq_rows + i]
            return _dot_f32(to_seq, jnp.concatenate(outs_tok, axis=0))

        aconv_s[...] = short_conv(ca, ca0_ref, caw_ref, K_A, sa_ref)
        bconv_s[...] = short_conv(cb, cb0_ref, cbw_ref, K_B, sb_ref)

        p_ref[:, 0:D_A] = (a_b * aconv_s[...]).astype(BF16)
        bc = bconv_s[...] + cbb_ref[...]
        mu = jnp.mean(bc, axis=-1, keepdims=True)
        xc = bc - mu
        ln = xc * lax.rsqrt(jnp.mean(xc * xc, axis=-1, keepdims=True) + EPS) * lng_ref[...] + lnb_ref[...]
        p_ref[:, D_A:D_A + D_B] = (ln * jax.nn.sigmoid(ln)).astype(BF16)

        row = lax.broadcasted_iota(jnp.int32, (L, L), 0)
        col = lax.broadcasted_iota(jnp.int32, (L, L), 1)
        same = (row // seq_rows) == (col // seq_rows)
        causal = same & (col <= row)
        g = gates_ref[...] + bif_ref[...]
        logf = jnp.minimum(g, 0.0) - jnp.log1p(jnp.exp(-jnp.abs(g)))
        bt = _dot_f32(causal.astype(F32), logf)
        bt_h = pltpu.roll(bt, 128 - H_C, axis=1)
        btl_h = pltpu.roll(_dot_f32(same.astype(F32), logf), 128 - H_C, axis=1)
        g_t = g.T
        bt_t = bt.T
        inter = bt_h + m0_ref[0]
        lane = lax.broadcasted_iota(jnp.int32, (L, W_GATE), 1)

        dlogs = []
        rmax = jnp.zeros((L, W_GATE), F32)
        for h in range(H_C):
            dlog = jnp.where(causal, bt[:, H_C + h:H_C + h + 1] - bt_t[H_C + h:H_C + h + 1, :]
                             + g_t[h:h + 1, :], -jnp.inf)
            dlogs.append(dlog)
            rmax = jnp.where(lane == h, jnp.max(dlog, axis=-1, keepdims=True), rmax)
        m_t = jnp.maximum(inter, rmax)
        w_inter = jnp.exp(inter - m_t)
        floor = jnp.exp(-m_t)
        last = (col == (row // seq_rows) * seq_rows + (seq_rows - 1)).astype(F32)
        m_new = _dot_f32(last, m_t)
        w_last = jnp.exp(btl_h - bt_h + g - m_new)
        w_prev = jnp.exp(inter - m_new)
        n_rows = n0_ref[0]

        den = jnp.zeros((L, W_GATE), F32)
        for h in range(H_C):
            hs = slice(h * DH_C, (h + 1) * DH_C)
            q = zm_ref[:, OFF_Q + h * DH_C:OFF_Q + (h + 1) * DH_C]
            k = zm_ref[:, OFF_K + h * DH_C:OFF_K + (h + 1) * DH_C]
            v = zm_ref[:, OFF_V + h * DH_C:OFF_V + (h + 1) * DH_C]
            s = _dot_nt(q, k) * (DH_C ** -0.5) * jnp.exp(dlogs[h] - m_t[:, h:h + 1])
            num_s[:, hs] = _dot(s.astype(BF16), v)
            qn_all = _dot_nt(q, n_rows[:, hs].astype(BF16))
            qn = jnp.sum(jnp.where(row == col, qn_all, 0.0), axis=-1, keepdims=True)
            den_h = jnp.sum(s, axis=-1, keepdims=True) + w_inter[:, h:h + 1] * qn
            den = jnp.where(lane == h, den_h, den)
            vw_s[h] = (v.astype(F32) * w_last[:, h:h + 1]).astype(BF16)
        qc_s[...] = jnp.zeros(qc_s.shape, F32)
        winter_s[...] = w_inter
        rinv_s[...] = 1.0 / jnp.maximum(jnp.abs(den), floor)
        wlt_s[...] = w_last.T[0:16, :]
        wprev_s[...] = w_prev
        mnew_s[...] = m_new

    spp = c0_ref.shape[1]
    tile = spp * seq_rows
    r0 = pl.multiple_of(j * tile, tile)
    rid = lax.broadcasted_iota(jnp.int32, (tile, 1), 0)
    lane_l = lax.broadcasted_iota(jnp.int32, (1, L), 1)
    row_l = lax.broadcasted_iota(jnp.int32, (L, 1), 0)
    for u in range(spp):
        sq = j * spp + u
        last_row = sq * seq_rows + (seq_rows - 1)
        seq_lanes = (lane_l // seq_rows) == sq
        seq_rows_mask = (row_l // seq_rows) == sq
        wprev_row = wprev_s[pl.ds(last_row, 1), :]
        for h in range(H_C):
            hs = slice(h * DH_C, (h + 1) * DH_C)
            c_old = c0_ref[0, u, h]
            q16 = zm_ref[pl.ds(r0, tile), OFF_Q + h * DH_C:OFF_Q + (h + 1) * DH_C]
            r = _dot_nt(q16, c_old.astype(BF16))
            qc_s[pl.ds(r0, tile), hs] = jnp.where((rid // seq_rows) == u, r, qc_s[pl.ds(r0, tile), hs])
            k = zm_ref[:, OFF_K + h * DH_C:OFF_K + (h + 1) * DH_C]
            vw = jnp.where(seq_rows_mask, vw_s[h], jnp.zeros((L, DH_C), BF16))
            wl = jnp.where(seq_lanes, wlt_s[...], 0.0)
            w_prev = wprev_row[:, h:h + 1]
            c_out_ref[0, u, h] = w_prev * c_old + _dot_tn(vw, k) * (DH_C ** -0.5)
            ksum = _dot(wl.astype(BF16), k)[h:h + 1, :]
            n_old = n0_ref[0, pl.ds(sq * seq_rows, 1), hs]
            n_out_ref[u, h:h + 1, :] = w_prev * n_old + ksum * (DH_C ** -0.5)
        m_out_ref[u] = mnew_s[pl.ds(last_row, 1), :]

    @pl.when(j == nseq // spp - 1)
    def _post():
        for h in range(H_C):
            hs = slice(h * DH_C, (h + 1) * DH_C)
            hh = (num_s[:, hs] + qc_s[:, hs] * winter_s[:, h:h + 1]) * rinv_s[:, h:h + 1]
            hn = hh * lax.rsqrt(jnp.mean(hh * hh, axis=-1, keepdims=True) + EPS) * mng_ref[:, hs]
            o = zm_ref[:, OFF_O + h * DH_C:OFF_O + (h + 1) * DH_C].astype(F32)
            p_ref[:, 2 * D_A + h * DH_C:2 * D_A + (h + 1) * DH_C] = (jax.nn.sigmoid(o) * hn).astype(BF16)


def _mixer(zm, gates, ca0, cb0, c0, n0, m0, caw, cbw, cbb, lng, lnb, bif, mng, c_acc, *,
           row_block0, n_groups, n_chunks, seq_rows, carry, layer, layer_in, layer_out):
    L = CHUNK
    nseq = L // seq_rows
    n_state = n_groups if carry else n_chunks * nseq
    rows = n_groups * n_chunks * L

    if carry:
        pair = PROMPT_PAIR
        spp = pair

        def seq_of(b, t, j):
            return b

        def conv_of(b, t, j):
            return b
        rows_blk = 1

        cps = PROMPT_CHUNKS_PER_STEP
        assert n_chunks % cps == 0 and row_block0 % cps == 0
        tok_rows = cps * L

        def tok(u):
            return lambda b, t, j: (row_block0 // cps + (b * pair + u) * (n_chunks // cps) + t, 0)

        def conv_in(taps, ch):
            return pl.BlockSpec((1, pair, taps - 1, ch), lambda *g: (layer_in, conv_of(*g), 0, 0))

        def conv_out(taps, ch):
            return (pl.BlockSpec((pair, taps - 1, ch), lambda *g: (conv_of(*g), 0, 0)),
                    jax.ShapeDtypeStruct((n_state, taps - 1, ch), F32))
        p_spec = pl.BlockSpec((pair, tok_rows, W_P), lambda b, t, j: (b, t, 0))
        p_shape = jax.ShapeDtypeStruct((n_groups, n_chunks * L, W_P), BF16)
        grid = (n_groups // pair, n_chunks // cps, 1)
        lead = (pair,)
    else:
        pair = 1
        tok_rows = L
        spp = SAMPLE_ROWS_PER_STEP // seq_rows

        def tok(u):
            return lambda b, t, j: (row_block0 + b * n_chunks + t, 0)
        p_spec = pl.BlockSpec((L, W_P), lambda b, t, j: (b * n_chunks + t, 0))
        p_shape = jax.ShapeDtypeStruct((rows, W_P), BF16)
        grid = (n_groups, n_chunks, nseq // spp)
        lead = ()

        def seq_of(b, t, j):
            return t * (nseq // spp) + j

        def conv_of(b, t, j):
            return t
        rows_blk = L

        def conv_in(taps, ch):
            return pl.BlockSpec((None, taps - 1, nseq, ch), lambda *g: (layer_in, 0, conv_of(*g), 0))

        def conv_out(taps, ch):
            return (pl.BlockSpec((taps - 1, nseq, ch), lambda *g: (0, conv_of(*g), 0)),
                    jax.ShapeDtypeStruct((taps - 1, n_state, ch), F32))

    has_acc = c_acc is not None
    kern = functools.partial(_mixer_kernel, seq_rows=seq_rows, carry=carry, has_acc=has_acc)
    in_specs = [pl.BlockSpec((tok_rows, W_MAIN), tok(u)) for u in range(pair)]
    in_specs += [pl.BlockSpec((tok_rows, W_GATE), tok(u)) for u in range(pair)]
    in_specs += [
        conv_in(K_A, D_A),
        conv_in(K_B, D_B),
        pl.BlockSpec((1, spp, H_C, DH_C, DH_C), lambda *g: (layer_in, seq_of(*g), 0, 0, 0)),
        pl.BlockSpec((None, pair, rows_blk, D_MODEL), lambda *g: (layer_in, conv_of(*g), 0, 0)),
        pl.BlockSpec((None, pair, rows_blk, W_GATE), lambda *g: (layer_in, conv_of(*g), 0, 0)),
        _resident((K_A, D_A), layer),
        _resident((K_B, D_B), layer),
        _resident((1, D_B), layer),
        _resident((1, D_B), layer),
        _resident((1, D_B), layer),
        _resident((1, W_GATE), layer),
        _resident((1, D_MODEL), layer),
    ]
    args = [zm] * pair + [gates] * pair + [ca0, cb0, c0, n0, m0, caw, cbw, cbb, lng, lnb, bif, mng]
    aliases = {}
    if has_acc:
        in_specs.append(pl.BlockSpec(memory_space=pl.ANY))
        args.append(c_acc)
        aliases = {len(args) - 1: 3}
    outs = pl.pallas_call(
        kern,
        grid=grid,
        in_specs=in_specs,
        out_specs=[
            p_spec,
            conv_out(K_A, D_A)[0],
            conv_out(K_B, D_B)[0],
            pl.BlockSpec((1, spp, H_C, DH_C, DH_C), lambda *g: (layer_out, seq_of(*g), 0, 0, 0)),
            pl.BlockSpec((spp, H_C, DH_C), lambda *g: (seq_of(*g), 0, 0)),
            pl.BlockSpec((spp, 1, W_GATE), lambda *g: (seq_of(*g), 0, 0)),
        ],
        out_shape=[
            p_shape,
            conv_out(K_A, D_A)[1],
            conv_out(K_B, D_B)[1],
            jax.ShapeDtypeStruct((DEPTH, n_state, H_C, DH_C, DH_C), F32),
            jax.ShapeDtypeStruct((n_state, H_C, DH_C), F32),
            jax.ShapeDtypeStruct((n_state, 1, W_GATE), F32),
        ],
        input_output_aliases=aliases,
        scratch_shapes=[
            pltpu.VMEM(lead + (8 + L, D_A), F32),
            pltpu.VMEM(lead + (32 + L, D_B), F32),
            pltpu.VMEM(lead + (8, 24 + L, D_B), F32),
            pltpu.VMEM(lead + (L, D_A), F32),
            pltpu.VMEM(lead + (L, D_B), F32),
            pltpu.VMEM(lead + (H_C, DH_C, DH_C), F32),
            pltpu.VMEM(lead + (8, D_MODEL), F32),
            pltpu.VMEM(lead + (8, W_GATE), F32),
        ] if carry else [
            pltpu.VMEM((L, D_A), F32),
            pltpu.VMEM((L, D_B), F32),
            pltpu.VMEM((L, D_MODEL), F32),
            pltpu.VMEM((L, D_MODEL), F32),
            pltpu.VMEM((H_C, L, DH_C), BF16),
            pltpu.VMEM((L, W_GATE), F32),
            pltpu.VMEM((L, W_GATE), F32),
            pltpu.VMEM((16, L), F32),
            pltpu.VMEM((L, W_GATE), F32),
            pltpu.VMEM((L, W_GATE), F32),
        ],
        compiler_params=_params(("arbitrary", "arbitrary", "arbitrary")),
        name="mixer_prompt" if carry else "mixer_sample",
    )(*args)
    outs = list(outs)
    outs[0] = outs[0].reshape(rows, W_P)
    return outs


def _outproj_kernel(x_ref, pp_ref, ps_ref, zg_ref, wa_ref, wb_ref, wc_ref, wo_ref, gx_ref, wq_ref,
                    x1_ref, q_ref, *, n_prompt_tiles):
    i = pl.program_id(0)
    p = jnp.where(i >= n_prompt_tiles, ps_ref[...], pp_ref[...])
    y_a = _dot(p[:, 0:D_A], wa_ref[...])
    y_b = _dot(p[:, D_A:D_A + D_B], wb_ref[...])
    y_c = _dot(p[:, D_A + D_B:], wc_ref[...])
    u = (jax.nn.sigmoid(zg_ref[:, 0:D_MODEL].astype(F32)) * y_a
         + jax.nn.sigmoid(zg_ref[:, D_MODEL:2 * D_MODEL].astype(F32)) * y_b
         + jax.nn.sigmoid(zg_ref[:, 2 * D_MODEL:].astype(F32)) * y_c)
    x1 = x_ref[...] + _dot(u.astype(BF16), wo_ref[...])
    x1_ref[...] = x1
    q_ref[...] = _dot(_rmsnorm(x1, gx_ref[...]).astype(BF16), wq_ref[...]).astype(BF16)


def _outproj(x, p_p, p_s, zg, wa, wb, wc, wo, gx, wq, layer):
    n = x.shape[0]
    tm = ROW_TILE
    npt = p_p.shape[0] // tm
    return pl.pallas_call(
        functools.partial(_outproj_kernel, n_prompt_tiles=npt),
        grid=(n // tm,),
        in_specs=[
            pl.BlockSpec((tm, D_MODEL), lambda i: (i, 0)),
            pl.BlockSpec((tm, W_P), lambda i: (jnp.minimum(i, npt - 1), 0)),
            pl.BlockSpec((tm, W_P), lambda i: (0, 0)),
            pl.BlockSpec((tm, W_MERGE), lambda i: (i, 0)),
            _resident((D_A, D_MODEL), layer),
            _resident((D_B, D_MODEL), layer),
            _resident((D_MODEL, D_MODEL), layer),
            _resident((D_MODEL, D_MODEL), layer),
            _resident((1, D_MODEL), layer),
            _resident((D_MODEL, D_MODEL), layer),
        ],
        out_specs=[
            pl.BlockSpec((tm, D_MODEL), lambda i: (i, 0)),
            pl.BlockSpec((tm, D_MODEL), lambda i: (i, 0)),
        ],
        out_shape=[
            jax.ShapeDtypeStruct((n, D_MODEL), F32),
            jax.ShapeDtypeStruct((n, D_MODEL), BF16),
        ],
        compiler_params=_params(("parallel",)),
        name="outproj",
    )(x, p_p, p_s, zg, wa, wb, wc, wo, gx, wq)


def _memkv_kernel(mem_ref, g_ref, w_ref, k_ref, v_ref, kb_ref, vb_ref):
    h = _rmsnorm(mem_ref[...], g_ref[0]).astype(BF16)
    hd = HX * DX
    k = _dot(h, w_ref[0, :, 0:hd])
    v = _dot(h, w_ref[0, :, hd:])
    for bb in range(k_ref.shape[1]):
        k_ref[0, bb] = k[bb * N_MEM:(bb + 1) * N_MEM].reshape(N_MEM, HX, DX)
        v_ref[0, bb] = v[bb * N_MEM:(bb + 1) * N_MEM].reshape(N_MEM, HX, DX)
    kb_ref[0] = k.astype(BF16)
    vb_ref[0] = v.astype(BF16)


def _memkv(mem, g, w):
    n = mem.shape[0]
    tm = ROW_TILE
    hd = HX * DX
    per = tm // N_MEM
    o_spec = pl.BlockSpec((1, tm, hd), lambda l, i: (l, i, 0))
    o5_spec = pl.BlockSpec((1, per, N_MEM, HX, DX), lambda l, i: (l, i, 0, 0, 0))
    return pl.pallas_call(
        _memkv_kernel,
        grid=(DEPTH, n // tm),
        in_specs=[
            pl.BlockSpec((tm, D_MODEL), lambda l, i: (i, 0)),
            pl.BlockSpec((1, 1, D_MODEL), lambda l, i: (l, 0, 0)),
            pl.BlockSpec((1, D_MODEL, 2 * hd), lambda l, i: (l, 0, 0)),
        ],
        out_specs=[o5_spec, o5_spec, o_spec, o_spec],
        out_shape=[
            jax.ShapeDtypeStruct((DEPTH, n // N_MEM, N_MEM, HX, DX), F32),
            jax.ShapeDtypeStruct((DEPTH, n // N_MEM, N_MEM, HX, DX), F32),
            jax.ShapeDtypeStruct((DEPTH, n, hd), BF16),
            jax.ShapeDtypeStruct((DEPTH, n, hd), BF16),
        ],
        compiler_params=_params(("arbitrary", "arbitrary")),
        name="memkv",
    )(mem, g, w)


def _attend(q, k, v):
    s = _dot_nt(q, k) * (DX ** -0.5)
    e = jnp.exp(s - jnp.max(s, axis=-1, keepdims=True))
    return _dot(e.astype(BF16), v) * (1.0 / jnp.sum(e, axis=-1, keepdims=True))


XS_ROWS = 32
XATTN_TILE = 1024


def _xattn_kernel(q_ref, k_ref, v_ref, qs_ref, ks_ref, vs_ref, o_ref, os_ref, *, seq_rows):
    for h in range(HX):
        hs = slice(h * DX, (h + 1) * DX)
        o_ref[:, hs] = _attend(q_ref[:, hs], k_ref[0, :, hs], v_ref[0, :, hs]).astype(BF16)
    _xattn_short(qs_ref, ks_ref, vs_ref, os_ref, seq_rows)


def _xattn(q, kb, vb, k_cache, v_cache, layer, *, n_groups, rows_per_group, seq_rows):
    tq = XATTN_TILE
    nt = rows_per_group // tq
    hd = HX * DX
    per = XS_ROWS // seq_rows
    n_seq = k_cache.shape[1]
    assert n_groups * nt * per == n_seq
    qs_block0 = n_groups * rows_per_group // XS_ROWS
    kv_spec = pl.BlockSpec((1, N_MEM, hd), lambda b, t: (layer, b, 0))
    cache_spec = pl.BlockSpec((1, per, N_MEM, HX, DX), lambda b, t: (layer, b * nt + t, 0, 0, 0))
    return pl.pallas_call(
        functools.partial(_xattn_kernel, seq_rows=seq_rows),
        grid=(n_groups, nt),
        in_specs=[pl.BlockSpec((tq, hd), lambda b, t: (b * nt + t, 0)), kv_spec, kv_spec,
                  pl.BlockSpec((XS_ROWS, hd), lambda b, t: (qs_block0 + b * nt + t, 0)),
                  cache_spec, cache_spec],
        out_specs=[pl.BlockSpec((tq, hd), lambda b, t: (b * nt + t, 0)),
                   pl.BlockSpec((XS_ROWS, hd), lambda b, t: (b * nt + t, 0))],
        out_shape=[jax.ShapeDtypeStruct((n_groups * rows_per_group, hd), BF16),
                   jax.ShapeDtypeStruct((n_seq * seq_rows, hd), BF16)],
        compiler_params=_params(("parallel", "parallel")),
        name="xattn",
    )(q, kb, vb, q, k_cache, v_cache)


def _xattn_short(q_ref, k_ref, v_ref, o_ref, seq_rows):
    rows = q_ref.shape[0]
    qs = jnp.concatenate([q_ref[:, h * DX:(h + 1) * DX] for h in range(HX)], axis=0)
    rid = lax.broadcasted_iota(jnp.int32, (HX * rows, 1), 0)
    cid = lax.broadcasted_iota(jnp.int32, (HX * rows, N_MEM * HX), 1)
    own_head = (cid % HX) == (rid // rows)
    acc = jnp.zeros((HX * rows, DX), F32)
    for e in range(rows // seq_rows):
        k2 = k_ref[0, e].reshape(N_MEM * HX, DX).astype(BF16)
        v2 = v_ref[0, e].reshape(N_MEM * HX, DX).astype(BF16)
        s = jnp.where(own_head, _dot_nt(qs, k2) * (DX ** -0.5), -jnp.inf)
        p = jnp.exp(s - jnp.max(s, axis=-1, keepdims=True))
        o = _dot(p.astype(BF16), v2) * (1.0 / jnp.sum(p, axis=-1, keepdims=True))
        acc = jnp.where(((rid % rows) // seq_rows) == e, o, acc)
    for h in range(HX):
        o_ref[:, h * DX:(h + 1) * DX] = acc[h * rows:(h + 1) * rows].astype(BF16)


FF_BLOCK = 256


def _ffn_kernel(x_ref, cp_ref, cs_ref, wxo_ref, gf_ref, wg_ref, wu_ref, wout_ref, gfin_ref,
                *rest, n_prompt_tiles, final):
    act_s = rest[-1]
    i = pl.program_id(0)
    ctx = jnp.where(i >= n_prompt_tiles, cs_ref[...], cp_ref[...])
    x2 = x_ref[...] + _dot(ctx, wxo_ref[...])
    h = _rmsnorm(x2, gf_ref[...]).astype(BF16)
    for jb in range(D_FF // FF_BLOCK):
        sl = slice(jb * FF_BLOCK, (jb + 1) * FF_BLOCK)
        gate = _dot(h, wg_ref[:, sl])
        up = _dot(h, wu_ref[:, sl])
        act_s[:, sl] = (gate * jax.nn.sigmoid(gate) * up).astype(BF16)
    x3 = x2 + _dot(act_s[...], wout_ref[...])
    if final:
        y = _rmsnorm(x3, gfin_ref[...])
        yp_ref, ys_ref = rest[0], rest[1]

        @pl.when(i < n_prompt_tiles)
        def _():
            yp_ref[...] = y

        @pl.when(i >= n_prompt_tiles)
        def _():
            ys_ref[...] = y
    else:
        rest[0][...] = x3


def _ffn(x, c_p, c_s, wxo, gf, w_in, wout, gfin, layer, *, final):
    n = x.shape[0]
    tm = ROW_TILE
    npt = c_p.shape[0] // tm
    if final:
        assert n - npt * tm == tm
        out_specs = [pl.BlockSpec((tm, D_MODEL), lambda i: (jnp.minimum(i, npt - 1), 0)),
                     pl.BlockSpec((tm, D_MODEL), lambda i: (0, 0))]
        out_shape = [jax.ShapeDtypeStruct((npt * tm, D_MODEL), F32),
                     jax.ShapeDtypeStruct((tm, D_MODEL), F32)]
        sem = ("arbitrary",)
    else:
        out_specs = pl.BlockSpec((tm, D_MODEL), lambda i: (i, 0))
        out_shape = jax.ShapeDtypeStruct((n, D_MODEL), F32)
        sem = ("parallel",)
    return pl.pallas_call(
        functools.partial(_ffn_kernel, n_prompt_tiles=npt, final=final),
        grid=(n // tm,),
        in_specs=[
            pl.BlockSpec((tm, D_MODEL), lambda i: (i, 0)),
            pl.BlockSpec((tm, D_MODEL), lambda i: (jnp.minimum(i, npt - 1), 0)),
            pl.BlockSpec((tm, D_MODEL), lambda i: (0, 0)),
            _resident((D_MODEL, D_MODEL), layer),
            _resident((1, D_MODEL), layer),
            _resident((D_MODEL, D_FF), layer, col=0),
            _resident((D_MODEL, D_FF), layer, col=1),
            _resident((D_FF, D_MODEL), layer),
            _resident((1, D_MODEL)),
        ],
        out_specs=out_specs,
        out_shape=out_shape,
        scratch_shapes=[pltpu.VMEM((tm, D_FF), BF16)],
        compiler_params=_params(sem),
        name="ffn",
    )(x, c_p, c_s, wxo, gf, w_in, w_in, wout, gfin)


def kernel(x_prompt, x_sample, state_conv_a, state_conv_b, state_mlstm_c, state_mlstm_n, state_mlstm_m,
           cache_mem_k, cache_mem_v, mem_prompt, norm_mix_g, w_in, b_if, conv_a_w, w_out_a, conv_b_w,
           conv_b_b, ln_b_g, ln_b_b, w_out_b, mlstm_norm_g, w_out_c, w_o, norm_x_g, norm_mem_g, w_xq,
           w_xkv, w_xo, norm_ffn_g, w_ffn_in, w_ffn_out, final_norm_g):
    bp, tp, d = x_prompt.shape
    bs, ts, _ = x_sample.shape
    n_p = bp * tp
    n_s = bs * ts
    hd = HX * DX

    gate_lo = W_MAIN
    gate_hi = W_MAIN + 2 * H_C
    w_inb = w_in.astype(BF16)
    w_gate = jnp.pad(w_inb[:, :, gate_lo:gate_hi], ((0, 0), (0, 0), (0, W_GATE - 2 * H_C)))
    w_merge = w_inb[:, :, gate_hi:]
    bif = jnp.pad(b_if, ((0, 0), (0, W_GATE - 2 * H_C)))[:, None, :]
    w_a = w_out_a.astype(BF16)
    w_b = w_out_b.astype(BF16)
    w_c = w_out_c.astype(BF16)
    w_ob = w_o.astype(BF16)
    w_q = w_xq.astype(BF16)
    w_kv = w_xkv.astype(BF16)
    w_xob = w_xo.astype(BF16)
    w_fi = w_ffn_in.astype(BF16)
    w_fo = w_ffn_out.astype(BF16)
    g_mix = norm_mix_g[:, None, :]
    g_x = norm_x_g[:, None, :]
    g_ffn = norm_ffn_g[:, None, :]
    small = (conv_a_w, conv_b_w, conv_b_b[:, None, :], ln_b_g[:, None, :], ln_b_b[:, None, :],
             bif, mlstm_norm_g[:, None, :])

    x = jnp.concatenate([x_prompt.reshape(n_p, d), x_sample.reshape(n_s, d)], axis=0)

    mem_k, mem_v, mem_kb, mem_vb = _memkv(mem_prompt.reshape(bp * N_MEM, d), norm_mem_g[:, None, :], w_kv)

    zeros_ca = jnp.zeros((1, bp, K_A - 1, D_A), F32)
    zeros_cb = jnp.zeros((1, bp, K_B - 1, D_B), F32)
    zeros_c = jnp.zeros((1, bp, H_C, DH_C, DH_C), F32)
    zeros_n = jnp.zeros((1, bp, 1, H_C * DH_C), F32)
    zeros_m = jnp.zeros((1, bp, 1, W_GATE), F32)

    n_chunks_s = n_s // CHUNK
    n0_rows = jnp.repeat(state_mlstm_n.reshape(DEPTH, bs, H_C * DH_C), ts, axis=1)
    n0_rows = n0_rows.reshape(DEPTH, n_chunks_s, CHUNK, H_C * DH_C)
    m0_rows = jnp.repeat(jnp.pad(state_mlstm_m, ((0, 0), (0, 0), (0, W_GATE - H_C))), ts, axis=1)
    m0_rows = m0_rows.reshape(DEPTH, n_chunks_s, CHUNK, W_GATE)

    conv_a_tm = jnp.transpose(state_conv_a, (0, 2, 1, 3))
    conv_b_tm = jnp.transpose(state_conv_b, (0, 2, 1, 3))

    outs = {k: [] for k in ("pa", "pb", "pn", "pm", "sa", "sb", "sn", "sm")}
    c1 = c2 = None
    for l in range(DEPTH):
        zm, zg, gates = _inproj(x, g_mix, w_inb, w_merge, w_gate, l)
        p_p, a1, b1, c1, n1, m1 = _mixer(
            zm, gates, zeros_ca, zeros_cb, zeros_c, zeros_n, zeros_m, *small, c1,
            row_block0=0, n_groups=bp, n_chunks=tp // CHUNK, seq_rows=CHUNK, carry=True,
            layer=l, layer_in=0, layer_out=l)
        p_s, a2, b2, c2, n2, m2 = _mixer(
            zm, gates, conv_a_tm, conv_b_tm, state_mlstm_c, n0_rows, m0_rows, *small, c2,
            row_block0=n_p // CHUNK, n_groups=1, n_chunks=n_chunks_s, seq_rows=ts, carry=False,
            layer=l, layer_in=l, layer_out=l)
        x1, qx = _outproj(x, p_p, p_s, zg, w_a, w_b, w_c, w_ob, g_x, w_q, l)
        c_p, c_s = _xattn(qx, mem_kb, mem_vb, cache_mem_k, cache_mem_v, l,
                          n_groups=bp, rows_per_group=tp, seq_rows=ts)
        x = _ffn(x1, c_p, c_s, w_xob, g_ffn, w_fi, w_fo, final_norm_g[None, :], l, final=(l == DEPTH - 1))
        outs["pa"].append(a1); outs["pb"].append(b1); outs["pn"].append(n1); outs["pm"].append(m1)
        outs["sa"].append(a2); outs["sb"].append(b2); outs["sn"].append(n2); outs["sm"].append(m2)

    y_prompt = x[0].reshape(bp, tp, d)
    y_sample = x[1].reshape(bs, ts, d)
    st = {k: jnp.stack(v) for k, v in outs.items()}
    return (y_prompt, y_sample,
            st["pa"], st["pb"], c1, st["pn"], st["pm"][:, :, 0, :H_C],
            mem_k, mem_v,
            jnp.transpose(st["sa"], (0, 2, 1, 3)), jnp.transpose(st["sb"], (0, 2, 1, 3)), c2,
            st["sn"], st["sm"][:, :, 0, :H_C])
```

```python
import functools

import jax
import jax.numpy as jnp
from jax import lax
from jax.experimental import pallas as pl
from jax.experimental.pallas import tpu as pltpu

D_MODEL = 1024
DEPTH = 4
D_A = 512
K_A = 3
D_B = 512
K_B = 31
H_C = 4
DH_C = 256
N_MEM = 256
HX = 4
DX = 256
D_FF = 2816
EPS = 1e-6

OFF_AB, OFF_AC, OFF_AX, OFF_BV, OFF_BG = 0, 512, 1024, 1536, 2048
OFF_Q, OFF_K, OFF_V, OFF_O = 2560, 3584, 4608, 5632
W_MAIN = 6656
W_MERGE = 3 * D_MODEL
W_GATE = 128
W_P = 2 * D_A + D_MODEL

CHUNK = 128
SAMPLE_ROWS_PER_STEP = 32
PROMPT_PAIR = 1
PROMPT_CHUNKS_PER_STEP = 1
ROW_TILE = 512
VMEM_LIMIT_BYTES = 56 * 1024 * 1024

F32 = jnp.float32
BF16 = jnp.bfloat16


def _dot(a, b):
    return jnp.dot(a, b, preferred_element_type=F32)


def _dot_nt(a, b):
    return lax.dot_general(a, b, (((1,), (1,)), ((), ())), preferred_element_type=F32)


def _dot_tn(a, b):
    return lax.dot_general(a, b, (((0,), (0,)), ((), ())), preferred_element_type=F32)


def _dot_f32(a, b):
    return jnp.dot(a, b, preferred_element_type=F32, precision=lax.Precision.HIGHEST)


def _rmsnorm(x, g):
    ms = jnp.mean(x * x, axis=-1, keepdims=True)
    return x * lax.rsqrt(ms + EPS) * g


def _resident(shape, layer=None, col=0):
    nd = len(shape)
    if layer is None:
        return pl.BlockSpec(shape, lambda *_: (0,) * nd, pipeline_mode=pl.Buffered(1))
    return pl.BlockSpec((None,) + shape, lambda *_: (layer,) + (0,) * (nd - 1) + (col,),
                        pipeline_mode=pl.Buffered(1))


def _params(sem):
    return pltpu.CompilerParams(dimension_semantics=sem, vmem_limit_bytes=VMEM_LIMIT_BYTES)


def _x_specs(tm, npt, sample_block):
    return [pl.BlockSpec((tm, D_MODEL), lambda i: (jnp.minimum(i, npt - 1), 0)),
            pl.BlockSpec((tm, D_MODEL), lambda i: (sample_block, 0))]


def _inproj_kernel(xa_ref, xb_ref, g_ref, wm_ref, wg_ref, wgate_ref, zm_ref, zg_ref, gates_ref, *, n_prompt_tiles):
    x = jnp.where(pl.program_id(0) >= n_prompt_tiles, xb_ref[...], xa_ref[...])
    h = _rmsnorm(x, g_ref[...]).astype(BF16)
    for j in range(W_MAIN // 512):
        sl = slice(j * 512, (j + 1) * 512)
        zm_ref[:, sl] = _dot(h, wm_ref[:, sl]).astype(BF16)
    for j in range(W_MERGE // 512):
        sl = slice(j * 512, (j + 1) * 512)
        zg_ref[:, sl] = _dot(h, wg_ref[:, sl]).astype(BF16)
    gates_ref[...] = _dot(h, wgate_ref[...])


def _inproj(xa, xb, sample_block, g, wm, wg, wgate, layer):
    tm = ROW_TILE
    npt = sample_block if xa is xb else xa.shape[0] // tm
    n = (npt + 1) * tm
    return pl.pallas_call(
        functools.partial(_inproj_kernel, n_prompt_tiles=npt),
        grid=(n // tm,),
        in_specs=_x_specs(tm, npt, sample_block) + [
            _resident((1, D_MODEL), layer),
            _resident((D_MODEL, W_MAIN), layer),
            _resident((D_MODEL, W_MERGE), layer),
            _resident((D_MODEL, W_GATE), layer),
        ],
        out_specs=[
            pl.BlockSpec((tm, W_MAIN), lambda i: (i, 0)),
            pl.BlockSpec((tm, W_MERGE), lambda i: (i, 0)),
            pl.BlockSpec((tm, W_GATE), lambda i: (i, 0)),
        ],
        out_shape=[
            jax.ShapeDtypeStruct((n, W_MAIN), BF16),
            jax.ShapeDtypeStruct((n, W_MERGE), BF16),
            jax.ShapeDtypeStruct((n, W_GATE), F32),
        ],
        compiler_params=_params(("parallel",)),
        name="inproj",
    )(xa, xb, g, wm, wg, wgate)


def _prompt_init(ca0_ref, cb0_ref, c0_ref, n0_ref, m0_ref, xa_s, xb_s, c_s, n_s, m_s):
    xa_s[0:8, :] = jnp.zeros((8, D_A), F32)
    xb_s[0:32, :] = jnp.zeros((32, D_B), F32)
    xa_s[6:8, :] = ca0_ref[...]
    xb_s[2:32, :] = cb0_ref[...]
    c_s[...] = c0_ref[...]
    n_s[0:1, :] = n0_ref[...]
    m_s[0:1, :] = m0_ref[...]


def _prompt_chunk(zm_ref, gates_ref, caw_ref, cbw_ref, cbb_ref, lng_ref, lnb_ref, bif_ref, mng_ref,
                  p_ref, sa_ref, sb_ref, c_out_ref, n_out_ref, m_out_ref,
                  xa_s, xb_s, xbr_s, aconv_s, bconv_s, c_s, n_s, m_s):
    L = CHUNK

    xa_s[8:8 + L, :] = (zm_ref[:, OFF_AC:OFF_AC + D_A].astype(F32)
                        * zm_ref[:, OFF_AX:OFF_AX + D_A].astype(F32))
    xb_s[32:32 + L, :] = zm_ref[:, OFF_BV:OFF_BV + D_B].astype(F32) * jax.nn.sigmoid(
        zm_ref[:, OFF_BG:OFF_BG + D_B].astype(F32))
    for r in range(1, 8):
        xbr_s[r] = xb_s[r:r + 24 + L, :]

    def conv_block(cblk):
        cs = slice(cblk * 128, (cblk + 1) * 128)
        acc = caw_ref[0:1, cs] * xa_s[6:6 + L, cs]
        for k in range(1, K_A):
            acc = acc + caw_ref[k:k + 1, cs] * xa_s[6 + k:6 + k + L, cs]
        aconv_s[:, cs] = acc
        acc = None
        for k in range(K_B):
            a8, r = (2 + k) // 8 * 8, (2 + k) % 8
            src = xb_s[a8:a8 + L, cs] if r == 0 else xbr_s[r, a8:a8 + L, cs]
            term = cbw_ref[k:k + 1, cs] * src
            acc = term if acc is None else acc + term
        bconv_s[:, cs] = acc

    row = lax.broadcasted_iota(jnp.int32, (L, L), 0)
    col = lax.broadcasted_iota(jnp.int32, (L, L), 1)
    causal = col <= row
    g = gates_ref[...] + bif_ref[...]
    logf = jnp.minimum(g, 0.0) - jnp.log1p(jnp.exp(-jnp.abs(g)))
    bt = _dot_f32(causal.astype(F32), logf)
    bt_h = pltpu.roll(bt, 128 - H_C, axis=1)
    g_t = g.T
    bt_t = bt.T
    inter = bt_h + m_s[0:1, :]
    lane = lax.broadcasted_iota(jnp.int32, (L, W_GATE), 1)
    dlogs = []
    rmax = jnp.zeros((L, W_GATE), F32)
    for h in range(H_C):
        dlog = jnp.where(causal, bt[:, H_C + h:H_C + h + 1] - bt_t[H_C + h:H_C + h + 1, :]
                         + g_t[h:h + 1, :], -jnp.inf)
        dlogs.append(dlog)
        rmax = jnp.where(lane == h, jnp.max(dlog, axis=-1, keepdims=True), rmax)
    m_t = jnp.maximum(inter, rmax)
    w_inter = jnp.exp(inter - m_t)
    floor = jnp.exp(-m_t)
    m_new = m_t[L - 1:L, :]
    w_last = jnp.exp(bt_h[L - 1:L, :] - bt_h + g - m_new)
    w_prev = jnp.exp(inter[L - 1:L, :] - m_new)
    wl16 = w_last.T[0:16, :].astype(BF16)
    n_row = n_s[0:1, :]

    def head(h):
        hs = slice(h * DH_C, (h + 1) * DH_C)
        q = zm_ref[:, OFF_Q + h * DH_C:OFF_Q + (h + 1) * DH_C]
        k = zm_ref[:, OFF_K + h * DH_C:OFF_K + (h + 1) * DH_C]
        v = zm_ref[:, OFF_V + h * DH_C:OFF_V + (h + 1) * DH_C]
        c_old = c_s[h]
        s = _dot_nt(q, k) * (DH_C ** -0.5) * jnp.exp(dlogs[h] - m_t[:, h:h + 1])
        num = _dot(s.astype(BF16), v)
        qc = _dot_nt(q, c_old.astype(BF16))
        qn = _dot_nt(q, jnp.broadcast_to(n_row[:, hs], (L, DH_C)).astype(BF16))[:, 0:1]
        wi = w_inter[:, h:h + 1]
        den = jnp.sum(s, axis=-1, keepdims=True) + wi * qn
        rinv = 1.0 / jnp.maximum(jnp.abs(den), floor[:, h:h + 1])
        hh = (num + qc * wi) * rinv
        hn = hh * lax.rsqrt(jnp.mean(hh * hh, axis=-1, keepdims=True) + EPS) * mng_ref[:, hs]
        o = zm_ref[:, OFF_O + h * DH_C:OFF_O + (h + 1) * DH_C].astype(F32)
        p_ref[:, 2 * D_A + h * DH_C:2 * D_A + (h + 1) * DH_C] = (jax.nn.sigmoid(o) * hn).astype(BF16)
        vw = (v.astype(F32) * w_last[:, h:h + 1]).astype(BF16)
        wp = w_prev[:, h:h + 1]
        c_new = wp * c_old + _dot_tn(vw, k) * (DH_C ** -0.5)
        n_new = wp * n_row[:, hs] + _dot(wl16, k)[h:h + 1, :] * (DH_C ** -0.5)
        c_s[h] = c_new
        c_out_ref[h] = c_new
        n_s[0:1, hs] = n_new
        n_out_ref[h:h + 1, :] = n_new

    conv_block(0)
    conv_block(1)
    head(0)
    conv_block(2)
    head(1)
    conv_block(3)
    head(2)

    sa_new = xa_s[6 + L:8 + L, :]
    sb_new = xb_s[2 + L:32 + L, :]
    xa_s[6:8, :] = sa_new
    xb_s[2:32, :] = sb_new
    sa_ref[...] = sa_new
    sb_ref[...] = sb_new
    p_ref[:, 0:D_A] = (zm_ref[:, OFF_AB:OFF_AB + D_A].astype(F32) * aconv_s[...]).astype(BF16)
    bc = bconv_s[...] + cbb_ref[...]
    mu = jnp.mean(bc, axis=-1, keepdims=True)
    xc = bc - mu
    ln = xc * lax.rsqrt(jnp.mean(xc * xc, axis=-1, keepdims=True) + EPS) * lng_ref[...] + lnb_ref[...]
    p_ref[:, D_A:D_A + D_B] = (ln * jax.nn.sigmoid(ln)).astype(BF16)

    head(3)
    m_s[0:1, :] = m_new
    m_out_ref[...] = m_new


def _mixer_kernel(*refs, seq_rows, carry, has_acc):
    pair = PROMPT_PAIR if carry else 1
    zm_refs, gates_refs, refs = refs[:pair], refs[pair:2 * pair], refs[2 * pair:]
    (ca0_ref, cb0_ref, c0_ref, n0_ref, m0_ref,
     caw_ref, cbw_ref, cbb_ref, lng_ref, lnb_ref, bif_ref, mng_ref) = refs[:12]
    refs = refs[13:] if has_acc else refs[12:]
    p_ref, sa_ref, sb_ref, c_out_ref, n_out_ref, m_out_ref = refs[:6]
    zm_ref, gates_ref = zm_refs[0], gates_refs[0]
    L = CHUNK
    nseq = L // seq_rows
    t = pl.program_id(1)
    j = pl.program_id(2)

    if carry:
        assert nseq == 1
        xa_s, xb_s, xbr_s, aconv_s, bconv_s, c_s, n_s, m_s = refs[6:]

        @pl.when(t == 0)
        def _():
            for u in range(pair):
                _prompt_init(ca0_ref.at[0, u], cb0_ref.at[0, u], c0_ref.at[0, u], n0_ref.at[u], m0_ref.at[u],
                             xa_s.at[u], xb_s.at[u], c_s.at[u], n_s.at[u], m_s.at[u])

        def chunk(ci, _):
            rs = pl.ds(ci * L if isinstance(ci, int) else pl.multiple_of(ci * L, L), L)
            for u in range(pair):
                _prompt_chunk(zm_refs[u].at[rs], gates_refs[u].at[rs],
                              caw_ref, cbw_ref, cbb_ref, lng_ref, lnb_ref, bif_ref, mng_ref,
                              p_ref.at[u, rs], sa_ref.at[u], sb_ref.at[u], c_out_ref.at[0, u], n_out_ref.at[u],
                              m_out_ref.at[u], xa_s.at[u], xb_s.at[u], xbr_s.at[u], aconv_s.at[u],
                              bconv_s.at[u], c_s.at[u], n_s.at[u], m_s.at[u])
            return 0

        n_inner = zm_ref.shape[0] // L
        if n_inner == 1:
            chunk(0, 0)
        else:
            lax.fori_loop(0, n_inner, chunk, 0)
        return

    aconv_s, bconv_s, num_s, qc_s, vw_s, winter_s, rinv_s, wlt_s, wprev_s, mnew_s = refs[6:]

    @pl.when(j == 0)
    def _pre():
        a_b = zm_ref[:, OFF_AB:OFF_AB + D_A].astype(F32)
        ca = zm_ref[:, OFF_AC:OFF_AC + D_A].astype(F32) * zm_ref[:, OFF_AX:OFF_AX + D_A].astype(F32)
        cb = zm_ref[:, OFF_BV:OFF_BV + D_B].astype(F32) * jax.nn.sigmoid(
            zm_ref[:, OFF_BG:OFF_BG + D_B].astype(F32))

        r_i = lax.broadcasted_iota(jnp.int32, (L, L), 0)
        c_i = lax.broadcasted_iota(jnp.int32, (L, L), 1)
        to_tok = (c_i == (r_i % nseq) * seq_rows + r_i // nseq).astype(F32)
        to_seq = (c_i == (r_i % seq_rows) * nseq + r_i // seq_rows).astype(F32)

        def short_conv(x, st_ref, w_ref, taps, new_ref):
            x_tok = _dot_f32(to_tok, x)
            window = [st_ref[i] for i in range(taps - 1)]
            window += [x_tok[tk * nseq:(tk + 1) * nseq] for tk in range(seq_rows)]
            outs_tok = []
            for tk in range(seq_rows):
                acc = w_ref[0:1, :] * window[tk]
                for k in range(1, taps):
                    acc = acc + w_ref[k:k + 1, :] * window[tk + k]
                outs_tok.append(acc)
            for i in range(taps - 1):
                new_ref[i] = window[seq_rows + i]
            return _dot_f32(to_seq, jnp.concatenate(outs_tok, axis=0))

        aconv_s[...] = short_conv(ca, ca0_ref, caw_ref, K_A, sa_ref)
        bconv_s[...] = short_conv(cb, cb0_ref, cbw_ref, K_B, sb_ref)

        p_ref[:, 0:D_A] = (a_b * aconv_s[...]).astype(BF16)
        bc = bconv_s[...] + cbb_ref[...]
        mu = jnp.mean(bc, axis=-1, keepdims=True)
        xc = bc - mu
        ln = xc * lax.rsqrt(jnp.mean(xc * xc, axis=-1, keepdims=True) + EPS) * lng_ref[...] + lnb_ref[...]
        p_ref[:, D_A:D_A + D_B] = (ln * jax.nn.sigmoid(ln)).astype(BF16)

        row = lax.broadcasted_iota(jnp.int32, (L, L), 0)
        col = lax.broadcasted_iota(jnp.int32, (L, L), 1)
        same = (row // seq_rows) == (col // seq_rows)
        causal = same & (col <= row)
        g = gates_ref[...] + bif_ref[...]
        logf = jnp.minimum(g, 0.0) - jnp.log1p(jnp.exp(-jnp.abs(g)))
        bt = _dot_f32(causal.astype(F32), logf)
        bt_h = pltpu.roll(bt, 128 - H_C, axis=1)
        btl_h = pltpu.roll(_dot_f32(same.astype(F32), logf), 128 - H_C, axis=1)
        g_t = g.T
        bt_t = bt.T
        inter = bt_h + m0_ref[0]
        lane = lax.broadcasted_iota(jnp.int32, (L, W_GATE), 1)

        dlogs = []
        rmax = jnp.zeros((L, W_GATE), F32)
        for h in range(H_C):
            dlog = jnp.where(causal, bt[:, H_C + h:H_C + h + 1] - bt_t[H_C + h:H_C + h + 1, :]
                             + g_t[h:h + 1, :], -jnp.inf)
            dlogs.append(dlog)
            rmax = jnp.where(lane == h, jnp.max(dlog, axis=-1, keepdims=True), rmax)
        m_t = jnp.maximum(inter, rmax)
        w_inter = jnp.exp(inter - m_t)
        floor = jnp.exp(-m_t)
        last = (col == (row // seq_rows) * seq_rows + (seq_rows - 1)).astype(F32)
        m_new = _dot_f32(last, m_t)
        w_last = jnp.exp(btl_h - bt_h + g - m_new)
        w_prev = jnp.exp(inter - m_new)
        n_rows = n0_ref[0]

        den = jnp.zeros((L, W_GATE), F32)
        for h in range(H_C):
            hs = slice(h * DH_C, (h + 1) * DH_C)
            q = zm_ref[:, OFF_Q + h * DH_C:OFF_Q + (h + 1) * DH_C]
            k = zm_ref[:, OFF_K + h * DH_C:OFF_K + (h + 1) * DH_C]
            v = zm_ref[:, OFF_V + h * DH_C:OFF_V + (h + 1) * DH_C]
            s = _dot_nt(q, k) * (DH_C ** -0.5) * jnp.exp(dlogs[h] - m_t[:, h:h + 1])
            num_s[:, hs] = _dot(s.astype(BF16), v)
            qn_all = _dot_nt(q, n_rows[:, hs].astype(BF16))
            qn = jnp.sum(jnp.where(row == col, qn_all, 0.0), axis=-1, keepdims=True)
            den_h = jnp.sum(s, axis=-1, keepdims=True) + w_inter[:, h:h + 1] * qn
            den = jnp.where(lane == h, den_h, den)
            vw_s[h] = (v.astype(F32) * w_last[:, h:h + 1]).astype(BF16)
        qc_s[...] = jnp.zeros(qc_s.shape, F32)
        winter_s[...] = w_inter
        rinv_s[...] = 1.0 / jnp.maximum(jnp.abs(den), floor)
        wlt_s[...] = w_last.T[0:16, :]
        wprev_s[...] = w_prev
        mnew_s[...] = m_new

    spp = c0_ref.shape[1]
    tile = spp * seq_rows
    r0 = pl.multiple_of(j * tile, tile)
    rid = lax.broadcasted_iota(jnp.int32, (tile, 1), 0)
    lane_l = lax.broadcasted_iota(jnp.int32, (1, L), 1)
    row_l = lax.broadcasted_iota(jnp.int32, (L, 1), 0)
    for u in range(spp):
        sq = j * spp + u
        last_row = sq * seq_rows + (seq_rows - 1)
        seq_lanes = (lane_l // seq_rows) == sq
        seq_rows_mask = (row_l // seq_rows) == sq
        wprev_row = wprev_s[pl.ds(last_row, 1), :]
        for h in range(H_C):
            hs = slice(h * DH_C, (h + 1) * DH_C)
            c_old = c0_ref[0, u, h]
            q16 = zm_ref[pl.ds(r0, tile), OFF_Q + h * DH_C:OFF_Q + (h + 1) * DH_C]
            r = _dot_nt(q16, c_old.astype(BF16))
            qc_s[pl.ds(r0, tile), hs] = jnp.where((rid // seq_rows) == u, r, qc_s[pl.ds(r0, tile), hs])
            k = zm_ref[:, OFF_K + h * DH_C:OFF_K + (h + 1) * DH_C]
            vw = jnp.where(seq_rows_mask, vw_s[h], jnp.zeros((L, DH_C), BF16))
            wl = jnp.where(seq_lanes, wlt_s[...], 0.0)
            w_prev = wprev_row[:, h:h + 1]
            c_out_ref[0, u, h] = w_prev * c_old + _dot_tn(vw, k) * (DH_C ** -0.5)
            ksum = _dot(wl.astype(BF16), k)[h:h + 1, :]
            n_old = n0_ref[0, pl.ds(sq * seq_rows, 1), hs]
            n_out_ref[u, h:h + 1, :] = w_prev * n_old + ksum * (DH_C ** -0.5)
        m_out_ref[u] = mnew_s[pl.ds(last_row, 1), :]

    @pl.when(j == nseq // spp - 1)
    def _post():
        for h in range(H_C):
            hs = slice(h * DH_C, (h + 1) * DH_C)
            hh = (num_s[:, hs] + qc_s[:, hs] * winter_s[:, h:h + 1]) * rinv_s[:, h:h + 1]
            hn = hh * lax.rsqrt(jnp.mean(hh * hh, axis=-1, keepdims=True) + EPS) * mng_ref[:, hs]
            o = zm_ref[:, OFF_O + h * DH_C:OFF_O + (h + 1) * DH_C].astype(F32)
            p_ref[:, 2 * D_A + h * DH_C:2 * D_A + (h + 1) * DH_C] = (jax.nn.sigmoid(o) * hn).astype(BF16)


def _mixer(zm, gates, ca0, cb0, c0, n0, m0, caw, cbw, cbb, lng, lnb, bif, mng, c_acc, *,
           row_block0, n_groups, n_chunks, seq_rows, carry, layer, layer_in, layer_out):
    L = CHUNK
    nseq = L // seq_rows
    n_state = n_groups if carry else n_chunks * nseq
    rows = n_groups * n_chunks * L

    if carry:
        pair = PROMPT_PAIR
        spp = pair

        def seq_of(b, t, j):
            return b

        def conv_of(b, t, j):
            return b
        rows_blk = 1

        cps = PROMPT_CHUNKS_PER_STEP
        assert n_chunks % cps == 0 and row_block0 % cps == 0
        tok_rows = cps * L

        def tok(u):
            return lambda b, t, j: (row_block0 // cps + (b * pair + u) * (n_chunks // cps) + t, 0)

        def conv_in(taps, ch):
            return pl.BlockSpec((1, pair, taps - 1, ch), lambda *g: (layer_in, conv_of(*g), 0, 0))

        def conv_out(taps, ch):
            return (pl.BlockSpec((pair, taps - 1, ch), lambda *g: (conv_of(*g), 0, 0)),
                    jax.ShapeDtypeStruct((n_state, taps - 1, ch), F32))
        p_spec = pl.BlockSpec((pair, tok_rows, W_P), lambda b, t, j: (b, t, 0))
        p_shape = jax.ShapeDtypeStruct((n_groups, n_chunks * L, W_P), BF16)
        grid = (n_groups // pair, n_chunks // cps, 1)
        lead = (pair,)
    else:
        pair = 1
        tok_rows = L
        spp = SAMPLE_ROWS_PER_STEP // seq_rows

        def tok(u):
            return lambda b, t, j: (row_block0 + b * n_chunks + t, 0)
        p_spec = pl.BlockSpec((L, W_P), lambda b, t, j: (b * n_chunks + t, 0))
        p_shape = jax.ShapeDtypeStruct((rows, W_P), BF16)
        grid = (n_groups, n_chunks, nseq // spp)
        lead = ()

        def seq_of(b, t, j):
            return t * (nseq // spp) + j

        def conv_of(b, t, j):
            return t
        rows_blk = L

        def conv_in(taps, ch):
            return pl.BlockSpec((None, taps - 1, nseq, ch), lambda *g: (layer_in, 0, conv_of(*g), 0))

        def conv_out(taps, ch):
            return (pl.BlockSpec((taps - 1, nseq, ch), lambda *g: (0, conv_of(*g), 0)),
                    jax.ShapeDtypeStruct((taps - 1, n_state, ch), F32))

    has_acc = c_acc is not None
    kern = functools.partial(_mixer_kernel, seq_rows=seq_rows, carry=carry, has_acc=has_acc)
    in_specs = [pl.BlockSpec((tok_rows, W_MAIN), tok(u)) for u in range(pair)]
    in_specs += [pl.BlockSpec((tok_rows, W_GATE), tok(u)) for u in range(pair)]
    in_specs += [
        conv_in(K_A, D_A),
        conv_in(K_B, D_B),
        pl.BlockSpec((1, spp, H_C, DH_C, DH_C), lambda *g: (layer_in, seq_of(*g), 0, 0, 0)),
        pl.BlockSpec((None, pair, rows_blk, D_MODEL), lambda *g: (layer_in, conv_of(*g), 0, 0)),
        pl.BlockSpec((None, pair, rows_blk, W_GATE), lambda *g: (layer_in, conv_of(*g), 0, 0)),
        _resident((K_A, D_A), layer),
        _resident((K_B, D_B), layer),
        _resident((1, D_B), layer),
        _resident((1, D_B), layer),
        _resident((1, D_B), layer),
        _resident((1, W_GATE), layer),
        _resident((1, D_MODEL), layer),
    ]
    args = [zm] * pair + [gates] * pair + [ca0, cb0, c0, n0, m0, caw, cbw, cbb, lng, lnb, bif, mng]
    aliases = {}
    if has_acc:
        in_specs.append(pl.BlockSpec(memory_space=pl.ANY))
        args.append(c_acc)
        aliases = {len(args) - 1: 3}
    outs = pl.pallas_call(
        kern,
        grid=grid,
        in_specs=in_specs,
        out_specs=[
            p_spec,
            conv_out(K_A, D_A)[0],
            conv_out(K_B, D_B)[0],
            pl.BlockSpec((1, spp, H_C, DH_C, DH_C), lambda *g: (layer_out, seq_of(*g), 0, 0, 0)),
            pl.BlockSpec((spp, H_C, DH_C), lambda *g: (seq_of(*g), 0, 0)),
            pl.BlockSpec((spp, 1, W_GATE), lambda *g: (seq_of(*g), 0, 0)),
        ],
        out_shape=[
            p_shape,
            conv_out(K_A, D_A)[1],
            conv_out(K_B, D_B)[1],
            jax.ShapeDtypeStruct((DEPTH, n_state, H_C, DH_C, DH_C), F32),
            jax.ShapeDtypeStruct((n_state, H_C, DH_C), F32),
            jax.ShapeDtypeStruct((n_state, 1, W_GATE), F32),
        ],
        input_output_aliases=aliases,
        scratch_shapes=[
            pltpu.VMEM(lead + (8 + L, D_A), F32),
            pltpu.VMEM(lead + (32 + L, D_B), F32),
            pltpu.VMEM(lead + (8, 24 + L, D_B), F32),
            pltpu.VMEM(lead + (L, D_A), F32),
            pltpu.VMEM(lead + (L, D_B), F32),
            pltpu.VMEM(lead + (H_C, DH_C, DH_C), F32),
            pltpu.VMEM(lead + (8, D_MODEL), F32),
            pltpu.VMEM(lead + (8, W_GATE), F32),
        ] if carry else [
            pltpu.VMEM((L, D_A), F32),
            pltpu.VMEM((L, D_B), F32),
            pltpu.VMEM((L, D_MODEL), F32),
            pltpu.VMEM((L, D_MODEL), F32),
            pltpu.VMEM((H_C, L, DH_C), BF16),
            pltpu.VMEM((L, W_GATE), F32),
            pltpu.VMEM((L, W_GATE), F32),
            pltpu.VMEM((16, L), F32),
            pltpu.VMEM((L, W_GATE), F32),
            pltpu.VMEM((L, W_GATE), F32),
        ],
        compiler_params=_params(("arbitrary", "arbitrary", "arbitrary")),
        name="mixer_prompt" if carry else "mixer_sample",
    )(*args)
    outs = list(outs)
    outs[0] = outs[0].reshape(rows, W_P)
    return outs


def _outproj_kernel(xa_ref, xb_ref, pp_ref, ps_ref, zg_ref, wa_ref, wb_ref, wc_ref, wo_ref, gx_ref, wq_ref,
                    x1_ref, q_ref, *, n_prompt_tiles):
    i = pl.program_id(0)
    x = jnp.where(i >= n_prompt_tiles, xb_ref[...], xa_ref[...])
    p = jnp.where(i >= n_prompt_tiles, ps_ref[...], pp_ref[...])
    y_a = _dot(p[:, 0:D_A], wa_ref[...])
    y_b = _dot(p[:, D_A:D_A + D_B], wb_ref[...])
    y_c = _dot(p[:, D_A + D_B:], wc_ref[...])
    u = (jax.nn.sigmoid(zg_ref[:, 0:D_MODEL].astype(F32)) * y_a
         + jax.nn.sigmoid(zg_ref[:, D_MODEL:2 * D_MODEL].astype(F32)) * y_b
         + jax.nn.sigmoid(zg_ref[:, 2 * D_MODEL:].astype(F32)) * y_c)
    x1 = x + _dot(u.astype(BF16), wo_ref[...])
    x1_ref[...] = x1
    q_ref[...] = _dot(_rmsnorm(x1, gx_ref[...]).astype(BF16), wq_ref[...]).astype(BF16)


def _outproj(xa, xb, sample_block, p_p, p_s, zg, wa, wb, wc, wo, gx, wq, layer):
    tm = ROW_TILE
    npt = p_p.shape[0] // tm
    n = (npt + 1) * tm
    return pl.pallas_call(
        functools.partial(_outproj_kernel, n_prompt_tiles=npt),
        grid=(n // tm,),
        in_specs=_x_specs(tm, npt, sample_block) + [
            pl.BlockSpec((tm, W_P), lambda i: (jnp.minimum(i, npt - 1), 0)),
            pl.BlockSpec((tm, W_P), lambda i: (0, 0)),
            pl.BlockSpec((tm, W_MERGE), lambda i: (i, 0)),
            _resident((D_A, D_MODEL), layer),
            _resident((D_B, D_MODEL), layer),
            _resident((D_MODEL, D_MODEL), layer),
            _resident((D_MODEL, D_MODEL), layer),
            _resident((1, D_MODEL), layer),
            _resident((D_MODEL, D_MODEL), layer),
        ],
        out_specs=[
            pl.BlockSpec((tm, D_MODEL), lambda i: (i, 0)),
            pl.BlockSpec((tm, D_MODEL), lambda i: (i, 0)),
        ],
        out_shape=[
            jax.ShapeDtypeStruct((n, D_MODEL), F32),
            jax.ShapeDtypeStruct((n, D_MODEL), BF16),
        ],
        compiler_params=_params(("parallel",)),
        name="outproj",
    )(xa, xb, p_p, p_s, zg, wa, wb, wc, wo, gx, wq)


def _memkv_kernel(mem_ref, g_ref, w_ref, k_ref, v_ref, kb_ref, vb_ref):
    h = _rmsnorm(mem_ref[...], g_ref[0]).astype(BF16)
    hd = HX * DX
    k = _dot(h, w_ref[0, :, 0:hd])
    v = _dot(h, w_ref[0, :, hd:])
    for bb in range(k_ref.shape[1]):
        k_ref[0, bb] = k[bb * N_MEM:(bb + 1) * N_MEM].reshape(N_MEM, HX, DX)
        v_ref[0, bb] = v[bb * N_MEM:(bb + 1) * N_MEM].reshape(N_MEM, HX, DX)
    kb_ref[0] = k.astype(BF16)
    vb_ref[0] = v.astype(BF16)


def _memkv(mem, g, w):
    n = mem.shape[0]
    tm = ROW_TILE
    hd = HX * DX
    per = tm // N_MEM
    o_spec = pl.BlockSpec((1, tm, hd), lambda l, i: (l, i, 0))
    o5_spec = pl.BlockSpec((1, per, N_MEM, HX, DX), lambda l, i: (l, i, 0, 0, 0))
    return pl.pallas_call(
        _memkv_kernel,
        grid=(DEPTH, n // tm),
        in_specs=[
            pl.BlockSpec((tm, D_MODEL), lambda l, i: (i, 0)),
            pl.BlockSpec((1, 1, D_MODEL), lambda l, i: (l, 0, 0)),
            pl.BlockSpec((1, D_MODEL, 2 * hd), lambda l, i: (l, 0, 0)),
        ],
        out_specs=[o5_spec, o5_spec, o_spec, o_spec],
        out_shape=[
            jax.ShapeDtypeStruct((DEPTH, n // N_MEM, N_MEM, HX, DX), F32),
            jax.ShapeDtypeStruct((DEPTH, n // N_MEM, N_MEM, HX, DX), F32),
            jax.ShapeDtypeStruct((DEPTH, n, hd), BF16),
            jax.ShapeDtypeStruct((DEPTH, n, hd), BF16),
        ],
        compiler_params=_params(("arbitrary", "arbitrary")),
        name="memkv",
    )(mem, g, w)


def _attend(q, k, v):
    s = _dot_nt(q, k) * (DX ** -0.5)
    e = jnp.exp(s - jnp.max(s, axis=-1, keepdims=True))
    return _dot(e.astype(BF16), v) * (1.0 / jnp.sum(e, axis=-1, keepdims=True))


XS_ROWS = 32
XATTN_TILE = 1024


def _xattn_kernel(q_ref, k_ref, v_ref, qs_ref, ks_ref, vs_ref, o_ref, os_ref, *, seq_rows):
    for h in range(HX):
        hs = slice(h * DX, (h + 1) * DX)
        o_ref[:, hs] = _attend(q_ref[:, hs], k_ref[0, :, hs], v_ref[0, :, hs]).astype(BF16)
    _xattn_short(qs_ref, ks_ref, vs_ref, os_ref, seq_rows)


def _xattn(q, kb, vb, k_cache, v_cache, layer, *, n_groups, rows_per_group, seq_rows):
    tq = XATTN_TILE
    nt = rows_per_group // tq
    hd = HX * DX
    per = XS_ROWS // seq_rows
    n_seq = k_cache.shape[1]
    assert n_groups * nt * per == n_seq
    qs_block0 = n_groups * rows_per_group // XS_ROWS
    kv_spec = pl.BlockSpec((1, N_MEM, hd), lambda b, t: (layer, b, 0))
    cache_spec = pl.BlockSpec((1, per, N_MEM, HX, DX), lambda b, t: (layer, b * nt + t, 0, 0, 0))
    return pl.pallas_call(
        functools.partial(_xattn_kernel, seq_rows=seq_rows),
        grid=(n_groups, nt),
        in_specs=[pl.BlockSpec((tq, hd), lambda b, t: (b * nt + t, 0)), kv_spec, kv_spec,
                  pl.BlockSpec((XS_ROWS, hd), lambda b, t: (qs_block0 + b * nt + t, 0)),
                  cache_spec, cache_spec],
        out_specs=[pl.BlockSpec((tq, hd), lambda b, t: (b * nt + t, 0)),
                   pl.BlockSpec((XS_ROWS, hd), lambda b, t: (b * nt + t, 0))],
        out_shape=[jax.ShapeDtypeStruct((n_groups * rows_per_group, hd), BF16),
                   jax.ShapeDtypeStruct((n_seq * seq_rows, hd), BF16)],
        compiler_params=_params(("parallel", "parallel")),
        name="xattn",
    )(q, kb, vb, q, k_cache, v_cache)


def _xattn_short(q_ref, k_ref, v_ref, o_ref, seq_rows):
    rows = q_ref.shape[0]
    grp = HX * seq_rows
    n = HX * rows
    assert grp == 16
    qs = jnp.concatenate([q_ref[:, h * DX:(h + 1) * DX] for h in range(HX)], axis=0)
    ri = lax.broadcasted_iota(jnp.int32, (n, n), 0)
    ci = lax.broadcasted_iota(jnp.int32, (n, n), 1)
    by_seq = ci == ((ri % grp) // seq_rows) * rows + (ri // grp) * seq_rows + ri % seq_rows
    by_head = ci == ((ri % rows) // seq_rows) * grp + (ri // rows) * seq_rows + ri % seq_rows
    qp = _dot(by_seq.astype(BF16), qs).astype(BF16)
    rid = lax.broadcasted_iota(jnp.int32, (grp, 1), 0)
    cid = lax.broadcasted_iota(jnp.int32, (grp, N_MEM * HX), 1)
    own_head = (cid % HX) == (rid // seq_rows)
    outs = []
    for e in range(rows // seq_rows):
        k2 = k_ref[0, e].reshape(N_MEM * HX, DX).astype(BF16)
        v2 = v_ref[0, e].reshape(N_MEM * HX, DX).astype(BF16)
        s = jnp.where(own_head, _dot_nt(qp[e * grp:(e + 1) * grp], k2) * (DX ** -0.5), -jnp.inf)
        p = jnp.exp(s - jnp.max(s, axis=-1, keepdims=True))
        o = _dot(p.astype(BF16), v2) * (1.0 / jnp.sum(p, axis=-1, keepdims=True))
        outs.append(o.astype(BF16))
    ob = _dot(by_head.astype(BF16), jnp.concatenate(outs, axis=0)).astype(BF16)
    for h in range(HX):
        o_ref[:, h * DX:(h + 1) * DX] = ob[h * rows:(h + 1) * rows]


FF_BLOCK = 256


def _ffn_kernel(x_ref, cp_ref, cs_ref, wxo_ref, gf_ref, wg_ref, wu_ref, wout_ref, gfin_ref,
                *rest, n_prompt_tiles, final):
    act_s = rest[-1]
    i = pl.program_id(0)
    ctx = jnp.where(i >= n_prompt_tiles, cs_ref[...], cp_ref[...])
    x2 = x_ref[...] + _dot(ctx, wxo_ref[...])
    h = _rmsnorm(x2, gf_ref[...]).astype(BF16)
    for jb in range(D_FF // FF_BLOCK):
        sl = slice(jb * FF_BLOCK, (jb + 1) * FF_BLOCK)
        gate = _dot(h, wg_ref[:, sl])
        up = _dot(h, wu_ref[:, sl])
        act_s[:, sl] = (gate * jax.nn.sigmoid(gate) * up).astype(BF16)
    x3 = x2 + _dot(act_s[...], wout_ref[...])
    if final:
        y = _rmsnorm(x3, gfin_ref[...])
        yp_ref, ys_ref = rest[0], rest[1]

        @pl.when(i < n_prompt_tiles)
        def _():
            yp_ref[...] = y

        @pl.when(i >= n_prompt_tiles)
        def _():
            ys_ref[...] = y
    else:
        rest[0][...] = x3


def _ffn(x, c_p, c_s, wxo, gf, w_in, wout, gfin, layer, *, final):
    n = x.shape[0]
    tm = ROW_TILE
    npt = c_p.shape[0] // tm
    if final:
        assert n - npt * tm == tm
        out_specs = [pl.BlockSpec((tm, D_MODEL), lambda i: (jnp.minimum(i, npt - 1), 0)),
                     pl.BlockSpec((tm, D_MODEL), lambda i: (0, 0))]
        out_shape = [jax.ShapeDtypeStruct((npt * tm, D_MODEL), F32),
                     jax.ShapeDtypeStruct((tm, D_MODEL), F32)]
        sem = ("arbitrary",)
    else:
        out_specs = pl.BlockSpec((tm, D_MODEL), lambda i: (i, 0))
        out_shape = jax.ShapeDtypeStruct((n, D_MODEL), F32)
        sem = ("parallel",)
    return pl.pallas_call(
        functools.partial(_ffn_kernel, n_prompt_tiles=npt, final=final),
        grid=(n // tm,),
        in_specs=[
            pl.BlockSpec((tm, D_MODEL), lambda i: (i, 0)),
            pl.BlockSpec((tm, D_MODEL), lambda i: (jnp.minimum(i, npt - 1), 0)),
            pl.BlockSpec((tm, D_MODEL), lambda i: (0, 0)),
            _resident((D_MODEL, D_MODEL), layer),
            _resident((1, D_MODEL), layer),
            _resident((D_MODEL, D_FF), layer, col=0),
            _resident((D_MODEL, D_FF), layer, col=1),
            _resident((D_FF, D_MODEL), layer),
            _resident((1, D_MODEL)),
        ],
        out_specs=out_specs,
        out_shape=out_shape,
        scratch_shapes=[pltpu.VMEM((tm, D_FF), BF16)],
        compiler_params=_params(sem),
        name="ffn",
    )(x, c_p, c_s, wxo, gf, w_in, w_in, wout, gfin)


def kernel(x_prompt, x_sample, state_conv_a, state_conv_b, state_mlstm_c, state_mlstm_n, state_mlstm_m,
           cache_mem_k, cache_mem_v, mem_prompt, norm_mix_g, w_in, b_if, conv_a_w, w_out_a, conv_b_w,
           conv_b_b, ln_b_g, ln_b_b, w_out_b, mlstm_norm_g, w_out_c, w_o, norm_x_g, norm_mem_g, w_xq,
           w_xkv, w_xo, norm_ffn_g, w_ffn_in, w_ffn_out, final_norm_g):
    bp, tp, d = x_prompt.shape
    bs, ts, _ = x_sample.shape
    n_p = bp * tp
    n_s = bs * ts
    hd = HX * DX

    gate_lo = W_MAIN
    gate_hi = W_MAIN + 2 * H_C
    w_inb = w_in.astype(BF16)
    w_gate = jnp.pad(w_inb[:, :, gate_lo:gate_hi], ((0, 0), (0, 0), (0, W_GATE - 2 * H_C)))
    w_merge = w_inb[:, :, gate_hi:]
    bif = jnp.pad(b_if, ((0, 0), (0, W_GATE - 2 * H_C)))[:, None, :]
    w_a = w_out_a.astype(BF16)
    w_b = w_out_b.astype(BF16)
    w_c = w_out_c.astype(BF16)
    w_ob = w_o.astype(BF16)
    w_q = w_xq.astype(BF16)
    w_kv = w_xkv.astype(BF16)
    w_xob = w_xo.astype(BF16)
    w_fi = w_ffn_in.astype(BF16)
    w_fo = w_ffn_out.astype(BF16)
    g_mix = norm_mix_g[:, None, :]
    g_x = norm_x_g[:, None, :]
    g_ffn = norm_ffn_g[:, None, :]
    small = (conv_a_w, conv_b_w, conv_b_b[:, None, :], ln_b_g[:, None, :], ln_b_b[:, None, :],
             bif, mlstm_norm_g[:, None, :])

    x = (x_prompt.reshape(n_p, d), x_sample.reshape(n_s, d), 0)

    mem_k, mem_v, mem_kb, mem_vb = _memkv(mem_prompt.reshape(bp * N_MEM, d), norm_mem_g[:, None, :], w_kv)

    zeros_ca = jnp.zeros((1, bp, K_A - 1, D_A), F32)
    zeros_cb = jnp.zeros((1, bp, K_B - 1, D_B), F32)
    zeros_c = jnp.zeros((1, bp, H_C, DH_C, DH_C), F32)
    zeros_n = jnp.zeros((1, bp, 1, H_C * DH_C), F32)
    zeros_m = jnp.zeros((1, bp, 1, W_GATE), F32)

    n_chunks_s = n_s // CHUNK
    n0_rows = jnp.repeat(state_mlstm_n.reshape(DEPTH, bs, H_C * DH_C), ts, axis=1)
    n0_rows = n0_rows.reshape(DEPTH, n_chunks_s, CHUNK, H_C * DH_C)
    m0_rows = jnp.repeat(jnp.pad(state_mlstm_m, ((0, 0), (0, 0), (0, W_GATE - H_C))), ts, axis=1)
    m0_rows = m0_rows.reshape(DEPTH, n_chunks_s, CHUNK, W_GATE)

    conv_a_tm = jnp.transpose(state_conv_a, (0, 2, 1, 3))
    conv_b_tm = jnp.transpose(state_conv_b, (0, 2, 1, 3))

    outs = {k: [] for k in ("pa", "pb", "pn", "pm", "sa", "sb", "sn", "sm")}
    c1 = c2 = None
    for l in range(DEPTH):
        zm, zg, gates = _inproj(*x, g_mix, w_inb, w_merge, w_gate, l)
        p_p, a1, b1, c1, n1, m1 = _mixer(
            zm, gates, zeros_ca, zeros_cb, zeros_c, zeros_n, zeros_m, *small, c1,
            row_block0=0, n_groups=bp, n_chunks=tp // CHUNK, seq_rows=CHUNK, carry=True,
            layer=l, layer_in=0, layer_out=l)
        p_s, a2, b2, c2, n2, m2 = _mixer(
            zm, gates, conv_a_tm, conv_b_tm, state_mlstm_c, n0_rows, m0_rows, *small, c2,
            row_block0=n_p // CHUNK, n_groups=1, n_chunks=n_chunks_s, seq_rows=ts, carry=False,
            layer=l, layer_in=l, layer_out=l)
        x1, qx = _outproj(*x, p_p, p_s, zg, w_a, w_b, w_c, w_ob, g_x, w_q, l)
        c_p, c_s = _xattn(qx, mem_kb, mem_vb, cache_mem_k, cache_mem_v, l,
                          n_groups=bp, rows_per_group=tp, seq_rows=ts)
        y = _ffn(x1, c_p, c_s, w_xob, g_ffn, w_fi, w_fo, final_norm_g[None, :], l, final=(l == DEPTH - 1))
        x = (y, y, n_p // ROW_TILE)
        outs["pa"].append(a1); outs["pb"].append(b1); outs["pn"].append(n1); outs["pm"].append(m1)
        outs["sa"].append(a2); outs["sb"].append(b2); outs["sn"].append(n2); outs["sm"].append(m2)

    y_prompt = y[0].reshape(bp, tp, d)
    y_sample = y[1].reshape(bs, ts, d)
    st = {k: jnp.stack(v) for k, v in outs.items()}
    return (y_prompt, y_sample,
            st["pa"], st["pb"], c1, st["pn"], st["pm"][:, :, 0, :H_C],
            mem_k, mem_v,
            jnp.transpose(st["sa"], (0, 2, 1, 3)), jnp.transpose(st["sb"], (0, 2, 1, 3)), c2,
            st["sn"], st["sm"][:, :, 0, :H_C])
```

```python
import functools

import jax
import jax.numpy as jnp
from jax import lax
from jax.experimental import pallas as pl
from jax.experimental.pallas import tpu as pltpu

D_MODEL = 1024
DEPTH = 4
D_A = 512
K_A = 3
D_B = 512
K_B = 31
H_C = 4
DH_C = 256
N_MEM = 256
HX = 4
DX = 256
D_FF = 2816
EPS = 1e-6

OFF_AB, OFF_AC, OFF_AX, OFF_BV, OFF_BG = 0, 512, 1024, 1536, 2048
OFF_Q, OFF_K, OFF_V, OFF_O = 2560, 3584, 4608, 5632
W_MAIN = 6656
W_MERGE = 3 * D_MODEL
W_GATE = 128
W_P = 2 * D_A + D_MODEL

CHUNK = 128
SAMPLE_ROWS_PER_STEP = 32
PROMPT_PAIR = 1
PROMPT_CHUNKS_PER_STEP = 1
ROW_TILE = 512
VMEM_LIMIT_BYTES = 56 * 1024 * 1024

F32 = jnp.float32
BF16 = jnp.bfloat16


def _dot(a, b):
    return jnp.dot(a, b, preferred_element_type=F32)


def _dot_nt(a, b):
    return lax.dot_general(a, b, (((1,), (1,)), ((), ())), preferred_element_type=F32)


def _dot_tn(a, b):
    return lax.dot_general(a, b, (((0,), (0,)), ((), ())), preferred_element_type=F32)


def _dot_f32(a, b):
    return jnp.dot(a, b, preferred_element_type=F32, precision=lax.Precision.HIGHEST)


def _rmsnorm(x, g):
    ms = jnp.mean(x * x, axis=-1, keepdims=True)
    return x * lax.rsqrt(ms + EPS) * g


def _resident(shape, layer=None, col=0):
    nd = len(shape)
    if layer is None:
        return pl.BlockSpec(shape, lambda *_: (0,) * nd, pipeline_mode=pl.Buffered(1))
    return pl.BlockSpec((None,) + shape, lambda *_: (layer,) + (0,) * (nd - 1) + (col,),
                        pipeline_mode=pl.Buffered(1))


def _params(sem):
    return pltpu.CompilerParams(dimension_semantics=sem, vmem_limit_bytes=VMEM_LIMIT_BYTES)


def _x_specs(tm, npt, sample_block, split):
    first = (lambda i: (jnp.minimum(i, npt - 1), 0)) if split else (lambda i: (i, 0))
    return [pl.BlockSpec((tm, D_MODEL), first),
            pl.BlockSpec((tm, D_MODEL), lambda i: (sample_block, 0))]


def _x_tile(xa_ref, xb_ref, n_prompt_tiles, split):
    if not split:
        return xa_ref[...]
    return jnp.where(pl.program_id(0) >= n_prompt_tiles, xb_ref[...], xa_ref[...])


def _inproj_kernel(xa_ref, xb_ref, g_ref, wm_ref, wg_ref, wgate_ref, zm_ref, zg_ref, gates_ref, *,
                   n_prompt_tiles, split):
    h = _rmsnorm(_x_tile(xa_ref, xb_ref, n_prompt_tiles, split), g_ref[...]).astype(BF16)
    for j in range(W_MAIN // 512):
        sl = slice(j * 512, (j + 1) * 512)
        zm_ref[:, sl] = _dot(h, wm_ref[:, sl]).astype(BF16)
    for j in range(W_MERGE // 512):
        sl = slice(j * 512, (j + 1) * 512)
        zg_ref[:, sl] = _dot(h, wg_ref[:, sl]).astype(BF16)
    gates_ref[...] = _dot(h, wgate_ref[...])


def _inproj(xa, xb, sample_block, g, wm, wg, wgate, layer):
    tm = ROW_TILE
    split = xa is not xb
    npt = xa.shape[0] // tm if split else sample_block
    n = (npt + 1) * tm
    return pl.pallas_call(
        functools.partial(_inproj_kernel, n_prompt_tiles=npt, split=split),
        grid=(n // tm,),
        in_specs=_x_specs(tm, npt, sample_block, split) + [
            _resident((1, D_MODEL), layer),
            _resident((D_MODEL, W_MAIN), layer),
            _resident((D_MODEL, W_MERGE), layer),
            _resident((D_MODEL, W_GATE), layer),
        ],
        out_specs=[
            pl.BlockSpec((tm, W_MAIN), lambda i: (i, 0)),
            pl.BlockSpec((tm, W_MERGE), lambda i: (i, 0)),
            pl.BlockSpec((tm, W_GATE), lambda i: (i, 0)),
        ],
        out_shape=[
            jax.ShapeDtypeStruct((n, W_MAIN), BF16),
            jax.ShapeDtypeStruct((n, W_MERGE), BF16),
            jax.ShapeDtypeStruct((n, W_GATE), F32),
        ],
        compiler_params=_params(("parallel",)),
        name="inproj",
    )(xa, xb, g, wm, wg, wgate)


def _prompt_init(ca0_ref, cb0_ref, c0_ref, n0_ref, m0_ref, xa_s, xb_s, c_s, n_s, m_s):
    xa_s[0:8, :] = jnp.zeros((8, D_A), F32)
    xb_s[0:32, :] = jnp.zeros((32, D_B), F32)
    xa_s[6:8, :] = ca0_ref[...]
    xb_s[2:32, :] = cb0_ref[...]
    c_s[...] = c0_ref[...]
    n_s[0:1, :] = n0_ref[...]
    m_s[0:1, :] = m0_ref[...]


def _prompt_chunk(zm_ref, gates_ref, caw_ref, cbw_ref, cbb_ref, lng_ref, lnb_ref, bif_ref, mng_ref,
                  p_ref, sa_ref, sb_ref, c_out_ref, n_out_ref, m_out_ref,
                  xa_s, xb_s, xbr_s, aconv_s, bconv_s, c_s, n_s, m_s):
    L = CHUNK

    xa_s[8:8 + L, :] = (zm_ref[:, OFF_AC:OFF_AC + D_A].astype(F32)
                        * zm_ref[:, OFF_AX:OFF_AX + D_A].astype(F32))
    xb_s[32:32 + L, :] = zm_ref[:, OFF_BV:OFF_BV + D_B].astype(F32) * jax.nn.sigmoid(
        zm_ref[:, OFF_BG:OFF_BG + D_B].astype(F32))
    for r in range(1, 8):
        xbr_s[r] = xb_s[r:r + 24 + L, :]

    def conv_block(cblk):
        cs = slice(cblk * 128, (cblk + 1) * 128)
        acc = caw_ref[0:1, cs] * xa_s[6:6 + L, cs]
        for k in range(1, K_A):
            acc = acc + caw_ref[k:k + 1, cs] * xa_s[6 + k:6 + k + L, cs]
        aconv_s[:, cs] = acc
        acc = None
        for k in range(K_B):
            a8, r = (2 + k) // 8 * 8, (2 + k) % 8
            src = xb_s[a8:a8 + L, cs] if r == 0 else xbr_s[r, a8:a8 + L, cs]
            term = cbw_ref[k:k + 1, cs] * src
            acc = term if acc is None else acc + term
        bconv_s[:, cs] = acc

    row = lax.broadcasted_iota(jnp.int32, (L, L), 0)
    col = lax.broadcasted_iota(jnp.int32, (L, L), 1)
    causal = col <= row
    g = gates_ref[...] + bif_ref[...]
    logf = jnp.minimum(g, 0.0) - jnp.log1p(jnp.exp(-jnp.abs(g)))
    bt = _dot_f32(causal.astype(F32), logf)
    bt_h = pltpu.roll(bt, 128 - H_C, axis=1)
    g_t = g.T
    bt_t = bt.T
    inter = bt_h + m_s[0:1, :]
    lane = lax.broadcasted_iota(jnp.int32, (L, W_GATE), 1)
    dlogs = []
    rmax = jnp.zeros((L, W_GATE), F32)
    for h in range(H_C):
        dlog = jnp.where(causal, bt[:, H_C + h:H_C + h + 1] - bt_t[H_C + h:H_C + h + 1, :]
                         + g_t[h:h + 1, :], -jnp.inf)
        dlogs.append(dlog)
        rmax = jnp.where(lane == h, jnp.max(dlog, axis=-1, keepdims=True), rmax)
    m_t = jnp.maximum(inter, rmax)
    w_inter = jnp.exp(inter - m_t)
    floor = jnp.exp(-m_t)
    m_new = m_t[L - 1:L, :]
    w_last = jnp.exp(bt_h[L - 1:L, :] - bt_h + g - m_new)
    w_prev = jnp.exp(inter[L - 1:L, :] - m_new)
    wl16 = w_last.T[0:16, :].astype(BF16)
    n_row = n_s[0:1, :]

    def head(h):
        hs = slice(h * DH_C, (h + 1) * DH_C)
        q = zm_ref[:, OFF_Q + h * DH_C:OFF_Q + (h + 1) * DH_C]
        k = zm_ref[:, OFF_K + h * DH_C:OFF_K + (h + 1) * DH_C]
        v = zm_ref[:, OFF_V + h * DH_C:OFF_V + (h + 1) * DH_C]
        c_old = c_s[h]
        s = _dot_nt(q, k) * (DH_C ** -0.5) * jnp.exp(dlogs[h] - m_t[:, h:h + 1])
        num = _dot(s.astype(BF16), v)
        qc = _dot_nt(q, c_old.astype(BF16))
        qn = _dot_nt(q, jnp.broadcast_to(n_row[:, hs], (L, DH_C)).astype(BF16))[:, 0:1]
        wi = w_inter[:, h:h + 1]
        den = jnp.sum(s, axis=-1, keepdims=True) + wi * qn
        rinv = 1.0 / jnp.maximum(jnp.abs(den), floor[:, h:h + 1])
        hh = (num + qc * wi) * rinv
        hn = hh * lax.rsqrt(jnp.mean(hh * hh, axis=-1, keepdims=True) + EPS) * mng_ref[:, hs]
        o = zm_ref[:, OFF_O + h * DH_C:OFF_O + (h + 1) * DH_C].astype(F32)
        p_ref[:, 2 * D_A + h * DH_C:2 * D_A + (h + 1) * DH_C] = (jax.nn.sigmoid(o) * hn).astype(BF16)
        vw = (v.astype(F32) * w_last[:, h:h + 1]).astype(BF16)
        wp = w_prev[:, h:h + 1]
        c_new = wp * c_old + _dot_tn(vw, k) * (DH_C ** -0.5)
        n_new = wp * n_row[:, hs] + _dot(wl16, k)[h:h + 1, :] * (DH_C ** -0.5)
        c_s[h] = c_new
        c_out_ref[h] = c_new
        n_s[0:1, hs] = n_new
        n_out_ref[h:h + 1, :] = n_new

    conv_block(0)
    conv_block(1)
    head(0)
    conv_block(2)
    head(1)
    conv_block(3)
    head(2)

    sa_new = xa_s[6 + L:8 + L, :]
    sb_new = xb_s[2 + L:32 + L, :]
    xa_s[6:8, :] = sa_new
    xb_s[2:32, :] = sb_new
    sa_ref[...] = sa_new
    sb_ref[...] = sb_new
    p_ref[:, 0:D_A] = (zm_ref[:, OFF_AB:OFF_AB + D_A].astype(F32) * aconv_s[...]).astype(BF16)
    bc = bconv_s[...] + cbb_ref[...]
    mu = jnp.mean(bc, axis=-1, keepdims=True)
    xc = bc - mu
    ln = xc * lax.rsqrt(jnp.mean(xc * xc, axis=-1, keepdims=True) + EPS) * lng_ref[...] + lnb_ref[...]
    p_ref[:, D_A:D_A + D_B] = (ln * jax.nn.sigmoid(ln)).astype(BF16)

    head(3)
    m_s[0:1, :] = m_new
    m_out_ref[...] = m_new


def _mixer_kernel(*refs, seq_rows, carry, has_acc):
    pair = PROMPT_PAIR if carry else 1
    zm_refs, gates_refs, refs = refs[:pair], refs[pair:2 * pair], refs[2 * pair:]
    (ca0_ref, cb0_ref, c0_ref, n0_ref, m0_ref,
     caw_ref, cbw_ref, cbb_ref, lng_ref, lnb_ref, bif_ref, mng_ref) = refs[:12]
    refs = refs[13:] if has_acc else refs[12:]
    p_ref, sa_ref, sb_ref, c_out_ref, n_out_ref, m_out_ref = refs[:6]
    zm_ref, gates_ref = zm_refs[0], gates_refs[0]
    L = CHUNK
    nseq = L // seq_rows
    t = pl.program_id(1)
    j = pl.program_id(2)

    if carry:
        assert nseq == 1
        xa_s, xb_s, xbr_s, aconv_s, bconv_s, c_s, n_s, m_s = refs[6:]

        @pl.when(t == 0)
        def _():
            for u in range(pair):
                _prompt_init(ca0_ref.at[0, u], cb0_ref.at[0, u], c0_ref.at[0, u], n0_ref.at[u], m0_ref.at[u],
                             xa_s.at[u], xb_s.at[u], c_s.at[u], n_s.at[u], m_s.at[u])

        def chunk(ci, _):
            rs = pl.ds(ci * L if isinstance(ci, int) else pl.multiple_of(ci * L, L), L)
            for u in range(pair):
                _prompt_chunk(zm_refs[u].at[rs], gates_refs[u].at[rs],
                              caw_ref, cbw_ref, cbb_ref, lng_ref, lnb_ref, bif_ref, mng_ref,
                              p_ref.at[u, rs], sa_ref.at[u], sb_ref.at[u], c_out_ref.at[0, u], n_out_ref.at[u],
                              m_out_ref.at[u], xa_s.at[u], xb_s.at[u], xbr_s.at[u], aconv_s.at[u],
                              bconv_s.at[u], c_s.at[u], n_s.at[u], m_s.at[u])
            return 0

        n_inner = zm_ref.shape[0] // L
        if n_inner == 1:
            chunk(0, 0)
        else:
            lax.fori_loop(0, n_inner, chunk, 0)
        return

    aconv_s, bconv_s, num_s, qc_s, vw_s, winter_s, rinv_s, wlt_s, wprev_s, mnew_s = refs[6:]

    @pl.when(j == 0)
    def _pre():
        a_b = zm_ref[:, OFF_AB:OFF_AB + D_A].astype(F32)
        ca = zm_ref[:, OFF_AC:OFF_AC + D_A].astype(F32) * zm_ref[:, OFF_AX:OFF_AX + D_A].astype(F32)
        cb = zm_ref[:, OFF_BV:OFF_BV + D_B].astype(F32) * jax.nn.sigmoid(
            zm_ref[:, OFF_BG:OFF_BG + D_B].astype(F32))

        r_i = lax.broadcasted_iota(jnp.int32, (L, L), 0)
        c_i = lax.broadcasted_iota(jnp.int32, (L, L), 1)
        to_tok = (c_i == (r_i % nseq) * seq_rows + r_i // nseq).astype(F32)
        to_seq = (c_i == (r_i % seq_rows) * nseq + r_i // seq_rows).astype(F32)

        def short_conv(x, st_ref, w_ref, taps, new_ref):
            x_tok = _dot_f32(to_tok, x)
            window = [st_ref[i] for i in range(taps - 1)]
            window += [x_tok[tk * nseq:(tk + 1) * nseq] for tk in range(seq_rows)]
            outs_tok = []
            for tk in range(seq_rows):
                acc = w_ref[0:1, :] * window[tk]
                for k in range(1, taps):
                    acc = acc + w_ref[k:k + 1, :] * window[tk + k]
                outs_tok.append(acc)
            for i in range(taps - 1):
                new_ref[i] = window[seq_rows + i]
            return _dot_f32(to_seq, jnp.concatenate(outs_tok, axis=0))

        aconv_s[...] = short_conv(ca, ca0_ref, caw_ref, K_A, sa_ref)
        bconv_s[...] = short_conv(cb, cb0_ref, cbw_ref, K_B, sb_ref)

        p_ref[:, 0:D_A] = (a_b * aconv_s[...]).astype(BF16)
        bc = bconv_s[...] + cbb_ref[...]
        mu = jnp.mean(bc, axis=-1, keepdims=True)
        xc = bc - mu
        ln = xc * lax.rsqrt(jnp.mean(xc * xc, axis=-1, keepdims=True) + EPS) * lng_ref[...] + lnb_ref[...]
        p_ref[:, D_A:D_A + D_B] = (ln * jax.nn.sigmoid(ln)).astype(BF16)

        row = lax.broadcasted_iota(jnp.int32, (L, L), 0)
        col = lax.broadcasted_iota(jnp.int32, (L, L), 1)
        same = (row // seq_rows) == (col // seq_rows)
        causal = same & (col <= row)
        g = gates_ref[...] + bif_ref[...]
        logf = jnp.minimum(g, 0.0) - jnp.log1p(jnp.exp(-jnp.abs(g)))
        bt = _dot_f32(causal.astype(F32), logf)
        bt_h = pltpu.roll(bt, 128 - H_C, axis=1)
        btl_h = pltpu.roll(_dot_f32(same.astype(F32), logf), 128 - H_C, axis=1)
        g_t = g.T
        bt_t = bt.T
        inter = bt_h + m0_ref[0]
        lane = lax.broadcasted_iota(jnp.int32, (L, W_GATE), 1)

        dlogs = []
        rmax = jnp.zeros((L, W_GATE), F32)
        for h in range(H_C):
            dlog = jnp.where(causal, bt[:, H_C + h:H_C + h + 1] - bt_t[H_C + h:H_C + h + 1, :]
                             + g_t[h:h + 1, :], -jnp.inf)
            dlogs.append(dlog)
            rmax = jnp.where(lane == h, jnp.max(dlog, axis=-1, keepdims=True), rmax)
        m_t = jnp.maximum(inter, rmax)
        w_inter = jnp.exp(inter - m_t)
        floor = jnp.exp(-m_t)
        last = (col == (row // seq_rows) * seq_rows + (seq_rows - 1)).astype(F32)
        m_new = _dot_f32(last, m_t)
        w_last = jnp.exp(btl_h - bt_h + g - m_new)
        w_prev = jnp.exp(inter - m_new)
        n_rows = n0_ref[0]

        den = jnp.zeros((L, W_GATE), F32)
        for h in range(H_C):
            hs = slice(h * DH_C, (h + 1) * DH_C)
            q = zm_ref[:, OFF_Q + h * DH_C:OFF_Q + (h + 1) * DH_C]
            k = zm_ref[:, OFF_K + h * DH_C:OFF_K + (h + 1) * DH_C]
            v = zm_ref[:, OFF_V + h * DH_C:OFF_V + (h + 1) * DH_C]
            s = _dot_nt(q, k) * (DH_C ** -0.5) * jnp.exp(dlogs[h] - m_t[:, h:h + 1])
            num_s[:, hs] = _dot(s.astype(BF16), v)
            qn_all = _dot_nt(q, n_rows[:, hs].astype(BF16))
            qn = jnp.sum(jnp.where(row == col, qn_all, 0.0), axis=-1, keepdims=True)
            den_h = jnp.sum(s, axis=-1, keepdims=True) + w_inter[:, h:h + 1] * qn
            den = jnp.where(lane == h, den_h, den)
            vw_s[h] = (v.astype(F32) * w_last[:, h:h + 1]).astype(BF16)
        qc_s[...] = jnp.zeros(qc_s.shape, F32)
        winter_s[...] = w_inter
        rinv_s[...] = 1.0 / jnp.maximum(jnp.abs(den), floor)
        wlt_s[...] = w_last.T[0:16, :]
        wprev_s[...] = w_prev
        mnew_s[...] = m_new

    spp = c0_ref.shape[1]
    tile = spp * seq_rows
    r0 = pl.multiple_of(j * tile, tile)
    rid = lax.broadcasted_iota(jnp.int32, (tile, 1), 0)
    lane_l = lax.broadcasted_iota(jnp.int32, (1, L), 1)
    row_l = lax.broadcasted_iota(jnp.int32, (L, 1), 0)
    for u in range(spp):
        sq = j * spp + u
        last_row = sq * seq_rows + (seq_rows - 1)
        seq_lanes = (lane_l // seq_rows) == sq
        seq_rows_mask = (row_l // seq_rows) == sq
        wprev_row = wprev_s[pl.ds(last_row, 1), :]
        for h in range(H_C):
            hs = slice(h * DH_C, (h + 1) * DH_C)
            c_old = c0_ref[0, u, h]
            q16 = zm_ref[pl.ds(r0, tile), OFF_Q + h * DH_C:OFF_Q + (h + 1) * DH_C]
            r = _dot_nt(q16, c_old.astype(BF16))
            qc_s[pl.ds(r0, tile), hs] = jnp.where((rid // seq_rows) == u, r, qc_s[pl.ds(r0, tile), hs])
            k = zm_ref[:, OFF_K + h * DH_C:OFF_K + (h + 1) * DH_C]
            vw = jnp.where(seq_rows_mask, vw_s[h], jnp.zeros((L, DH_C), BF16))
            wl = jnp.where(seq_lanes, wlt_s[...], 0.0)
            w_prev = wprev_row[:, h:h + 1]
            c_out_ref[0, u, h] = w_prev * c_old + _dot_tn(vw, k) * (DH_C ** -0.5)
            ksum = _dot(wl.astype(BF16), k)[h:h + 1, :]
            n_old = n0_ref[0, pl.ds(sq * seq_rows, 1), hs]
            n_out_ref[u, h:h + 1, :] = w_prev * n_old + ksum * (DH_C ** -0.5)
        m_out_ref[u] = mnew_s[pl.ds(last_row, 1), :]

    @pl.when(j == nseq // spp - 1)
    def _post():
        for h in range(H_C):
            hs = slice(h * DH_C, (h + 1) * DH_C)
            hh = (num_s[:, hs] + qc_s[:, hs] * winter_s[:, h:h + 1]) * rinv_s[:, h:h + 1]
            hn = hh * lax.rsqrt(jnp.mean(hh * hh, axis=-1, keepdims=True) + EPS) * mng_ref[:, hs]
            o = zm_ref[:, OFF_O + h * DH_C:OFF_O + (h + 1) * DH_C].astype(F32)
            p_ref[:, 2 * D_A + h * DH_C:2 * D_A + (h + 1) * DH_C] = (jax.nn.sigmoid(o) * hn).astype(BF16)


def _mixer(zm, gates, ca0, cb0, c0, n0, m0, caw, cbw, cbb, lng, lnb, bif, mng, c_acc, *,
           row_block0, n_groups, n_chunks, seq_rows, carry, layer, layer_in, layer_out):
    L = CHUNK
    nseq = L // seq_rows
    n_state = n_groups if carry else n_chunks * nseq
    rows = n_groups * n_chunks * L

    if carry:
        pair = PROMPT_PAIR
        spp = pair

        def seq_of(b, t, j):
            return b

        def conv_of(b, t, j):
            return b
        rows_blk = 1

        cps = PROMPT_CHUNKS_PER_STEP
        assert n_chunks % cps == 0 and row_block0 % cps == 0
        tok_rows = cps * L

        def tok(u):
            return lambda b, t, j: (row_block0 // cps + (b * pair + u) * (n_chunks // cps) + t, 0)

        def conv_in(taps, ch):
            return pl.BlockSpec((1, pair, taps - 1, ch), lambda *g: (layer_in, conv_of(*g), 0, 0))

        def conv_out(taps, ch):
            return (pl.BlockSpec((pair, taps - 1, ch), lambda *g: (conv_of(*g), 0, 0)),
                    jax.ShapeDtypeStruct((n_state, taps - 1, ch), F32))
        p_spec = pl.BlockSpec((pair, tok_rows, W_P), lambda b, t, j: (b, t, 0))
        p_shape = jax.ShapeDtypeStruct((n_groups, n_chunks * L, W_P), BF16)
        grid = (n_groups // pair, n_chunks // cps, 1)
        lead = (pair,)
    else:
        pair = 1
        tok_rows = L
        spp = SAMPLE_ROWS_PER_STEP // seq_rows

        def tok(u):
            return lambda b, t, j: (row_block0 + b * n_chunks + t, 0)
        p_spec = pl.BlockSpec((L, W_P), lambda b, t, j: (b * n_chunks + t, 0))
        p_shape = jax.ShapeDtypeStruct((rows, W_P), BF16)
        grid = (n_groups, n_chunks, nseq // spp)
        lead = ()

        def seq_of(b, t, j):
            return t * (nseq // spp) + j

        def conv_of(b, t, j):
            return t
        rows_blk = L

        def conv_in(taps, ch):
            return pl.BlockSpec((None, taps - 1, nseq, ch), lambda *g: (layer_in, 0, conv_of(*g), 0))

        def conv_out(taps, ch):
            return (pl.BlockSpec((taps - 1, nseq, ch), lambda *g: (0, conv_of(*g), 0)),
                    jax.ShapeDtypeStruct((taps - 1, n_state, ch), F32))

    has_acc = c_acc is not None
    kern = functools.partial(_mixer_kernel, seq_rows=seq_rows, carry=carry, has_acc=has_acc)
    in_specs = [pl.BlockSpec((tok_rows, W_MAIN), tok(u)) for u in range(pair)]
    in_specs += [pl.BlockSpec((tok_rows, W_GATE), tok(u)) for u in range(pair)]
    in_specs += [
        conv_in(K_A, D_A),
        conv_in(K_B, D_B),
        pl.BlockSpec((1, spp, H_C, DH_C, DH_C), lambda *g: (layer_in, seq_of(*g), 0, 0, 0)),
        pl.BlockSpec((None, pair, rows_blk, D_MODEL), lambda *g: (layer_in, conv_of(*g), 0, 0)),
        pl.BlockSpec((None, pair, rows_blk, W_GATE), lambda *g: (layer_in, conv_of(*g), 0, 0)),
        _resident((K_A, D_A), layer),
        _resident((K_B, D_B), layer),
        _resident((1, D_B), layer),
        _resident((1, D_B), layer),
        _resident((1, D_B), layer),
        _resident((1, W_GATE), layer),
        _resident((1, D_MODEL), layer),
    ]
    args = [zm] * pair + [gates] * pair + [ca0, cb0, c0, n0, m0, caw, cbw, cbb, lng, lnb, bif, mng]
    aliases = {}
    if has_acc:
        in_specs.append(pl.BlockSpec(memory_space=pl.ANY))
        args.append(c_acc)
        aliases = {len(args) - 1: 3}
    outs = pl.pallas_call(
        kern,
        grid=grid,
        in_specs=in_specs,
        out_specs=[
            p_spec,
            conv_out(K_A, D_A)[0],
            conv_out(K_B, D_B)[0],
            pl.BlockSpec((1, spp, H_C, DH_C, DH_C), lambda *g: (layer_out, seq_of(*g), 0, 0, 0)),
            pl.BlockSpec((spp, H_C, DH_C), lambda *g: (seq_of(*g), 0, 0)),
            pl.BlockSpec((spp, 1, W_GATE), lambda *g: (seq_of(*g), 0, 0)),
        ],
        out_shape=[
            p_shape,
            conv_out(K_A, D_A)[1],
            conv_out(K_B, D_B)[1],
            jax.ShapeDtypeStruct((DEPTH, n_state, H_C, DH_C, DH_C), F32),
            jax.ShapeDtypeStruct((n_state, H_C, DH_C), F32),
            jax.ShapeDtypeStruct((n_state, 1, W_GATE), F32),
        ],
        input_output_aliases=aliases,
        scratch_shapes=[
            pltpu.VMEM(lead + (8 + L, D_A), F32),
            pltpu.VMEM(lead + (32 + L, D_B), F32),
            pltpu.VMEM(lead + (8, 24 + L, D_B), F32),
            pltpu.VMEM(lead + (L, D_A), F32),
            pltpu.VMEM(lead + (L, D_B), F32),
            pltpu.VMEM(lead + (H_C, DH_C, DH_C), F32),
            pltpu.VMEM(lead + (8, D_MODEL), F32),
            pltpu.VMEM(lead + (8, W_GATE), F32),
        ] if carry else [
            pltpu.VMEM((L, D_A), F32),
            pltpu.VMEM((L, D_B), F32),
            pltpu.VMEM((L, D_MODEL), F32),
            pltpu.VMEM((L, D_MODEL), F32),
            pltpu.VMEM((H_C, L, DH_C), BF16),
            pltpu.VMEM((L, W_GATE), F32),
            pltpu.VMEM((L, W_GATE), F32),
            pltpu.VMEM((16, L), F32),
            pltpu.VMEM((L, W_GATE), F32),
            pltpu.VMEM((L, W_GATE), F32),
        ],
        compiler_params=_params(("arbitrary", "arbitrary", "arbitrary")),
        name="mixer_prompt" if carry else "mixer_sample",
    )(*args)
    outs = list(outs)
    outs[0] = outs[0].reshape(rows, W_P)
    return outs


def _outproj_kernel(xa_ref, xb_ref, pp_ref, ps_ref, zg_ref, wa_ref, wb_ref, wc_ref, wo_ref, gx_ref, wq_ref,
                    x1_ref, q_ref, *, n_prompt_tiles, split):
    i = pl.program_id(0)
    x = _x_tile(xa_ref, xb_ref, n_prompt_tiles, split)
    p = jnp.where(i >= n_prompt_tiles, ps_ref[...], pp_ref[...])
    y_a = _dot(p[:, 0:D_A], wa_ref[...])
    y_b = _dot(p[:, D_A:D_A + D_B], wb_ref[...])
    y_c = _dot(p[:, D_A + D_B:], wc_ref[...])
    u = (jax.nn.sigmoid(zg_ref[:, 0:D_MODEL].astype(F32)) * y_a
         + jax.nn.sigmoid(zg_ref[:, D_MODEL:2 * D_MODEL].astype(F32)) * y_b
         + jax.nn.sigmoid(zg_ref[:, 2 * D_MODEL:].astype(F32)) * y_c)
    x1 = x + _dot(u.astype(BF16), wo_ref[...])
    x1_ref[...] = x1
    q_ref[...] = _dot(_rmsnorm(x1, gx_ref[...]).astype(BF16), wq_ref[...]).astype(BF16)


def _outproj(xa, xb, sample_block, p_p, p_s, zg, wa, wb, wc, wo, gx, wq, layer):
    tm = ROW_TILE
    npt = p_p.shape[0] // tm
    n = (npt + 1) * tm
    split = xa is not xb
    return pl.pallas_call(
        functools.partial(_outproj_kernel, n_prompt_tiles=npt, split=split),
        grid=(n // tm,),
        in_specs=_x_specs(tm, npt, sample_block, split) + [
            pl.BlockSpec((tm, W_P), lambda i: (jnp.minimum(i, npt - 1), 0)),
            pl.BlockSpec((tm, W_P), lambda i: (0, 0)),
            pl.BlockSpec((tm, W_MERGE), lambda i: (i, 0)),
            _resident((D_A, D_MODEL), layer),
            _resident((D_B, D_MODEL), layer),
            _resident((D_MODEL, D_MODEL), layer),
            _resident((D_MODEL, D_MODEL), layer),
            _resident((1, D_MODEL), layer),
            _resident((D_MODEL, D_MODEL), layer),
        ],
        out_specs=[
            pl.BlockSpec((tm, D_MODEL), lambda i: (i, 0)),
            pl.BlockSpec((tm, D_MODEL), lambda i: (i, 0)),
        ],
        out_shape=[
            jax.ShapeDtypeStruct((n, D_MODEL), F32),
            jax.ShapeDtypeStruct((n, D_MODEL), BF16),
        ],
        compiler_params=_params(("parallel",)),
        name="outproj",
    )(xa, xb, p_p, p_s, zg, wa, wb, wc, wo, gx, wq)


def _memkv_kernel(mem_ref, g_ref, w_ref, k_ref, v_ref, kb_ref, vb_ref):
    h = _rmsnorm(mem_ref[...], g_ref[0]).astype(BF16)
    hd = HX * DX
    k = _dot(h, w_ref[0, :, 0:hd])
    v = _dot(h, w_ref[0, :, hd:])
    for bb in range(k_ref.shape[1]):
        k_ref[0, bb] = k[bb * N_MEM:(bb + 1) * N_MEM].reshape(N_MEM, HX, DX)
        v_ref[0, bb] = v[bb * N_MEM:(bb + 1) * N_MEM].reshape(N_MEM, HX, DX)
    kb_ref[0] = k.astype(BF16)
    vb_ref[0] = v.astype(BF16)


def _memkv(mem, g, w):
    n = mem.shape[0]
    tm = ROW_TILE
    hd = HX * DX
    per = tm // N_MEM
    o_spec = pl.BlockSpec((1, tm, hd), lambda l, i: (l, i, 0))
    o5_spec = pl.BlockSpec((1, per, N_MEM, HX, DX), lambda l, i: (l, i, 0, 0, 0))
    return pl.pallas_call(
        _memkv_kernel,
        grid=(DEPTH, n // tm),
        in_specs=[
            pl.BlockSpec((tm, D_MODEL), lambda l, i: (i, 0)),
            pl.BlockSpec((1, 1, D_MODEL), lambda l, i: (l, 0, 0)),
            pl.BlockSpec((1, D_MODEL, 2 * hd), lambda l, i: (l, 0, 0)),
        ],
        out_specs=[o5_spec, o5_spec, o_spec, o_spec],
        out_shape=[
            jax.ShapeDtypeStruct((DEPTH, n // N_MEM, N_MEM, HX, DX), F32),
            jax.ShapeDtypeStruct((DEPTH, n // N_MEM, N_MEM, HX, DX), F32),
            jax.ShapeDtypeStruct((DEPTH, n, hd), BF16),
            jax.ShapeDtypeStruct((DEPTH, n, hd), BF16),
        ],
        compiler_params=_params(("arbitrary", "arbitrary")),
        name="memkv",
    )(mem, g, w)


def _attend(q, k, v):
    s = _dot_nt(q, k) * (DX ** -0.5)
    e = jnp.exp(s - jnp.max(s, axis=-1, keepdims=True))
    return _dot(e.astype(BF16), v) * (1.0 / jnp.sum(e, axis=-1, keepdims=True))


XS_ROWS = 32
XATTN_TILE = 1024


def _xattn_kernel(q_ref, k_ref, v_ref, qs_ref, ks_ref, vs_ref, o_ref, os_ref, *, seq_rows):
    for h in range(HX):
        hs = slice(h * DX, (h + 1) * DX)
        o_ref[:, hs] = _attend(q_ref[:, hs], k_ref[0, :, hs], v_ref[0, :, hs]).astype(BF16)
    _xattn_short(qs_ref, ks_ref, vs_ref, os_ref, seq_rows)


def _xattn(q, kb, vb, k_cache, v_cache, layer, *, n_groups, rows_per_group, seq_rows):
    tq = XATTN_TILE
    nt = rows_per_group // tq
    hd = HX * DX
    per = XS_ROWS // seq_rows
    n_seq = k_cache.shape[1]
    assert n_groups * nt * per == n_seq
    qs_block0 = n_groups * rows_per_group // XS_ROWS
    kv_spec = pl.BlockSpec((1, N_MEM, hd), lambda b, t: (layer, b, 0))
    cache_spec = pl.BlockSpec((1, per, N_MEM, HX, DX), lambda b, t: (layer, b * nt + t, 0, 0, 0))
    return pl.pallas_call(
        functools.partial(_xattn_kernel, seq_rows=seq_rows),
        grid=(n_groups, nt),
        in_specs=[pl.BlockSpec((tq, hd), lambda b, t: (b * nt + t, 0)), kv_spec, kv_spec,
                  pl.BlockSpec((XS_ROWS, hd), lambda b, t: (qs_block0 + b * nt + t, 0)),
                  cache_spec, cache_spec],
        out_specs=[pl.BlockSpec((tq, hd), lambda b, t: (b * nt + t, 0)),
                   pl.BlockSpec((XS_ROWS, hd), lambda b, t: (b * nt + t, 0))],
        out_shape=[jax.ShapeDtypeStruct((n_groups * rows_per_group, hd), BF16),
                   jax.ShapeDtypeStruct((n_seq * seq_rows, hd), BF16)],
        compiler_params=_params(("parallel", "parallel")),
        name="xattn",
    )(q, kb, vb, q, k_cache, v_cache)


def _xattn_short(q_ref, k_ref, v_ref, o_ref, seq_rows):
    rows = q_ref.shape[0]
    grp = HX * seq_rows
    n = HX * rows
    assert grp == 16
    qs = jnp.concatenate([q_ref[:, h * DX:(h + 1) * DX] for h in range(HX)], axis=0)
    ri = lax.broadcasted_iota(jnp.int32, (n, n), 0)
    ci = lax.broadcasted_iota(jnp.int32, (n, n), 1)
    by_seq = ci == ((ri % grp) // seq_rows) * rows + (ri // grp) * seq_rows + ri % seq_rows
    by_head = ci == ((ri % rows) // seq_rows) * grp + (ri // rows) * seq_rows + ri % seq_rows
    qp = _dot(by_seq.astype(BF16), qs).astype(BF16)
    rid = lax.broadcasted_iota(jnp.int32, (grp, 1), 0)
    cid = lax.broadcasted_iota(jnp.int32, (grp, N_MEM * HX), 1)
    own_head = (cid % HX) == (rid // seq_rows)
    outs = []
    for e in range(rows // seq_rows):
        k2 = k_ref[0, e].reshape(N_MEM * HX, DX).astype(BF16)
        v2 = v_ref[0, e].reshape(N_MEM * HX, DX).astype(BF16)
        s = jnp.where(own_head, _dot_nt(qp[e * grp:(e + 1) * grp], k2) * (DX ** -0.5), -jnp.inf)
        p = jnp.exp(s - jnp.max(s, axis=-1, keepdims=True))
        o = _dot(p.astype(BF16), v2) * (1.0 / jnp.sum(p, axis=-1, keepdims=True))
        outs.append(o.astype(BF16))
    ob = _dot(by_head.astype(BF16), jnp.concatenate(outs, axis=0)).astype(BF16)
    for h in range(HX):
        o_ref[:, h * DX:(h + 1) * DX] = ob[h * rows:(h + 1) * rows]


FF_BLOCK = 256


def _ffn_kernel(x_ref, cp_ref, cs_ref, wxo_ref, gf_ref, wg_ref, wu_ref, wout_ref, gfin_ref,
                *rest, n_prompt_tiles, final):
    act_s = rest[-1]
    i = pl.program_id(0)
    ctx = jnp.where(i >= n_prompt_tiles, cs_ref[...], cp_ref[...])
    x2 = x_ref[...] + _dot(ctx, wxo_ref[...])
    h = _rmsnorm(x2, gf_ref[...]).astype(BF16)
    for jb in range(D_FF // FF_BLOCK):
        sl = slice(jb * FF_BLOCK, (jb + 1) * FF_BLOCK)
        gate = _dot(h, wg_ref[:, sl])
        up = _dot(h, wu_ref[:, sl])
        act_s[:, sl] = (gate * jax.nn.sigmoid(gate) * up).astype(BF16)
    x3 = x2 + _dot(act_s[...], wout_ref[...])
    if final:
        y = _rmsnorm(x3, gfin_ref[...])
        yp_ref, ys_ref = rest[0], rest[1]

        @pl.when(i < n_prompt_tiles)
        def _():
            yp_ref[...] = y

        @pl.when(i >= n_prompt_tiles)
        def _():
            ys_ref[...] = y
    else:
        rest[0][...] = x3


def _ffn(x, c_p, c_s, wxo, gf, w_in, wout, gfin, layer, *, final):
    n = x.shape[0]
    tm = ROW_TILE
    npt = c_p.shape[0] // tm
    if final:
        assert n - npt * tm == tm
        out_specs = [pl.BlockSpec((tm, D_MODEL), lambda i: (jnp.minimum(i, npt - 1), 0)),
                     pl.BlockSpec((tm, D_MODEL), lambda i: (0, 0))]
        out_shape = [jax.ShapeDtypeStruct((npt * tm, D_MODEL), F32),
                     jax.ShapeDtypeStruct((tm, D_MODEL), F32)]
        sem = ("arbitrary",)
    else:
        out_specs = pl.BlockSpec((tm, D_MODEL), lambda i: (i, 0))
        out_shape = jax.ShapeDtypeStruct((n, D_MODEL), F32)
        sem = ("parallel",)
    return pl.pallas_call(
        functools.partial(_ffn_kernel, n_prompt_tiles=npt, final=final),
        grid=(n // tm,),
        in_specs=[
            pl.BlockSpec((tm, D_MODEL), lambda i: (i, 0)),
            pl.BlockSpec((tm, D_MODEL), lambda i: (jnp.minimum(i, npt - 1), 0)),
            pl.BlockSpec((tm, D_MODEL), lambda i: (0, 0)),
            _resident((D_MODEL, D_MODEL), layer),
            _resident((1, D_MODEL), layer),
            _resident((D_MODEL, D_FF), layer, col=0),
            _resident((D_MODEL, D_FF), layer, col=1),
            _resident((D_FF, D_MODEL), layer),
            _resident((1, D_MODEL)),
        ],
        out_specs=out_specs,
        out_shape=out_shape,
        scratch_shapes=[pltpu.VMEM((tm, D_FF), BF16)],
        compiler_params=_params(sem),
        name="ffn",
    )(x, c_p, c_s, wxo, gf, w_in, w_in, wout, gfin)


def kernel(x_prompt, x_sample, state_conv_a, state_conv_b, state_mlstm_c, state_mlstm_n, state_mlstm_m,
           cache_mem_k, cache_mem_v, mem_prompt, norm_mix_g, w_in, b_if, conv_a_w, w_out_a, conv_b_w,
           conv_b_b, ln_b_g, ln_b_b, w_out_b, mlstm_norm_g, w_out_c, w_o, norm_x_g, norm_mem_g, w_xq,
           w_xkv, w_xo, norm_ffn_g, w_ffn_in, w_ffn_out, final_norm_g):
    bp, tp, d = x_prompt.shape
    bs, ts, _ = x_sample.shape
    n_p = bp * tp
    n_s = bs * ts
    hd = HX * DX

    gate_lo = W_MAIN
    gate_hi = W_MAIN + 2 * H_C
    w_inb = w_in.astype(BF16)
    w_gate = jnp.pad(w_inb[:, :, gate_lo:gate_hi], ((0, 0), (0, 0), (0, W_GATE - 2 * H_C)))
    w_merge = w_inb[:, :, gate_hi:]
    bif = jnp.pad(b_if, ((0, 0), (0, W_GATE - 2 * H_C)))[:, None, :]
    w_a = w_out_a.astype(BF16)
    w_b = w_out_b.astype(BF16)
    w_c = w_out_c.astype(BF16)
    w_ob = w_o.astype(BF16)
    w_q = w_xq.astype(BF16)
    w_kv = w_xkv.astype(BF16)
    w_xob = w_xo.astype(BF16)
    w_fi = w_ffn_in.astype(BF16)
    w_fo = w_ffn_out.astype(BF16)
    g_mix = norm_mix_g[:, None, :]
    g_x = norm_x_g[:, None, :]
    g_ffn = norm_ffn_g[:, None, :]
    small = (conv_a_w, conv_b_w, conv_b_b[:, None, :], ln_b_g[:, None, :], ln_b_b[:, None, :],
             bif, mlstm_norm_g[:, None, :])

    x = (x_prompt.reshape(n_p, d), x_sample.reshape(n_s, d), 0)

    mem_k, mem_v, mem_kb, mem_vb = _memkv(mem_prompt.reshape(bp * N_MEM, d), norm_mem_g[:, None, :], w_kv)

    zeros_ca = jnp.zeros((1, bp, K_A - 1, D_A), F32)
    zeros_cb = jnp.zeros((1, bp, K_B - 1, D_B), F32)
    zeros_c = jnp.zeros((1, bp, H_C, DH_C, DH_C), F32)
    zeros_n = jnp.zeros((1, bp, 1, H_C * DH_C), F32)
    zeros_m = jnp.zeros((1, bp, 1, W_GATE), F32)

    n_chunks_s = n_s // CHUNK
    n0_rows = jnp.repeat(state_mlstm_n.reshape(DEPTH, bs, H_C * DH_C), ts, axis=1)
    n0_rows = n0_rows.reshape(DEPTH, n_chunks_s, CHUNK, H_C * DH_C)
    m0_rows = jnp.repeat(jnp.pad(state_mlstm_m, ((0, 0), (0, 0), (0, W_GATE - H_C))), ts, axis=1)
    m0_rows = m0_rows.reshape(DEPTH, n_chunks_s, CHUNK, W_GATE)

    conv_a_tm = jnp.transpose(state_conv_a, (0, 2, 1, 3))
    conv_b_tm = jnp.transpose(state_conv_b, (0, 2, 1, 3))

    outs = {k: [] for k in ("pa", "pb", "pn", "pm", "sa", "sb", "sn", "sm")}
    c1 = c2 = None
    for l in range(DEPTH):
        zm, zg, gates = _inproj(*x, g_mix, w_inb, w_merge, w_gate, l)
        p_p, a1, b1, c1, n1, m1 = _mixer(
            zm, gates, zeros_ca, zeros_cb, zeros_c, zeros_n, zeros_m, *small, c1,
            row_block0=0, n_groups=bp, n_chunks=tp // CHUNK, seq_rows=CHUNK, carry=True,
            layer=l, layer_in=0, layer_out=l)
        p_s, a2, b2, c2, n2, m2 = _mixer(
            zm, gates, conv_a_tm, conv_b_tm, state_mlstm_c, n0_rows, m0_rows, *small, c2,
            row_block0=n_p // CHUNK, n_groups=1, n_chunks=n_chunks_s, seq_rows=ts, carry=False,
            layer=l, layer_in=l, layer_out=l)
        x1, qx = _outproj(*x, p_p, p_s, zg, w_a, w_b, w_c, w_ob, g_x, w_q, l)
        c_p, c_s = _xattn(qx, mem_kb, mem_vb, cache_mem_k, cache_mem_v, l,
                          n_groups=bp, rows_per_group=tp, seq_rows=ts)
        y = _ffn(x1, c_p, c_s, w_xob, g_ffn, w_fi, w_fo, final_norm_g[None, :], l, final=(l == DEPTH - 1))
        x = (y, y, n_p // ROW_TILE)
        outs["pa"].append(a1); outs["pb"].append(b1); outs["pn"].append(n1); outs["pm"].append(m1)
        outs["sa"].append(a2); outs["sb"].append(b2); outs["sn"].append(n2); outs["sm"].append(m2)

    y_prompt = y[0].reshape(bp, tp, d)
    y_sample = y[1].reshape(bs, ts, d)
    st = {k: jnp.stack(v) for k, v in outs.items()}
    return (y_prompt, y_sample,
            st["pa"], st["pb"], c1, st["pn"], st["pm"][:, :, 0, :H_C],
            mem_k, mem_v,
            jnp.transpose(st["sa"], (0, 2, 1, 3)), jnp.transpose(st["sb"], (0, 2, 1, 3)), c2,
            st["sn"], st["sm"][:, :, 0, :H_C])
```
